```python
import jax, jax.numpy as jnp
from jax import lax
import numpy as np

D_MODEL = 1024
BATCH = 16
SEQ = 2048
DEPTH = 2

GRID_W = 64
HEAD_DIM = 64
Q_BLOCK = 128
NORM_EPS = 1e-6
NEG_INF = -1e30
ROPE_THETA = 500000.0
PARTIAL_ROPE_DIM = HEAD_DIM // 4
GQA_Q_HEADS = 8
GQA_KV_HEADS = 2
GQA_GROUP = GQA_Q_HEADS // GQA_KV_HEADS
AXIAL_THETA = 10000.0
MLA_HEADS = 8
MLA_Q_RANK = 256
MLA_KV_RANK = 128
MLA_NOPE_DIM = 64
MLA_ROPE_DIM = 32
MLA_V_DIM = 64
MLA_ROPE_THETA = 10000.0
AB_SPLITS = (GQA_Q_HEADS * HEAD_DIM,
             GQA_Q_HEADS * HEAD_DIM + GQA_KV_HEADS * HEAD_DIM,
             GQA_Q_HEADS * HEAD_DIM + 2 * GQA_KV_HEADS * HEAD_DIM,
             GQA_Q_HEADS * HEAD_DIM + 2 * GQA_KV_HEADS * HEAD_DIM + MLA_Q_RANK,
             GQA_Q_HEADS * HEAD_DIM + 2 * GQA_KV_HEADS * HEAD_DIM + MLA_Q_RANK + MLA_KV_RANK)
AB_IN_WIDTH = AB_SPLITS[-1] + MLA_ROPE_DIM
AB_OUT_WIDTH = GQA_Q_HEADS * HEAD_DIM + MLA_HEADS * MLA_V_DIM
DIL_CONFIGS = ((128, 1), (512, 4), (2048, 16))
DIL_GROUPS = len(DIL_CONFIGS)
DIL_HEADS = 8
DIL_IN_WIDTH = DIL_GROUPS * 3 * DIL_HEADS * HEAD_DIM
DIL_OUT_WIDTH = DIL_HEADS * HEAD_DIM
D_FF = 3584
N_EXPERTS = 8
TOP_K = 2
D_FF_EXPERT = 3584
MOE_BLOCK = 256
PLE_DIM = 256
N_EVEN = (DEPTH + 1) // 2
N_ODD = DEPTH // 2

kernel_name = 'hybrid_gqa_mla_dilated_moe_encoder'


def rms_norm(x, g):
    xf = x.astype(jnp.float32)
    y = xf * lax.rsqrt(jnp.mean(xf * xf, axis=-1, keepdims=True) + NORM_EPS)
    return (y * g.astype(jnp.float32)).astype(x.dtype)


def rotary_tables(pos, dim, theta):
    exponent = jnp.arange(0, dim, 2, dtype=jnp.float32) / dim
    inv_freq = jnp.power(jnp.float32(theta), -exponent)
    ang = pos.astype(jnp.float32)[:, None] * inv_freq[None, :]
    return jnp.cos(ang), jnp.sin(ang)


def apply_rotary(x, cos, sin):
    half = x.shape[-1] // 2
    shape = (cos.shape[0],) + (1,) * (x.ndim - 3) + (half,)
    c = cos.reshape(shape)
    s_ = sin.reshape(shape)
    xf = x.astype(jnp.float32)
    x1, x2 = xf[..., :half], xf[..., half:]
    return jnp.concatenate([x1 * c - x2 * s_, x2 * c + x1 * s_], axis=-1).astype(x.dtype)


def axial_rotary(x, row_cs, col_cs):
    half = x.shape[-1] // 2
    return jnp.concatenate([apply_rotary(x[..., :half], *row_cs),
                            apply_rotary(x[..., half:], *col_cs)], axis=-1)


def dense_blocked_attention(q, k, v, scale):
    b, s, hk, g, dk = q.shape
    nblk = s // Q_BLOCK
    qb = jnp.moveaxis(q.reshape(b, nblk, Q_BLOCK, hk, g, dk), 1, 0)

    def one_block(qblk):
        sc = jnp.einsum('bqhgd,bkhd->bhgqk', qblk, k, preferred_element_type=jnp.float32) * scale
        probs = jax.nn.softmax(sc, axis=-1)
        return jnp.einsum('bhgqk,bkhe->bqhge', probs.astype(v.dtype), v)

    o = lax.map(one_block, qb)
    return jnp.moveaxis(o, 0, 1).reshape(b, s, hk, g, v.shape[-1])


def dilated_window_attention(q, k, v, dilation, half, scale):
    b, s, h, dh = q.shape
    L = s // dilation
    nb = -(-L // half)
    Lp = nb * half
    z = b * dilation

    def residues(x):
        e = x.shape[-1]
        return x.reshape(b, L, dilation, h, e).transpose(0, 2, 1, 3, 4).reshape(z, L, h, e)

    def windows(x):
        e = x.shape[-1]
        xp = jnp.pad(residues(x), ((0, 0), (half, Lp - L + half), (0, 0), (0, 0)))
        xp = xp.reshape(z, nb + 2, half, h, e)
        return jnp.concatenate([xp[:, :-2], xp[:, 1:-1], xp[:, 2:]], axis=2)

    qs = jnp.pad(residues(q), ((0, 0), (0, Lp - L), (0, 0), (0, 0))).reshape(z, nb, half, h, dh)
    kw = windows(k)
    vw = windows(v)
    qi = jnp.arange(nb)[:, None] * half + jnp.arange(half)[None, :]
    ki = jnp.arange(nb)[:, None] * half - half + jnp.arange(3 * half)[None, :]
    mask = ((jnp.abs(qi[:, :, None] - ki[:, None, :]) <= half)
            & (ki[:, None, :] >= 0) & (ki[:, None, :] < L))
    sc = jnp.einsum('znqhd,znkhd->znhqk', qs, kw, preferred_element_type=jnp.float32) * scale
    sc = jnp.where(mask[None, :, None], sc, NEG_INF)
    m = jnp.max(sc, axis=-1, keepdims=True)
    pexp = jnp.exp(sc - m)
    l = jnp.sum(pexp, axis=-1, keepdims=True)
    o = jnp.einsum('znhqk,znkhd->znqhd', (pexp / l).astype(v.dtype), vw)
    lse = jnp.swapaxes((m + jnp.log(l))[..., 0], 2, 3)

    def unresidue(y):
        e = y.shape[-1]
        y = y.reshape(b, dilation, Lp, h, e)[:, :, :L]
        return y.transpose(0, 2, 1, 3, 4).reshape(b, s, h, e)

    o = unresidue(o.reshape(z, Lp, h, dh))
    lse = unresidue(lse.reshape(z, Lp, h, 1))[..., 0]
    return o, lse


def mixer_gqa_mla(a, w_in, gqa_qn, gqa_kn, mla_qn, w_uq, mla_kvn, w_ukv, w_out, row_cs, col_cs, mla_cs):
    b, s, _ = a.shape
    zin = a @ w_in
    q_a, k_a, v_a, c_q, c_kv, k_r = jnp.split(zin, AB_SPLITS, axis=-1)
    q_a = rms_norm(q_a.reshape(b, s, GQA_KV_HEADS, GQA_GROUP, HEAD_DIM), gqa_qn)
    k_a = rms_norm(k_a.reshape(b, s, GQA_KV_HEADS, HEAD_DIM), gqa_kn)
    v_a = v_a.reshape(b, s, GQA_KV_HEADS, HEAD_DIM)
    q_a = axial_rotary(q_a, row_cs, col_cs)
    k_a = axial_rotary(k_a, row_cs, col_cs)
    o_a = dense_blocked_attention(q_a, k_a, v_a, HEAD_DIM ** -0.5).reshape(b, s, GQA_Q_HEADS * HEAD_DIM)
    q_m = (rms_norm(c_q, mla_qn) @ w_uq).reshape(b, s, MLA_HEADS, MLA_NOPE_DIM + MLA_ROPE_DIM)
    q_m = jnp.concatenate([q_m[..., :MLA_NOPE_DIM], apply_rotary(q_m[..., MLA_NOPE_DIM:], *mla_cs)], axis=-1)
    kv = (rms_norm(c_kv, mla_kvn) @ w_ukv).reshape(b, s, MLA_HEADS, MLA_NOPE_DIM + MLA_V_DIM)
    k_rope = apply_rotary(k_r.reshape(b, s, 1, MLA_ROPE_DIM), *mla_cs)
    k_m = jnp.concatenate([kv[..., :MLA_NOPE_DIM],
                           jnp.broadcast_to(k_rope, (b, s, MLA_HEADS, MLA_ROPE_DIM))], axis=-1)
    v_m = kv[..., MLA_NOPE_DIM:]
    o_m = dense_blocked_attention(q_m[:, :, :, None, :], k_m, v_m,
                                  (MLA_NOPE_DIM + MLA_ROPE_DIM) ** -0.5).reshape(b, s, MLA_HEADS * MLA_V_DIM)
    return jnp.concatenate([o_a, o_m], axis=-1) @ w_out


def mixer_dilated(a, w_qkv, w_out, part_cs):
    b, s, _ = a.shape
    zin = (a @ w_qkv).reshape(b, s, DIL_GROUPS, 3, DIL_HEADS, HEAD_DIM)

    def partial_rope(x):
        return jnp.concatenate([apply_rotary(x[..., :PARTIAL_ROPE_DIM], *part_cs),
                                x[..., PARTIAL_ROPE_DIM:]], axis=-1)

    q = partial_rope(zin[:, :, :, 0])
    k = partial_rope(zin[:, :, :, 1])
    v = zin[:, :, :, 2]
    outs, lses = [], []
    for gi, (window, dilation) in enumerate(DIL_CONFIGS):
        o_g, lse_g = dilated_window_attention(q[:, :, gi], k[:, :, gi], v[:, :, gi],
                                              dilation, window // (2 * dilation), HEAD_DIM ** -0.5)
        outs.append(o_g)
        lses.append(lse_g)
    alpha = jax.nn.softmax(jnp.stack(lses, axis=0), axis=0)
    o = jnp.sum(alpha[..., None] * jnp.stack(outs, axis=0).astype(jnp.float32), axis=0).astype(a.dtype)
    return o.reshape(b, s, DIL_OUT_WIDTH) @ w_out


def swiglu(a, w_gate, w_up, w_down):
    return (jax.nn.silu(a @ w_gate) * (a @ w_up)) @ w_down


def moe_swiglu(a, w_router, w_gate, w_up, w_down):
    b, s, d = a.shape
    n = b * s
    nk = n * TOP_K
    af = a.reshape(n, d)
    logits = jnp.einsum('nd,de->ne', af, w_router, preferred_element_type=jnp.float32)
    top_logits, top_idx = lax.top_k(logits, TOP_K)
    gates = jax.nn.softmax(top_logits, axis=-1)
    flat_e = top_idx.reshape(nk).astype(jnp.int32)
    flat_tok = jnp.arange(nk, dtype=jnp.int32) // TOP_K
    flat_g = gates.reshape(nk)
    order = jnp.argsort(flat_e)
    sorted_e = flat_e[order]
    counts = jnp.bincount(flat_e, length=N_EXPERTS).astype(jnp.int32)
    padded = (counts + MOE_BLOCK - 1) // MOE_BLOCK * MOE_BLOCK
    end_pad = jnp.cumsum(padded)
    start_pad = end_pad - padded
    start_unp = jnp.cumsum(counts) - counts
    dest = start_pad[sorted_e] + jnp.arange(nk, dtype=jnp.int32) - start_unp[sorted_e]
    n_slots = nk + N_EXPERTS * MOE_BLOCK
    n_blk = n_slots // MOE_BLOCK
    slot_tok = jnp.zeros((n_slots,), jnp.int32).at[dest].set(flat_tok[order])
    slot_gate = jnp.zeros((n_slots,), jnp.float32).at[dest].set(flat_g[order])
    blk_start = jnp.arange(n_blk, dtype=jnp.int32) * MOE_BLOCK
    blk_e = jnp.minimum(jnp.searchsorted(end_pad, blk_start, side='right'), N_EXPERTS - 1)
    xs = af[slot_tok].reshape(n_blk, MOE_BLOCK, d)

    def expert_block(args):
        xb, e = args
        return (jax.nn.silu(xb @ w_gate[e]) * (xb @ w_up[e])) @ w_down[e]

    ys = lax.map(expert_block, (xs, blk_e)).reshape(n_slots, d)
    out = jnp.zeros((n, d), jnp.float32).at[slot_tok].add(ys.astype(jnp.float32) * slot_gate[:, None])
    return out.astype(a.dtype).reshape(b, s, d)


def setup_inputs(seed: int = 0) -> dict:
    key = jax.random.key(seed)
    ks = list(jax.random.split(key, 32))
    counter = [0]

    def nk():
        counter[0] += 1
        return ks[counter[0] - 1]

    def w(shape, fan_in):
        return jax.random.normal(nk(), shape, jnp.float32) * (fan_in ** -0.5)

    def g(shape):
        return 1.0 + 0.05 * jax.random.normal(nk(), shape, jnp.float32)

    D = D_MODEL
    return {
        'x': jax.random.normal(nk(), (BATCH, SEQ, D), jnp.float32),
        'p': jax.random.normal(nk(), (DEPTH, BATCH, SEQ, PLE_DIM), jnp.float32),
        'mix_pre_g': g((DEPTH, D)),
        'mix_post_g': g((DEPTH, D)),
        'ffn_pre_g': g((DEPTH, D)),
        'ffn_post_g': g((DEPTH, D)),
        'ple_w_proj': w((DEPTH, PLE_DIM, D), PLE_DIM),
        'ple_gate_norm_g': g((DEPTH, D)),
        'ple_w_gate': w((DEPTH, D, D), D),
        'ple_post_g': g((DEPTH, D)),
        'ab_w_in': w((N_EVEN, D, AB_IN_WIDTH), D),
        'gqa_q_norm_g': g((N_EVEN, HEAD_DIM)),
        'gqa_k_norm_g': g((N_EVEN, HEAD_DIM)),
        'mla_q_norm_g': g((N_EVEN, MLA_Q_RANK)),
        'mla_w_uq': w((N_EVEN, MLA_Q_RANK, MLA_HEADS * (MLA_NOPE_DIM + MLA_ROPE_DIM)), MLA_Q_RANK),
        'mla_kv_norm_g': g((N_EVEN, MLA_KV_RANK)),
        'mla_w_ukv': w((N_EVEN, MLA_KV_RANK, MLA_HEADS * (MLA_NOPE_DIM + MLA_V_DIM)), MLA_KV_RANK),
        'ab_w_out': w((N_EVEN, AB_OUT_WIDTH, D), AB_OUT_WIDTH),
        'ffn_w_gate': w((N_EVEN, D, D_FF), D),
        'ffn_w_up': w((N_EVEN, D, D_FF), D),
        'ffn_w_down': w((N_EVEN, D_FF, D), D_FF),
        'dil_w_qkv': w((N_ODD, D, DIL_IN_WIDTH), D),
        'dil_w_out': w((N_ODD, DIL_OUT_WIDTH, D), DIL_OUT_WIDTH),
        'moe_w_router': w((N_ODD, D, N_EXPERTS), D),
        'moe_w_gate': w((N_ODD, N_EXPERTS, D, D_FF_EXPERT), D),
        'moe_w_up': w((N_ODD, N_EXPERTS, D, D_FF_EXPERT), D),
        'moe_w_down': w((N_ODD, N_EXPERTS, D_FF_EXPERT, D), D_FF_EXPERT),
    }


def reference(x, p, mix_pre_g, mix_post_g, ffn_pre_g, ffn_post_g, ple_w_proj, ple_gate_norm_g,
              ple_w_gate, ple_post_g, ab_w_in, gqa_q_norm_g, gqa_k_norm_g, mla_q_norm_g, mla_w_uq,
              mla_kv_norm_g, mla_w_ukv, ab_w_out, ffn_w_gate, ffn_w_up, ffn_w_down, dil_w_qkv,
              dil_w_out, moe_w_router, moe_w_gate, moe_w_up, moe_w_down):
    s = x.shape[1]
    rows = s // GRID_W
    t = jnp.arange(s, dtype=jnp.int32)
    row_pos = jnp.repeat(jnp.arange(rows, dtype=jnp.int32), GRID_W)
    col_pos = jnp.tile(jnp.arange(GRID_W, dtype=jnp.int32), rows)
    row_cs = rotary_tables(row_pos, HEAD_DIM // 2, AXIAL_THETA)
    col_cs = rotary_tables(col_pos, HEAD_DIM // 2, AXIAL_THETA)
    mla_cs = rotary_tables(t, MLA_ROPE_DIM, MLA_ROPE_THETA)
    part_cs = rotary_tables(t, PARTIAL_ROPE_DIM, ROPE_THETA)

    h = x
    for i in range(DEPTH):
        j = i // 2
        a = rms_norm(h, mix_pre_g[i])
        if i % 2 == 0:
            m = mixer_gqa_mla(a, ab_w_in[j], gqa_q_norm_g[j], gqa_k_norm_g[j], mla_q_norm_g[j],
                              mla_w_uq[j], mla_kv_norm_g[j], mla_w_ukv[j], ab_w_out[j],
                              row_cs, col_cs, mla_cs)
        else:
            m = mixer_dilated(a, dil_w_qkv[j], dil_w_out[j], part_cs)
        h = h + rms_norm(m, mix_post_g[i])
        a = rms_norm(h, ffn_pre_g[i])
        if i % 2 == 0:
            f = swiglu(a, ffn_w_gate[j], ffn_w_up[j], ffn_w_down[j])
        else:
            f = moe_swiglu(a, moe_w_router[j], moe_w_gate[j], moe_w_up[j], moe_w_down[j])
        h = h + rms_norm(f, ffn_post_g[i])
        gate = jax.nn.sigmoid(rms_norm(h, ple_gate_norm_g[i]) @ ple_w_gate[i])
        e = (p[i].astype(h.dtype) @ ple_w_proj[i]) * gate
        h = h + rms_norm(e, ple_post_g[i])
    return h
```

```python
import functools

import jax
import jax.numpy as jnp
from jax import lax
from jax.experimental import pallas as pl
from jax.experimental.pallas import tpu as pltpu

F32 = jnp.float32
BF16 = jnp.bfloat16

D_MODEL = 1024
GRID_W = 64
HEAD_DIM = 64
NORM_EPS = 1e-6
NEG_INF = -1e30
ROPE_THETA = 500000.0
PARTIAL_ROPE_DIM = HEAD_DIM // 4
GQA_Q_HEADS = 8
GQA_KV_HEADS = 2
AXIAL_THETA = 10000.0
MLA_HEADS = 8
MLA_Q_RANK = 256
MLA_KV_RANK = 128
MLA_NOPE_DIM = 64
MLA_ROPE_DIM = 32
MLA_V_DIM = 64
MLA_ROPE_THETA = 10000.0
DIL_CONFIGS = ((128, 1), (512, 4), (2048, 16))
DIL_GROUPS = len(DIL_CONFIGS)
DIL_HEADS = 8
DIL_HALF = 64
DIL_IN_WIDTH = DIL_GROUPS * 3 * DIL_HEADS * HEAD_DIM
D_FF = 3584
N_EXPERTS = 8
TOP_K = 2
PLE_DIM = 256

LANES = 128
VMEM_BYTES = 64 * 1024 * 1024
MIB = 1024 * 1024

TM = 512
TQ = 256
TF = 512
TM_FFN = 1024
TM_MOE = 512
AB_IN_PAD = 1280


def _params(semantics, vmem_mib):
    return pltpu.CompilerParams(dimension_semantics=semantics,
                                vmem_limit_bytes=vmem_mib * MIB)


def _dot(a, b):
    return jnp.dot(a, b, preferred_element_type=F32)


def _rms(x, g):
    return x * lax.rsqrt(jnp.mean(x * x, axis=-1, keepdims=True) + NORM_EPS) * g


def _lane_iota(shape):
    return lax.broadcasted_iota(jnp.int32, shape, len(shape) - 1)


def _swap_halves(x, k):
    w = x.shape[-1]
    fwd = pltpu.roll(x, w - k, 1)
    bwd = pltpu.roll(x, k, 1)
    return jnp.where((_lane_iota(x.shape) % (2 * k)) < k, fwd, bwd)


def _rotate(x, c, s, k):
    return x * c + _swap_halves(x, k) * s


def _head_rms(x, g):
    lo = _lane_iota(x.shape) < HEAD_DIM
    x2 = x * x
    s_all = jnp.sum(x2, axis=-1, keepdims=True)
    s_lo = jnp.sum(jnp.where(lo, x2, 0.0), axis=-1, keepdims=True)
    ms = jnp.where(lo, s_lo, s_all - s_lo) * (1.0 / HEAD_DIM)
    return x * lax.rsqrt(ms + NORM_EPS) * g


def _l0_in_kernel(h_ref, gpre_ref, win_ref, gq_ref, gk_ref, ca_ref, sa_ref,
                  gmq_ref, wuq_ref, cm_ref, sm_ref, gmkv_ref, wk_ref, wp_ref, wv_ref,
                  ckr_ref, skr_ref,
                  qa_ref, kat_ref, va_ref, qm_ref, kmt_ref, vm_ref):
    a = _rms(h_ref[...], gpre_ref[...]).astype(BF16)
    z = _dot(a, win_ref[...])
    ca, sa = ca_ref[...], sa_ref[...]
    for j in range(4):
        xq = _head_rms(z[:, j * LANES:(j + 1) * LANES], gq_ref[...])
        qa_ref[:, j * LANES:(j + 1) * LANES] = (
            _rotate(xq, ca, sa, 16) * (HEAD_DIM ** -0.5)).astype(BF16)
    xk = _rotate(_head_rms(z[:, 512:640], gk_ref[...]), ca, sa, 16)
    kat_ref[...] = xk.T.astype(BF16)
    va_ref[...] = z[:, 640:768].astype(BF16)
    cq = _rms(z[:, 768:1024], gmq_ref[...]).astype(BF16)
    qm = _dot(cq, wuq_ref[...])
    cm, sm = cm_ref[...], sm_ref[...]
    scale_m = (MLA_NOPE_DIM + MLA_ROPE_DIM) ** -0.5
    for hb in range(MLA_HEADS):
        x = qm[:, hb * LANES:(hb + 1) * LANES]
        qm_ref[:, hb * LANES:(hb + 1) * LANES] = (_rotate(x, cm, sm, 16) * scale_m).astype(BF16)
    ckv = _rms(z[:, 1024:1152], gmkv_ref[...]).astype(BF16)
    kr = _rotate(z[:, 1152:1280], ckr_ref[...], skr_ref[...], 16).astype(BF16)
    km = _dot(ckv, wk_ref[...]) + _dot(kr, wp_ref[...])
    kmt_ref[...] = km.T.astype(BF16)
    vm_ref[...] = _dot(ckv, wv_ref[...]).astype(BF16)


def _l0_in(h, gpre, win, gq, gk, ca, sa, gmq, wuq, cm, sm, gmkv, wk, wp, wv, ckr, skr, b, s):
    m = h.shape[0]
    nt = s // TM
    row = lambda i: (i, 0)
    const = lambda i: (0, 0)
    tab = lambda i: (i % nt, 0)
    full = lambda a: pl.BlockSpec(a.shape, const)
    in_specs = [pl.BlockSpec((TM, D_MODEL), row), full(gpre), full(win), full(gq), full(gk),
                pl.BlockSpec((TM, LANES), tab), pl.BlockSpec((TM, LANES), tab),
                full(gmq), full(wuq), pl.BlockSpec((TM, LANES), tab), pl.BlockSpec((TM, LANES), tab),
                full(gmkv), full(wk), full(wp), full(wv),
                pl.BlockSpec((TM, LANES), tab), pl.BlockSpec((TM, LANES), tab)]
    tr = lambda i: (i // nt, 0, i % nt)
    out_specs = [pl.BlockSpec((TM, 512), row),
                 pl.BlockSpec((None, LANES, TM), tr),
                 pl.BlockSpec((TM, LANES), row),
                 pl.BlockSpec((TM, 1024), row),
                 pl.BlockSpec((None, 1024, TM), tr),
                 pl.BlockSpec((TM, 512), row)]
    out_shape = [jax.ShapeDtypeStruct((m, 512), BF16),
                 jax.ShapeDtypeStruct((b, LANES, s), BF16),
                 jax.ShapeDtypeStruct((m, LANES), BF16),
                 jax.ShapeDtypeStruct((m, 1024), BF16),
                 jax.ShapeDtypeStruct((b, 1024, s), BF16),
                 jax.ShapeDtypeStruct((m, 512), BF16)]
    return pl.pallas_call(
        _l0_in_kernel, grid=(m // TM,), in_specs=in_specs, out_specs=out_specs,
        out_shape=out_shape, compiler_params=_params(("parallel",), 48), name="l0_in",
    )(h, gpre, win, gq, gk, ca, sa, gmq, wuq, cm, sm, gmkv, wk, wp, wv, ckr, skr)


def _softmax_pv(s, v):
    m = jnp.max(s, axis=-1, keepdims=True)
    p = jnp.exp(s - m)
    l = jnp.sum(p, axis=-1, keepdims=True)
    return _dot(p.astype(BF16), v) / l


def _gqa_attn_kernel(q_ref, kt_ref, v_ref, o_ref):
    q = q_ref[...]
    lo = _lane_iota(q.shape) < HEAD_DIM
    kt = kt_ref[...]
    v = v_ref[...]
    zero = jnp.zeros_like(q)
    o0 = _softmax_pv(_dot(jnp.where(lo, q, zero), kt), v)
    o1 = _softmax_pv(_dot(jnp.where(lo, zero, q), kt), v)
    o_ref[...] = jnp.where(lo, o0, o1).astype(BF16)


def _gqa_attn(qa, kat, va, b, s):
    m = qa.shape[0]
    nq = s // TQ
    return pl.pallas_call(
        _gqa_attn_kernel, grid=(b, nq, 4),
        in_specs=[pl.BlockSpec((TQ, LANES), lambda bi, i, j: (bi * nq + i, j)),
                  pl.BlockSpec((None, LANES, s), lambda bi, i, j: (bi, 0, 0)),
                  pl.BlockSpec((s, LANES), lambda bi, i, j: (bi, 0))],
        out_specs=pl.BlockSpec((TQ, LANES), lambda bi, i, j: (bi * nq + i, j)),
        out_shape=jax.ShapeDtypeStruct((m, 512), BF16),
        compiler_params=_params(("parallel", "parallel", "parallel"), 32), name="gqa_attn",
    )(qa, kat, va)


def _mla_attn_kernel(q_ref, kt_ref, v_ref, o_ref):
    v = v_ref[...]
    o0 = _softmax_pv(_dot(q_ref[:, 0:LANES], kt_ref[0:LANES, :]), v)
    o1 = _softmax_pv(_dot(q_ref[:, LANES:2 * LANES], kt_ref[LANES:2 * LANES, :]), v)
    lo = _lane_iota(o0.shape) < MLA_V_DIM
    o_ref[...] = jnp.where(lo, o0, o1).astype(BF16)


def _mla_attn(qm, kmt, vm, b, s):
    m = qm.shape[0]
    nq = s // TQ
    return pl.pallas_call(
        _mla_attn_kernel, grid=(b, nq, 4),
        in_specs=[pl.BlockSpec((TQ, 2 * LANES), lambda bi, i, j: (bi * nq + i, j)),
                  pl.BlockSpec((None, 2 * LANES, s), lambda bi, i, j: (bi, j, 0)),
                  pl.BlockSpec((s, LANES), lambda bi, i, j: (bi, j))],
        out_specs=pl.BlockSpec((TQ, LANES), lambda bi, i, j: (bi * nq + i, j)),
        out_shape=jax.ShapeDtypeStruct((m, 512), BF16),
        compiler_params=_params(("parallel", "parallel", "parallel"), 32), name="mla_attn",
    )(qm, kmt, vm)


def _l0_out_kernel(oa_ref, om_ref, w_ref, g_ref, h_ref, out_ref):
    k = oa_ref.shape[1]
    y = _dot(oa_ref[...], w_ref[0:k, :]) + _dot(om_ref[...], w_ref[k:, :])
    out_ref[...] = h_ref[...] + _rms(y, g_ref[...])


def _l0_out(oa, om, w, g, h):
    m = h.shape[0]
    row = lambda i: (i, 0)
    const = lambda i: (0, 0)
    return pl.pallas_call(
        _l0_out_kernel, grid=(m // TM,),
        in_specs=[pl.BlockSpec((TM, oa.shape[1]), row), pl.BlockSpec((TM, om.shape[1]), row),
                  pl.BlockSpec(w.shape, const), pl.BlockSpec(g.shape, const),
                  pl.BlockSpec((TM, D_MODEL), row)],
        out_specs=pl.BlockSpec((TM, D_MODEL), row),
        out_shape=jax.ShapeDtypeStruct((m, D_MODEL), F32),
        compiler_params=_params(("parallel",), 32), name="l0_out",
    )(oa, om, w, g, h)


def _silu(x):
    return x * (1.0 / (1.0 + jnp.exp(-x)))


def _ffn_kernel(h_ref, gpre_ref, wg_ref, wu_ref, wd_ref, gpost_ref, out_ref, a_ref, acc_ref):
    j = pl.program_id(1)

    @pl.when(j == 0)
    def _():
        a_ref[...] = _rms(h_ref[...], gpre_ref[...]).astype(BF16)
        acc_ref[...] = jnp.zeros_like(acc_ref)

    a = a_ref[...]
    mid = (_silu(_dot(a, wg_ref[...])) * _dot(a, wu_ref[...])).astype(BF16)
    acc_ref[...] += _dot(mid, wd_ref[...])

    @pl.when(j == pl.num_programs(1) - 1)
    def _():
        out_ref[...] = h_ref[...] + _rms(acc_ref[...], gpost_ref[...])


def _ffn(h, gpre, wg, wu, wd, gpost):
    m = h.shape[0]
    dff = wg.shape[1]
    row = lambda i, j: (i, 0)
    const = lambda i, j: (0, 0)
    return pl.pallas_call(
        _ffn_kernel, grid=(m // TM_FFN, dff // TF),
        in_specs=[pl.BlockSpec((TM_FFN, D_MODEL), row), pl.BlockSpec(gpre.shape, const),
                  pl.BlockSpec((D_MODEL, TF), lambda i, j: (0, j)),
                  pl.BlockSpec((D_MODEL, TF), lambda i, j: (0, j)),
                  pl.BlockSpec((TF, D_MODEL), lambda i, j: (j, 0)),
                  pl.BlockSpec(gpost.shape, const)],
        out_specs=pl.BlockSpec((TM_FFN, D_MODEL), row),
        out_shape=jax.ShapeDtypeStruct((m, D_MODEL), F32),
        scratch_shapes=[pltpu.VMEM((TM_FFN, D_MODEL), BF16), pltpu.VMEM((TM_FFN, D_MODEL), F32)],
        compiler_params=_params(("parallel", "arbitrary"), 48), name="ffn",
    )(h, gpre, wg, wu, wd, gpost)


def _ple_kernel(h_ref, p_ref, ggn_ref, wg_ref, wp_ref, gpost_ref, out_ref):
    h = h_ref[...]
    gate = _dot(_rms(h, ggn_ref[...]).astype(BF16), wg_ref[...])
    gate = 1.0 / (1.0 + jnp.exp(-gate))
    e = _dot(p_ref[...].astype(BF16), wp_ref[...]) * gate
    out_ref[...] = h + _rms(e, gpost_ref[...])


def _ple(h, p, ggn, wg, wp, gpost):
    m = h.shape[0]
    row = lambda i: (i, 0)
    const = lambda i: (0, 0)
    return pl.pallas_call(
        _ple_kernel, grid=(m // TM,),
        in_specs=[pl.BlockSpec((TM, D_MODEL), row), pl.BlockSpec((TM, PLE_DIM), row),
                  pl.BlockSpec(ggn.shape, const), pl.BlockSpec(wg.shape, const),
                  pl.BlockSpec(wp.shape, const), pl.BlockSpec(gpost.shape, const)],
        out_specs=pl.BlockSpec((TM, D_MODEL), row),
        out_shape=jax.ShapeDtypeStruct((m, D_MODEL), F32),
        compiler_params=_params(("parallel",), 32), name="ple",
    )(h, p, ggn, wg, wp, gpost)


def _l1_qkv_kernel(h_ref, gpre_ref, w_ref, c_ref, s_ref, out_ref, a_ref):
    @pl.when(pl.program_id(1) == 0)
    def _():
        a_ref[...] = _rms(h_ref[...], gpre_ref[...]).astype(BF16)

    z = _dot(a_ref[...], w_ref[...])
    c, s = c_ref[...], s_ref[...]
    for blk in range(4):
        x = z[:, blk * LANES:(blk + 1) * LANES]
        out_ref[:, blk * LANES:(blk + 1) * LANES] = _rotate(x, c, s, PARTIAL_ROPE_DIM // 2).astype(BF16)


def _l1_qkv(h, gpre, w, ctab, stab, s):
    m = h.shape[0]
    nt = s // TM
    n = w.shape[1]
    return pl.pallas_call(
        _l1_qkv_kernel, grid=(m // TM, n // 512),
        in_specs=[pl.BlockSpec((TM, D_MODEL), lambda i, j: (i, 0)),
                  pl.BlockSpec(gpre.shape, lambda i, j: (0, 0)),
                  pl.BlockSpec((D_MODEL, 512), lambda i, j: (0, j)),
                  pl.BlockSpec((None, TM, LANES), lambda i, j: (j % 3, i % nt, 0)),
                  pl.BlockSpec((None, TM, LANES), lambda i, j: (j % 3, i % nt, 0))],
        out_specs=pl.BlockSpec((TM, 512), lambda i, j: (i, j)),
        out_shape=jax.ShapeDtypeStruct((m, n), BF16),
        scratch_shapes=[pltpu.VMEM((TM, D_MODEL), BF16)],
        compiler_params=_params(("parallel", "arbitrary"), 32), name="l1_qkv",
    )(h, gpre, w, ctab, stab)


def _dil_attn_kernel(q_ref, k_ref, v_ref, o_ref, lse_ref, *, length, width):
    nb = length // DIL_HALF
    lo = _lane_iota((DIL_HALF, LANES)) < HEAD_DIM
    row = lax.broadcasted_iota(jnp.int32, (DIL_HALF, width), 0)
    col = lax.broadcasted_iota(jnp.int32, (DIL_HALF, width), 1)
    rel = row - col

    def body(i, carry):
        q0 = pl.multiple_of(i * DIL_HALF, DIL_HALF)
        start = pl.multiple_of(jnp.clip(q0 - DIL_HALF, 0, length - width), DIL_HALF)
        q = q_ref[pl.ds(q0, DIL_HALF), :]
        kw = k_ref[pl.ds(start, width), :]
        vw = v_ref[pl.ds(start, width), :]
        mask = jnp.abs(rel + (q0 - start)) <= DIL_HALF
        zero = jnp.zeros_like(q)
        outs, lses = [], []
        for qh in (jnp.where(lo, q, zero), jnp.where(lo, zero, q)):
            sc = lax.dot_general(qh, kw, (((1,), (1,)), ((), ())), preferred_element_type=F32)
            sc = jnp.where(mask, sc, NEG_INF)
            m = jnp.max(sc, axis=-1, keepdims=True)
            p = jnp.exp(sc - m)
            l = jnp.sum(p, axis=-1, keepdims=True)
            outs.append(_dot(p.astype(BF16), vw) / l)
            lses.append(m + jnp.log(l))
        o_ref[pl.ds(q0, DIL_HALF), :] = jnp.where(lo, outs[0], outs[1]).astype(BF16)
        lse_ref[pl.ds(q0, DIL_HALF), :] = jnp.where(lo, lses[0], lses[1])
        return carry

    lax.fori_loop(0, nb, body, 0)


def _dil_attn(qkv, gi, dilation, b, s):
    length = s // dilation
    width = min(3 * DIL_HALF, length)
    x = qkv.reshape(b, length, dilation * DIL_IN_WIDTH)
    per_tok = DIL_IN_WIDTH // LANES
    base = gi * 12

    def spec(j):
        return pl.BlockSpec((None, length, LANES),
                            lambda bi, r, hp: (bi, 0, r * per_tok + base + 4 * j + hp))

    out_spec = pl.BlockSpec((None, length, LANES), lambda bi, r, hp: (bi, 0, r * 4 + hp))
    o, lse = pl.pallas_call(
        functools.partial(_dil_attn_kernel, length=length, width=width),
        grid=(b, dilation, 4),
        in_specs=[spec(0), spec(1), spec(2)],
        out_specs=[out_spec, out_spec],
        out_shape=[jax.ShapeDtypeStruct((b, length, dilation * 512), BF16),
                   jax.ShapeDtypeStruct((b, length, dilation * 512), F32)],
        compiler_params=_params(("parallel", "parallel", "parallel"), 32), name=f"dil_attn_{gi}",
    )(x, x, x)
    return o.reshape(b * s, 512), lse.reshape(b * s, 512)


def _l1_out_kernel(o0_ref, o1_ref, o2_ref, l0_ref, l1_ref, l2_ref, w_ref, g_ref, h_ref, out_ref):
    l0, l1, l2 = l0_ref[...], l1_ref[...], l2_ref[...]
    mx = jnp.maximum(jnp.maximum(l0, l1), l2)
    e0, e1, e2 = jnp.exp(l0 - mx), jnp.exp(l1 - mx), jnp.exp(l2 - mx)
    o = (e0 * o0_ref[...].astype(F32) + e1 * o1_ref[...].astype(F32)
         + e2 * o2_ref[...].astype(F32)) / (e0 + e1 + e2)
    y = _dot(o.astype(BF16), w_ref[...])
    out_ref[...] = h_ref[...] + _rms(y, g_ref[...])


def _l1_out(os_, lses, w, g, h):
    m = h.shape[0]
    row = lambda i: (i, 0)
    const = lambda i: (0, 0)
    act = pl.BlockSpec((TM, 512), row)
    return pl.pallas_call(
        _l1_out_kernel, grid=(m // TM,),
        in_specs=[act] * 6 + [pl.BlockSpec(w.shape, const), pl.BlockSpec(g.shape, const),
                              pl.BlockSpec((TM, D_MODEL), row)],
        out_specs=pl.BlockSpec((TM, D_MODEL), row),
        out_shape=jax.ShapeDtypeStruct((m, D_MODEL), F32),
        compiler_params=_params(("parallel",), 32), name="l1_out",
    )(*os_, *lses, w, g, h)


def _router_kernel(h_ref, gpre_ref, wr_ref, a_ref, idx_ref, gate_ref, cnt_ref, carry_ref):
    @pl.when(pl.program_id(0) == 0)
    def _():
        carry_ref[...] = jnp.zeros_like(carry_ref)

    a = _rms(h_ref[...], gpre_ref[...])
    a_ref[...] = a.astype(BF16)
    logits = jnp.dot(a, wr_ref[...], precision=lax.Precision.HIGHEST, preferred_element_type=F32)
    tm = logits.shape[0]
    lane = _lane_iota(logits.shape)
    lanef = lane.astype(F32)
    lg = jnp.where(lane < N_EXPERTS, logits, -jnp.inf)
    m1 = jnp.max(lg, axis=-1, keepdims=True)
    i1 = jnp.min(jnp.where(lg == m1, lanef, float(LANES)), axis=-1, keepdims=True)
    lg2 = jnp.where(lanef == i1, -jnp.inf, lg)
    m2 = jnp.max(lg2, axis=-1, keepdims=True)
    i2 = jnp.min(jnp.where(lg2 == m2, lanef, float(LANES)), axis=-1, keepdims=True)
    t = jnp.exp(m2 - m1)
    g1 = 1.0 / (1.0 + t)
    g2 = t / (1.0 + t)
    hit1 = lanef == i1
    hit2 = lanef == i2
    onehot = jnp.where(hit1 | hit2, 1.0, 0.0)
    r = lax.broadcasted_iota(jnp.int32, (tm, tm), 0)
    c = lax.broadcasted_iota(jnp.int32, (tm, tm), 1)
    tri = jnp.where(c < r, 1.0, 0.0).astype(BF16)
    before = _dot(tri, onehot.astype(BF16)) + carry_ref[...]
    rank1 = jnp.sum(jnp.where(hit1, before, 0.0), axis=-1, keepdims=True)
    rank2 = jnp.sum(jnp.where(hit2, before, 0.0), axis=-1, keepdims=True)
    info = jnp.where(lane == 0, i1, jnp.where(lane == 1, i2, jnp.where(lane == 2, rank1, rank2)))
    idx_ref[...] = info.astype(jnp.int32)
    gate_ref[...] = jnp.where(lane == 0, g1, g2)
    carry_ref[...] += jnp.sum(onehot, axis=0, keepdims=True)
    cnt_ref[...] = carry_ref[...]


def _router(h, gpre, wr):
    m = h.shape[0]
    row = lambda i: (i, 0)
    const = lambda i: (0, 0)
    return pl.pallas_call(
        _router_kernel, grid=(m // TM,),
        in_specs=[pl.BlockSpec((TM, D_MODEL), row), pl.BlockSpec(gpre.shape, const),
                  pl.BlockSpec(wr.shape, const)],
        out_specs=[pl.BlockSpec((TM, D_MODEL), row), pl.BlockSpec((TM, LANES), row),
                   pl.BlockSpec((TM, LANES), row), pl.BlockSpec((1, LANES), const)],
        out_shape=[jax.ShapeDtypeStruct((m, D_MODEL), BF16),
                   jax.ShapeDtypeStruct((m, LANES), jnp.int32),
                   jax.ShapeDtypeStruct((m, LANES), F32),
                   jax.ShapeDtypeStruct((1, LANES), F32)],
        scratch_shapes=[pltpu.VMEM((1, LANES), F32)],
        compiler_params=_params(("arbitrary",), 32), name="moe_router",
    )(h, gpre, wr)


def _expert_kernel(te_ref, tv_ref, x_ref, wg_ref, wu_ref, wd_ref, y_ref, acc_ref):
    i = pl.program_id(0)
    j = pl.program_id(1)
    last = pl.num_programs(1) - 1
    valid = tv_ref[i] > 0

    @pl.when(valid & (j == 0))
    def _():
        acc_ref[...] = jnp.zeros_like(acc_ref)

    @pl.when(valid)
    def _():
        x = x_ref[...]
        mid = (_silu(_dot(x, wg_ref[...])) * _dot(x, wu_ref[...])).astype(BF16)
        acc_ref[...] += _dot(mid, wd_ref[...])

    @pl.when(valid & (j == last))
    def _():
        y_ref[...] = acc_ref[...]

    @pl.when(jnp.logical_not(valid) & (j == last))
    def _():
        y_ref[...] = jnp.zeros_like(y_ref)


def _experts(tile_e, tile_valid, xs, wg, wu, wd):
    n_slots = xs.shape[0]
    dff = wg.shape[2]
    grid_spec = pltpu.PrefetchScalarGridSpec(
        num_scalar_prefetch=2, grid=(n_slots // TM_MOE, dff // TF),
        in_specs=[pl.BlockSpec((TM_MOE, D_MODEL), lambda i, j, te, tv: (i, 0)),
                  pl.BlockSpec((None, D_MODEL, TF), lambda i, j, te, tv: (te[i], 0, j)),
                  pl.BlockSpec((None, D_MODEL, TF), lambda i, j, te, tv: (te[i], 0, j)),
                  pl.BlockSpec((None, TF, D_MODEL), lambda i, j, te, tv: (te[i], j, 0))],
        out_specs=pl.BlockSpec((TM_MOE, D_MODEL), lambda i, j, te, tv: (i, 0)),
        scratch_shapes=[pltpu.VMEM((TM_MOE, D_MODEL), F32)])
    return pl.pallas_call(
        _expert_kernel, grid_spec=grid_spec,
        out_shape=jax.ShapeDtypeStruct((n_slots, D_MODEL), F32),
        compiler_params=_params(("parallel", "arbitrary"), 32), name="moe_experts",
    )(tile_e, tile_valid, xs, wg, wu, wd)


def _combine_kernel(y0_ref, y1_ref, gate_ref, g_ref, h_ref, out_ref):
    gate = gate_ref[...]
    f = y0_ref[...] * gate[:, 0:1] + y1_ref[...] * gate[:, 1:2]
    out_ref[...] = h_ref[...] + _rms(f, g_ref[...])


def _combine(y0, y1, gate, g, h):
    m = h.shape[0]
    row = lambda i: (i, 0)
    const = lambda i: (0, 0)
    act = pl.BlockSpec((TM, D_MODEL), row)
    return pl.pallas_call(
        _combine_kernel, grid=(m // TM,),
        in_specs=[act, act, pl.BlockSpec((TM, LANES), row), pl.BlockSpec(g.shape, const), act],
        out_specs=act,
        out_shape=jax.ShapeDtypeStruct((m, D_MODEL), F32),
        compiler_params=_params(("parallel",), 32), name="moe_combine",
    )(y0, y1, gate, g, h)


def _moe(h, gpre, wr, wg, wu, wd, gpost):
    n = h.shape[0]
    a, idx, gate, cnt = _router(h, gpre, wr)
    counts = cnt[0, :N_EXPERTS].astype(jnp.int32)
    padded = (counts + TM_MOE - 1) // TM_MOE * TM_MOE
    end_pad = jnp.cumsum(padded)
    start_pad = end_pad - padded
    dest = start_pad[idx[:, 0:2]] + idx[:, 2:4]
    n_slots = n * TOP_K + N_EXPERTS * TM_MOE
    tok = jnp.repeat(jnp.arange(n, dtype=jnp.int32), TOP_K)
    slot_tok = jnp.zeros((n_slots,), jnp.int32).at[dest.reshape(-1)].set(tok)
    tile_start = jnp.arange(n_slots // TM_MOE, dtype=jnp.int32) * TM_MOE
    tile_e = jnp.minimum(jnp.searchsorted(end_pad, tile_start, side='right'),
                         N_EXPERTS - 1).astype(jnp.int32)
    tile_valid = (tile_start < end_pad[-1]).astype(jnp.int32)
    xs = a[slot_tok]
    ys = _experts(tile_e, tile_valid, xs, wg, wu, wd)
    return _combine(ys[dest[:, 0]], ys[dest[:, 1]], gate, gpost, h)


def _rotary_tables(pos, dim, theta):
    exponent = jnp.arange(0, dim, 2, dtype=F32) / dim
    inv_freq = jnp.power(jnp.float32(theta), -exponent)
    ang = pos.astype(F32)[:, None] * inv_freq[None, :]
    return jnp.cos(ang), jnp.sin(ang)


def _tables(s):
    t = jnp.arange(s, dtype=jnp.int32)
    cr, sr = _rotary_tables(t // GRID_W, HEAD_DIM // 2, AXIAL_THETA)
    cc, sc = _rotary_tables(t % GRID_W, HEAD_DIM // 2, AXIAL_THETA)
    ca = jnp.tile(jnp.concatenate([cr, cr, cc, cc], axis=1), (1, 2))
    sa = jnp.tile(jnp.concatenate([-sr, sr, -sc, sc], axis=1), (1, 2))
    cm_, sm_ = _rotary_tables(t, MLA_ROPE_DIM, MLA_ROPE_THETA)
    one = jnp.ones((s, 1), F32)
    zero = jnp.zeros((s, 1), F32)
    cm = jnp.concatenate([jnp.tile(one, (1, 64)), cm_, cm_, jnp.tile(one, (1, 32))], axis=1)
    sm = jnp.concatenate([jnp.tile(zero, (1, 64)), -sm_, sm_, jnp.tile(zero, (1, 32))], axis=1)
    ckr = jnp.concatenate([cm_, cm_, jnp.tile(zero, (1, 96))], axis=1)
    skr = jnp.concatenate([-sm_, sm_, jnp.tile(zero, (1, 96))], axis=1)
    cp_, sp_ = _rotary_tables(t, PARTIAL_ROPE_DIM, ROPE_THETA)
    cp = jnp.tile(jnp.concatenate([cp_, cp_, jnp.tile(one, (1, 48))], axis=1), (1, 2))
    sp = jnp.tile(jnp.concatenate([-sp_, sp_, jnp.tile(zero, (1, 48))], axis=1), (1, 2))
    scale = HEAD_DIM ** -0.5
    ctab = jnp.stack([cp * scale, cp, jnp.ones_like(cp)])
    stab = jnp.stack([sp * scale, sp, jnp.zeros_like(sp)])
    return ca, sa, cm, sm, ckr, skr, ctab, stab


_GQA_PERM = (0, 4, 1, 5, 2, 6, 3, 7)


def _l0_weights(w_in, w_uq, w_ukv, w_out):
    d = w_in.shape[0]
    q = w_in[:, :512].reshape(d, 8, 64)[:, _GQA_PERM, :].reshape(d, 512)
    win = jnp.concatenate([q, w_in[:, 512:], jnp.zeros((d, AB_IN_PAD - w_in.shape[1]), w_in.dtype)],
                          axis=1).astype(BF16)
    uq = w_uq.reshape(MLA_Q_RANK, MLA_HEADS, MLA_NOPE_DIM + MLA_ROPE_DIM)
    wuq = jnp.pad(uq, ((0, 0), (0, 0), (0, LANES - uq.shape[2]))).reshape(MLA_Q_RANK, -1).astype(BF16)
    ukv = w_ukv.reshape(MLA_KV_RANK, MLA_HEADS, MLA_NOPE_DIM + MLA_V_DIM)
    wk = jnp.pad(ukv[:, :, :MLA_NOPE_DIM], ((0, 0), (0, 0), (0, LANES - MLA_NOPE_DIM)))
    wk = wk.reshape(MLA_KV_RANK, -1).astype(BF16)
    wv = ukv[:, :, MLA_NOPE_DIM:].reshape(MLA_KV_RANK, -1).astype(BF16)
    src = jnp.arange(LANES)[:, None]
    dst = jnp.arange(MLA_HEADS * LANES)[None, :]
    wp = ((src < MLA_ROPE_DIM) & (dst % LANES == src + MLA_NOPE_DIM)).astype(BF16)
    oa = w_out[:512].reshape(8, 64, -1)[_GQA_PERM, :, :].reshape(512, -1)
    wout = jnp.concatenate([oa, w_out[512:]], axis=0).astype(BF16)
    return win, wuq, wk, wp, wv, wout


def kernel(x, p, mix_pre_g, mix_post_g, ffn_pre_g, ffn_post_g, ple_w_proj, ple_gate_norm_g, ple_w_gate, ple_post_g, ab_w_in, gqa_q_norm_g, gqa_k_norm_g, mla_q_norm_g, mla_w_uq, mla_kv_norm_g, mla_w_ukv, ab_w_out, ffn_w_gate, ffn_w_up, ffn_w_down, dil_w_qkv, dil_w_out, moe_w_router, moe_w_gate, moe_w_up, moe_w_down):
    b, s, d = x.shape
    m = b * s
    row = lambda g: g.reshape(1, -1).astype(F32)
    ca, sa, cm, sm, ckr, skr, ctab, stab = _tables(s)
    h = x.reshape(m, d)
    pf = p.reshape(p.shape[0], m, PLE_DIM)

    win, wuq, wk, wp, wv, wout = _l0_weights(ab_w_in[0], mla_w_uq[0], mla_w_ukv[0], ab_w_out[0])
    gq = jnp.tile(row(gqa_q_norm_g[0]), (1, 2))
    gk = jnp.tile(row(gqa_k_norm_g[0]), (1, 2))
    qa, kat, va, qm, kmt, vm = _l0_in(h, row(mix_pre_g[0]), win, gq, gk, ca, sa,
                                      row(mla_q_norm_g[0]), wuq, cm, sm,
                                      row(mla_kv_norm_g[0]), wk, wp, wv, ckr, skr, b, s)
    oa = _gqa_attn(qa, kat, va, b, s)
    om = _mla_attn(qm, kmt, vm, b, s)
    h = _l0_out(oa, om, wout, row(mix_post_g[0]), h)
    h = _ffn(h, row(ffn_pre_g[0]), ffn_w_gate[0].astype(BF16), ffn_w_up[0].astype(BF16),
             ffn_w_down[0].astype(BF16), row(ffn_post_g[0]))
    h = _ple(h, pf[0], row(ple_gate_norm_g[0]), ple_w_gate[0].astype(BF16),
             ple_w_proj[0].astype(BF16), row(ple_post_g[0]))

    qkv = _l1_qkv(h, row(mix_pre_g[1]), dil_w_qkv[0].astype(BF16), ctab, stab, s)
    outs, lses = [], []
    for gi, (_, dilation) in enumerate(DIL_CONFIGS):
        o_g, lse_g = _dil_attn(qkv, gi, dilation, b, s)
        outs.append(o_g)
        lses.append(lse_g)
    h = _l1_out(outs, lses, dil_w_out[0].astype(BF16), row(mix_post_g[1]), h)
    wr = jnp.pad(moe_w_router[0].astype(F32), ((0, 0), (0, LANES - N_EXPERTS)))
    h = _moe(h, row(ffn_pre_g[1]), wr, moe_w_gate[0].astype(BF16), moe_w_up[0].astype(BF16),
             moe_w_down[0].astype(BF16), row(ffn_post_g[1]))
    h = _ple(h, pf[1], row(ple_gate_norm_g[1]), ple_w_gate[1].astype(BF16),
             ple_w_proj[1].astype(BF16), row(ple_post_g[1]))
    return h.reshape(b, s, d)
```

```python
import functools

import jax
import jax.numpy as jnp
from jax import lax
from jax.experimental import pallas as pl
from jax.experimental.pallas import tpu as pltpu

F32 = jnp.float32
BF16 = jnp.bfloat16

D_MODEL = 1024
GRID_W = 64
HEAD_DIM = 64
NORM_EPS = 1e-6
NEG_INF = -1e30
ROPE_THETA = 500000.0
PARTIAL_ROPE_DIM = HEAD_DIM // 4
GQA_Q_HEADS = 8
GQA_KV_HEADS = 2
AXIAL_THETA = 10000.0
MLA_HEADS = 8
MLA_Q_RANK = 256
MLA_KV_RANK = 128
MLA_NOPE_DIM = 64
MLA_ROPE_DIM = 32
MLA_V_DIM = 64
MLA_ROPE_THETA = 10000.0
DIL_CONFIGS = ((128, 1), (512, 4), (2048, 16))
DIL_GROUPS = len(DIL_CONFIGS)
DIL_HEADS = 8
DIL_HALF = 64
DIL_QB = 128
LOG2_E = 1.4426950408889634
DIL_IN_WIDTH = DIL_GROUPS * 3 * DIL_HEADS * HEAD_DIM
D_FF = 3584
N_EXPERTS = 8
TOP_K = 2
PLE_DIM = 256

LANES = 128
VMEM_BYTES = 64 * 1024 * 1024
MIB = 1024 * 1024

TM = 512
TQ = 256
KEY_CHUNK = 512
TF = 512
TM_FFN = 1024
TM_MOE = 512
AB_IN_PAD = 1280


def _params(semantics, vmem_mib):
    return pltpu.CompilerParams(dimension_semantics=semantics,
                                vmem_limit_bytes=vmem_mib * MIB)


def _dot(a, b):
    return jnp.dot(a, b, preferred_element_type=F32)


def _rms(x, g):
    return x * lax.rsqrt(jnp.mean(x * x, axis=-1, keepdims=True) + NORM_EPS) * g


def _lane_iota(shape):
    return lax.broadcasted_iota(jnp.int32, shape, len(shape) - 1)


def _swap_halves(x, k):
    w = x.shape[-1]
    fwd = pltpu.roll(x, w - k, 1)
    bwd = pltpu.roll(x, k, 1)
    return jnp.where((_lane_iota(x.shape) % (2 * k)) < k, fwd, bwd)


def _rotate(x, c, s, k):
    return x * c + _swap_halves(x, k) * s


def _head_rms(x, g):
    lo = _lane_iota(x.shape) < HEAD_DIM
    x2 = x * x
    s_all = jnp.sum(x2, axis=-1, keepdims=True)
    s_lo = jnp.sum(jnp.where(lo, x2, 0.0), axis=-1, keepdims=True)
    ms = jnp.where(lo, s_lo, s_all - s_lo) * (1.0 / HEAD_DIM)
    return x * lax.rsqrt(ms + NORM_EPS) * g


def _l0_in_kernel(h_ref, gpre_ref, win_ref, gq_ref, gk_ref, ca_ref, sa_ref,
                  gmq_ref, wuq_ref, cm_ref, sm_ref, gmkv_ref, wk_ref, wp_ref, wv_ref,
                  ckr_ref, skr_ref,
                  qa_ref, kat_ref, va_ref, qm_ref, kmt_ref, vm_ref):
    a = _rms(h_ref[...], gpre_ref[...]).astype(BF16)
    z = _dot(a, win_ref[...])
    ca, sa = ca_ref[...], sa_ref[...]
    for j in range(4):
        xq = _head_rms(z[:, j * LANES:(j + 1) * LANES], gq_ref[...])
        qa_ref[:, j * LANES:(j + 1) * LANES] = (
            _rotate(xq, ca, sa, 16) * (HEAD_DIM ** -0.5 * LOG2_E)).astype(BF16)
    xk = _rotate(_head_rms(z[:, 512:640], gk_ref[...]), ca, sa, 16)
    kat_ref[...] = xk.T.astype(BF16)
    ones = jnp.ones((z.shape[0], LANES), BF16)
    va_ref[:, 0:LANES] = z[:, 640:768].astype(BF16)
    va_ref[:, LANES:] = ones
    cq = _rms(z[:, 768:1024], gmq_ref[...]).astype(BF16)
    qm = _dot(cq, wuq_ref[...])
    cm, sm = cm_ref[...], sm_ref[...]
    scale_m = (MLA_NOPE_DIM + MLA_ROPE_DIM) ** -0.5 * LOG2_E
    for hb in range(MLA_HEADS):
        x = qm[:, hb * LANES:(hb + 1) * LANES]
        qm_ref[:, hb * LANES:(hb + 1) * LANES] = (_rotate(x, cm, sm, 16) * scale_m).astype(BF16)
    ckv = _rms(z[:, 1024:1152], gmkv_ref[...]).astype(BF16)
    kr = _rotate(z[:, 1152:1280], ckr_ref[...], skr_ref[...], 16).astype(BF16)
    km = _dot(ckv, wk_ref[...]) + _dot(kr, wp_ref[...])
    kmt_ref[...] = km.T.astype(BF16)
    vm = _dot(ckv, wv_ref[...]).astype(BF16)
    for j in range(4):
        vm_ref[:, 2 * j * LANES:(2 * j + 1) * LANES] = vm[:, j * LANES:(j + 1) * LANES]
        vm_ref[:, (2 * j + 1) * LANES:(2 * j + 2) * LANES] = ones


def _l0_in(h, gpre, win, gq, gk, ca, sa, gmq, wuq, cm, sm, gmkv, wk, wp, wv, ckr, skr, b, s):
    m = h.shape[0]
    nt = s // TM
    row = lambda i: (i, 0)
    const = lambda i: (0, 0)
    tab = lambda i: (i % nt, 0)
    full = lambda a: pl.BlockSpec(a.shape, const)
    in_specs = [pl.BlockSpec((TM, D_MODEL), row), full(gpre), full(win), full(gq), full(gk),
                pl.BlockSpec((TM, LANES), tab), pl.BlockSpec((TM, LANES), tab),
                full(gmq), full(wuq), pl.BlockSpec((TM, LANES), tab), pl.BlockSpec((TM, LANES), tab),
                full(gmkv), full(wk), full(wp), full(wv),
                pl.BlockSpec((TM, LANES), tab), pl.BlockSpec((TM, LANES), tab)]
    tr = lambda i: (i // nt, 0, i % nt)
    out_specs = [pl.BlockSpec((TM, 512), row),
                 pl.BlockSpec((None, LANES, TM), tr),
                 pl.BlockSpec((TM, 2 * LANES), row),
                 pl.BlockSpec((TM, 1024), row),
                 pl.BlockSpec((None, 1024, TM), tr),
                 pl.BlockSpec((TM, 1024), row)]
    out_shape = [jax.ShapeDtypeStruct((m, 512), BF16),
                 jax.ShapeDtypeStruct((b, LANES, s), BF16),
                 jax.ShapeDtypeStruct((m, 2 * LANES), BF16),
                 jax.ShapeDtypeStruct((m, 1024), BF16),
                 jax.ShapeDtypeStruct((b, 1024, s), BF16),
                 jax.ShapeDtypeStruct((m, 1024), BF16)]
    return pl.pallas_call(
        _l0_in_kernel, grid=(m // TM,), in_specs=in_specs, out_specs=out_specs,
        out_shape=out_shape, compiler_params=_params(("parallel",), 48), name="l0_in",
    )(h, gpre, win, gq, gk, ca, sa, gmq, wuq, cm, sm, gmkv, wk, wp, wv, ckr, skr)


def _softmax_pv(q, kt_ref, v_ref):
    n_keys = v_ref.shape[0]
    parts = []
    for c in range(n_keys // KEY_CHUNK):
        keys = slice(c * KEY_CHUNK, (c + 1) * KEY_CHUNK)
        s = _dot(q, kt_ref[:, keys])
        m = jnp.max(s, axis=-1, keepdims=True)
        parts.append((m, _dot(jnp.exp2(s - m).astype(BF16), v_ref[keys, :])))
    m_all = functools.reduce(jnp.maximum, [m for m, _ in parts])
    o = sum(o_c * jnp.exp2(m_c - m_all) for m_c, o_c in parts)
    return o[:, 0:LANES] / o[:, LANES:]


def _gqa_attn_kernel(q_ref, kt_ref, v_ref, o_ref):
    q = q_ref[...]
    tq = q.shape[0]
    lo = _lane_iota(q.shape) < HEAD_DIM
    zero = jnp.zeros_like(q)
    q2 = jnp.concatenate([jnp.where(lo, q, zero), jnp.where(lo, zero, q)], axis=0)
    o = _softmax_pv(q2, kt_ref, v_ref)
    o_ref[...] = jnp.where(lo, o[:tq], o[tq:]).astype(BF16)


def _gqa_attn(qa, kat, va, b, s):
    m = qa.shape[0]
    nq = s // TQ
    return pl.pallas_call(
        _gqa_attn_kernel, grid=(b, nq, 4),
        in_specs=[pl.BlockSpec((TQ, LANES), lambda bi, i, j: (bi * nq + i, j)),
                  pl.BlockSpec((None, LANES, s), lambda bi, i, j: (bi, 0, 0)),
                  pl.BlockSpec((s, 2 * LANES), lambda bi, i, j: (bi, 0))],
        out_specs=pl.BlockSpec((TQ, LANES), lambda bi, i, j: (bi * nq + i, j)),
        out_shape=jax.ShapeDtypeStruct((m, 512), BF16),
        compiler_params=_params(("parallel", "parallel", "parallel"), 32), name="gqa_attn",
    )(qa, kat, va)


def _mla_attn_kernel(q_ref, kt_ref, v_ref, o_ref):
    o0 = _softmax_pv(q_ref[:, 0:LANES], kt_ref.at[0:LANES, :], v_ref)
    o1 = _softmax_pv(q_ref[:, LANES:2 * LANES], kt_ref.at[LANES:2 * LANES, :], v_ref)
    lo = _lane_iota(o0.shape) < MLA_V_DIM
    o_ref[...] = jnp.where(lo, o0, o1).astype(BF16)


def _mla_attn(qm, kmt, vm, b, s):
    m = qm.shape[0]
    nq = s // TQ
    return pl.pallas_call(
        _mla_attn_kernel, grid=(b, nq, 4),
        in_specs=[pl.BlockSpec((TQ, 2 * LANES), lambda bi, i, j: (bi * nq + i, j)),
                  pl.BlockSpec((None, 2 * LANES, s), lambda bi, i, j: (bi, j, 0)),
                  pl.BlockSpec((s, 2 * LANES), lambda bi, i, j: (bi, j))],
        out_specs=pl.BlockSpec((TQ, LANES), lambda bi, i, j: (bi * nq + i, j)),
        out_shape=jax.ShapeDtypeStruct((m, 512), BF16),
        compiler_params=_params(("parallel", "parallel", "parallel"), 32), name="mla_attn",
    )(qm, kmt, vm)


def _l0_out_kernel(oa_ref, om_ref, w_ref, g_ref, h_ref, out_ref):
    k = oa_ref.shape[1]
    y = _dot(oa_ref[...], w_ref[0:k, :]) + _dot(om_ref[...], w_ref[k:, :])
    out_ref[...] = h_ref[...] + _rms(y, g_ref[...])


def _l0_out(oa, om, w, g, h):
    m = h.shape[0]
    row = lambda i: (i, 0)
    const = lambda i: (0, 0)
    return pl.pallas_call(
        _l0_out_kernel, grid=(m // TM,),
        in_specs=[pl.BlockSpec((TM, oa.shape[1]), row), pl.BlockSpec((TM, om.shape[1]), row),
                  pl.BlockSpec(w.shape, const), pl.BlockSpec(g.shape, const),
                  pl.BlockSpec((TM, D_MODEL), row)],
        out_specs=pl.BlockSpec((TM, D_MODEL), row),
        out_shape=jax.ShapeDtypeStruct((m, D_MODEL), F32),
        compiler_params=_params(("parallel",), 32), name="l0_out",
    )(oa, om, w, g, h)


def _silu(x):
    return x * (1.0 / (1.0 + jnp.exp(-x)))


def _ffn_kernel(h_ref, gpre_ref, wg_ref, wu_ref, wd_ref, gpost_ref, out_ref, a_ref, acc_ref):
    j = pl.program_id(1)

    @pl.when(j == 0)
    def _():
        a_ref[...] = _rms(h_ref[...], gpre_ref[...]).astype(BF16)
        acc_ref[...] = jnp.zeros_like(acc_ref)

    a = a_ref[...]
    mid = (_silu(_dot(a, wg_ref[...])) * _dot(a, wu_ref[...])).astype(BF16)
    acc_ref[...] += _dot(mid, wd_ref[...])

    @pl.when(j == pl.num_programs(1) - 1)
    def _():
        out_ref[...] = h_ref[...] + _rms(acc_ref[...], gpost_ref[...])


def _ffn(h, gpre, wg, wu, wd, gpost):
    m = h.shape[0]
    dff = wg.shape[1]
    row = lambda i, j: (i, 0)
    const = lambda i, j: (0, 0)
    return pl.pallas_call(
        _ffn_kernel, grid=(m // TM_FFN, dff // TF),
        in_specs=[pl.BlockSpec((TM_FFN, D_MODEL), row), pl.BlockSpec(gpre.shape, const),
                  pl.BlockSpec((D_MODEL, TF), lambda i, j: (0, j)),
                  pl.BlockSpec((D_MODEL, TF), lambda i, j: (0, j)),
                  pl.BlockSpec((TF, D_MODEL), lambda i, j: (j, 0)),
                  pl.BlockSpec(gpost.shape, const)],
        out_specs=pl.BlockSpec((TM_FFN, D_MODEL), row),
        out_shape=jax.ShapeDtypeStruct((m, D_MODEL), F32),
        scratch_shapes=[pltpu.VMEM((TM_FFN, D_MODEL), BF16), pltpu.VMEM((TM_FFN, D_MODEL), F32)],
        compiler_params=_params(("parallel", "arbitrary"), 48), name="ffn",
    )(h, gpre, wg, wu, wd, gpost)


def _ple_kernel(h_ref, p_ref, ggn_ref, wg_ref, wp_ref, gpost_ref, out_ref):
    h = h_ref[...]
    gate = _dot(_rms(h, ggn_ref[...]).astype(BF16), wg_ref[...])
    gate = 1.0 / (1.0 + jnp.exp(-gate))
    e = _dot(p_ref[...].astype(BF16), wp_ref[...]) * gate
    out_ref[...] = h + _rms(e, gpost_ref[...])


def _ple(h, p, ggn, wg, wp, gpost):
    m = h.shape[0]
    row = lambda i: (i, 0)
    const = lambda i: (0, 0)
    return pl.pallas_call(
        _ple_kernel, grid=(m // TM,),
        in_specs=[pl.BlockSpec((TM, D_MODEL), row), pl.BlockSpec((TM, PLE_DIM), row),
                  pl.BlockSpec(ggn.shape, const), pl.BlockSpec(wg.shape, const),
                  pl.BlockSpec(wp.shape, const), pl.BlockSpec(gpost.shape, const)],
        out_specs=pl.BlockSpec((TM, D_MODEL), row),
        out_shape=jax.ShapeDtypeStruct((m, D_MODEL), F32),
        compiler_params=_params(("parallel",), 32), name="ple",
    )(h, p, ggn, wg, wp, gpost)


def _l1_qkv_kernel(h_ref, gpre_ref, w_ref, c_ref, s_ref, out_ref, a_ref):
    @pl.when(pl.program_id(1) == 0)
    def _():
        a_ref[...] = _rms(h_ref[...], gpre_ref[...]).astype(BF16)

    z = _dot(a_ref[...], w_ref[...])
    c, s = c_ref[...], s_ref[...]
    for blk in range(4):
        x = z[:, blk * LANES:(blk + 1) * LANES]
        out_ref[:, blk * LANES:(blk + 1) * LANES] = _rotate(x, c, s, PARTIAL_ROPE_DIM // 2).astype(BF16)


def _l1_qkv(h, gpre, w, ctab, stab, s):
    m = h.shape[0]
    nt = s // TM
    n = w.shape[1]
    return pl.pallas_call(
        _l1_qkv_kernel, grid=(m // TM, n // 512),
        in_specs=[pl.BlockSpec((TM, D_MODEL), lambda i, j: (i, 0)),
                  pl.BlockSpec(gpre.shape, lambda i, j: (0, 0)),
                  pl.BlockSpec((D_MODEL, 512), lambda i, j: (0, j)),
                  pl.BlockSpec((None, TM, LANES), lambda i, j: (j % 3, i % nt, 0)),
                  pl.BlockSpec((None, TM, LANES), lambda i, j: (j % 3, i % nt, 0))],
        out_specs=pl.BlockSpec((TM, 512), lambda i, j: (i, j)),
        out_shape=jax.ShapeDtypeStruct((m, n), BF16),
        scratch_shapes=[pltpu.VMEM((TM, D_MODEL), BF16)],
        compiler_params=_params(("parallel", "arbitrary"), 32), name="l1_qkv",
    )(h, gpre, w, ctab, stab)


def _dil_attn_kernel(q0_ref, k0_ref, v0_ref, q1_ref, k1_ref, v1_ref, q2_ref, k2_ref, v2_ref, o_ref,
                     stage_ref, qd_ref, kd_ref, vd_ref, ra_ref, rl_ref, rm_ref,
                     acc_ref, sum_ref, max_ref):
    seq = o_ref.shape[0]
    lo = _lane_iota((DIL_QB, LANES)) < HEAD_DIM
    vd_ref[:, LANES:2 * LANES] = jnp.ones((seq, LANES), BF16)
    groups = ((q0_ref, k0_ref, v0_ref), (q1_ref, k1_ref, v1_ref), (q2_ref, k2_ref, v2_ref))
    for gi, (q_ref, k_ref, v_ref) in enumerate(groups):
        dil = DIL_CONFIGS[gi][1]
        length = seq // dil
        width = min(DIL_QB + 2 * DIL_HALF, length)
        nblk = length // DIL_QB
        for src, dst, c1 in ((q_ref, qd_ref, LANES), (k_ref, kd_ref, LANES), (v_ref, vd_ref, LANES)):
            if dil == 1:
                dst[:, 0:c1] = src[...]
            else:
                stage_ref[...] = src[...].astype(F32)
                for r in range(dil):
                    dst[r * length:(r + 1) * length, 0:c1] = (
                        stage_ref[pl.ds(r, length, stride=dil), :].astype(BF16))
        first = gi == 0
        ta, tl, tm_ = (acc_ref, sum_ref, max_ref) if first else (ra_ref, rl_ref, rm_ref)
        row = lax.broadcasted_iota(jnp.int32, (2 * DIL_QB, width), 0) % DIL_QB
        col = lax.broadcasted_iota(jnp.int32, (2 * DIL_QB, width), 1)
        rel = row - col

        def body(it, carry, length=length, width=width, nblk=nblk, ta=ta, tl=tl, tm_=tm_, rel=rel):
            r = it // nblk
            i = it % nblk
            q0 = i * DIL_QB
            start = jnp.clip(q0 - DIL_HALF, 0, length - width)
            qrow = pl.multiple_of(r * length + q0, DIL_HALF)
            krow = pl.multiple_of(r * length + start, DIL_HALF)
            q = qd_ref[pl.ds(qrow, DIL_QB), :]
            kw = kd_ref[pl.ds(krow, width), :]
            vw = vd_ref[pl.ds(krow, width), :]
            zero = jnp.zeros_like(q)
            q2 = jnp.concatenate([jnp.where(lo, q, zero), jnp.where(lo, zero, q)], axis=0)
            sc = lax.dot_general(q2, kw, (((1,), (1,)), ((), ())), preferred_element_type=F32)
            sc = jnp.where(jnp.abs(rel + (q0 - start)) <= DIL_HALF, sc, NEG_INF)
            m = jnp.max(sc, axis=-1, keepdims=True)
            p = jnp.exp2(sc - m).astype(BF16)
            o2 = _dot(p, vw)
            ta[pl.ds(qrow, DIL_QB), :] = jnp.where(lo, o2[:DIL_QB, 0:LANES], o2[DIL_QB:, 0:LANES])
            tl[pl.ds(qrow, DIL_QB), :] = jnp.where(lo, o2[:DIL_QB, LANES:], o2[DIL_QB:, LANES:])
            tm_[pl.ds(qrow, DIL_QB), :] = jnp.where(lo, m[:DIL_QB], m[DIL_QB:])
            return carry

        lax.fori_loop(0, dil * nblk, body, 0, unroll=4)
        if first:
            continue
        for r in range(dil):
            rows = pl.ds(r, length, stride=dil)
            blk = slice(r * length, (r + 1) * length)
            m_old, m_blk = max_ref[rows, :], rm_ref[blk, :]
            m_new = jnp.maximum(m_old, m_blk)
            w_old, w_blk = jnp.exp2(m_old - m_new), jnp.exp2(m_blk - m_new)
            acc_ref[rows, :] = acc_ref[rows, :] * w_old + ra_ref[blk, :] * w_blk
            sum_ref[rows, :] = sum_ref[rows, :] * w_old + rl_ref[blk, :] * w_blk
            max_ref[rows, :] = m_new
    o_ref[...] = (acc_ref[...] / sum_ref[...]).astype(BF16)


def _dil_attn(qkv, b, s):
    m = qkv.shape[0]

    def spec(gi, j):
        return pl.BlockSpec((s, LANES), lambda bi, hp: (bi, gi * 12 + 4 * j + hp))

    stat = pltpu.VMEM((s, LANES), F32)
    return pl.pallas_call(
        _dil_attn_kernel, grid=(b, 4),
        in_specs=[spec(gi, j) for gi in range(DIL_GROUPS) for j in range(3)],
        out_specs=pl.BlockSpec((s, LANES), lambda bi, hp: (bi, hp)),
        out_shape=jax.ShapeDtypeStruct((m, 512), BF16),
        scratch_shapes=[stat, pltpu.VMEM((s, LANES), BF16), pltpu.VMEM((s, LANES), BF16),
                        pltpu.VMEM((s, 2 * LANES), BF16), stat, stat, stat, stat, stat, stat],
        compiler_params=_params(("parallel", "parallel"), 40), name="dil_attn",
    )(*([qkv] * 9))


def _l1_out_kernel(o_ref, w_ref, g_ref, h_ref, out_ref):
    out_ref[...] = h_ref[...] + _rms(_dot(o_ref[...], w_ref[...]), g_ref[...])


def _l1_out(o, w, g, h):
    m = h.shape[0]
    row = lambda i: (i, 0)
    const = lambda i: (0, 0)
    return pl.pallas_call(
        _l1_out_kernel, grid=(m // TM,),
        in_specs=[pl.BlockSpec((TM, 512), row), pl.BlockSpec(w.shape, const),
                  pl.BlockSpec(g.shape, const), pl.BlockSpec((TM, D_MODEL), row)],
        out_specs=pl.BlockSpec((TM, D_MODEL), row),
        out_shape=jax.ShapeDtypeStruct((m, D_MODEL), F32),
        compiler_params=_params(("parallel",), 32), name="l1_out",
    )(o, w, g, h)


def _router_kernel(h_ref, gpre_ref, wr_ref, a_ref, idx_ref, gate_ref, cnt_ref, carry_ref):
    @pl.when(pl.program_id(0) == 0)
    def _():
        carry_ref[...] = jnp.zeros_like(carry_ref)

    a = _rms(h_ref[...], gpre_ref[...])
    a_ref[...] = a.astype(BF16)
    logits = jnp.dot(a, wr_ref[...], precision=lax.Precision.HIGHEST, preferred_element_type=F32)
    tm = logits.shape[0]
    lane = _lane_iota(logits.shape)
    lanef = lane.astype(F32)
    lg = jnp.where(lane < N_EXPERTS, logits, -jnp.inf)
    m1 = jnp.max(lg, axis=-1, keepdims=True)
    i1 = jnp.min(jnp.where(lg == m1, lanef, float(LANES)), axis=-1, keepdims=True)
    lg2 = jnp.where(lanef == i1, -jnp.inf, lg)
    m2 = jnp.max(lg2, axis=-1, keepdims=True)
    i2 = jnp.min(jnp.where(lg2 == m2, lanef, float(LANES)), axis=-1, keepdims=True)
    t = jnp.exp(m2 - m1)
    g1 = 1.0 / (1.0 + t)
    g2 = t / (1.0 + t)
    hit1 = lanef == i1
    hit2 = lanef == i2
    onehot = jnp.where(hit1 | hit2, 1.0, 0.0)
    r = lax.broadcasted_iota(jnp.int32, (tm, tm), 0)
    c = lax.broadcasted_iota(jnp.int32, (tm, tm), 1)
    tri = jnp.where(c < r, 1.0, 0.0).astype(BF16)
    before = _dot(tri, onehot.astype(BF16)) + carry_ref[...]
    rank1 = jnp.sum(jnp.where(hit1, before, 0.0), axis=-1, keepdims=True)
    rank2 = jnp.sum(jnp.where(hit2, before, 0.0), axis=-1, keepdims=True)
    info = jnp.where(lane == 0, i1, jnp.where(lane == 1, i2, jnp.where(lane == 2, rank1, rank2)))
    idx_ref[...] = info.astype(jnp.int32)
    gate_ref[...] = jnp.where(lane == 0, g1, g2)
    carry_ref[...] += jnp.sum(onehot, axis=0, keepdims=True)
    cnt_ref[...] = carry_ref[...]


def _router(h, gpre, wr):
    m = h.shape[0]
    row = lambda i: (i, 0)
    const = lambda i: (0, 0)
    return pl.pallas_call(
        _router_kernel, grid=(m // TM,),
        in_specs=[pl.BlockSpec((TM, D_MODEL), row), pl.BlockSpec(gpre.shape, const),
                  pl.BlockSpec(wr.shape, const)],
        out_specs=[pl.BlockSpec((TM, D_MODEL), row), pl.BlockSpec((TM, LANES), row),
                   pl.BlockSpec((TM, LANES), row), pl.BlockSpec((1, LANES), const)],
        out_shape=[jax.ShapeDtypeStruct((m, D_MODEL), BF16),
                   jax.ShapeDtypeStruct((m, LANES), jnp.int32),
                   jax.ShapeDtypeStruct((m, LANES), F32),
                   jax.ShapeDtypeStruct((1, LANES), F32)],
        scratch_shapes=[pltpu.VMEM((1, LANES), F32)],
        compiler_params=_params(("arbitrary",), 32), name="moe_router",
    )(h, gpre, wr)


def _expert_kernel(te_ref, tv_ref, x_ref, wg_ref, wu_ref, wd_ref, y_ref, acc_ref):
    i = pl.program_id(0)
    j = pl.program_id(1)
    last = pl.num_programs(1) - 1
    valid = tv_ref[i] > 0

    @pl.when(valid & (j == 0))
    def _():
        acc_ref[...] = jnp.zeros_like(acc_ref)

    @pl.when(valid)
    def _():
        x = x_ref[...]
        mid = (_silu(_dot(x, wg_ref[...])) * _dot(x, wu_ref[...])).astype(BF16)
        acc_ref[...] += _dot(mid, wd_ref[...])

    @pl.when(valid & (j == last))
    def _():
        y_ref[...] = acc_ref[...]

    @pl.when(jnp.logical_not(valid) & (j == last))
    def _():
        y_ref[...] = jnp.zeros_like(y_ref)


def _experts(tile_e, tile_valid, xs, wg, wu, wd):
    n_slots = xs.shape[0]
    dff = wg.shape[2]
    grid_spec = pltpu.PrefetchScalarGridSpec(
        num_scalar_prefetch=2, grid=(n_slots // TM_MOE, dff // TF),
        in_specs=[pl.BlockSpec((TM_MOE, D_MODEL), lambda i, j, te, tv: (i, 0)),
                  pl.BlockSpec((None, D_MODEL, TF), lambda i, j, te, tv: (te[i], 0, j)),
                  pl.BlockSpec((None, D_MODEL, TF), lambda i, j, te, tv: (te[i], 0, j)),
                  pl.BlockSpec((None, TF, D_MODEL), lambda i, j, te, tv: (te[i], j, 0))],
        out_specs=pl.BlockSpec((TM_MOE, D_MODEL), lambda i, j, te, tv: (i, 0)),
        scratch_shapes=[pltpu.VMEM((TM_MOE, D_MODEL), F32)])
    return pl.pallas_call(
        _expert_kernel, grid_spec=grid_spec,
        out_shape=jax.ShapeDtypeStruct((n_slots, D_MODEL), F32),
        compiler_params=_params(("parallel", "arbitrary"), 32), name="moe_experts",
    )(tile_e, tile_valid, xs, wg, wu, wd)


def _combine_kernel(y0_ref, y1_ref, gate_ref, g_ref, h_ref, out_ref):
    gate = gate_ref[...]
    f = y0_ref[...] * gate[:, 0:1] + y1_ref[...] * gate[:, 1:2]
    out_ref[...] = h_ref[...] + _rms(f, g_ref[...])


def _combine(y0, y1, gate, g, h):
    m = h.shape[0]
    row = lambda i: (i, 0)
    const = lambda i: (0, 0)
    act = pl.BlockSpec((TM, D_MODEL), row)
    return pl.pallas_call(
        _combine_kernel, grid=(m // TM,),
        in_specs=[act, act, pl.BlockSpec((TM, LANES), row), pl.BlockSpec(g.shape, const), act],
        out_specs=act,
        out_shape=jax.ShapeDtypeStruct((m, D_MODEL), F32),
        compiler_params=_params(("parallel",), 32), name="moe_combine",
    )(y0, y1, gate, g, h)


def _moe(h, gpre, wr, wg, wu, wd, gpost):
    n = h.shape[0]
    a, idx, gate, cnt = _router(h, gpre, wr)
    counts = cnt[0, :N_EXPERTS].astype(jnp.int32)
    padded = (counts + TM_MOE - 1) // TM_MOE * TM_MOE
    end_pad = jnp.cumsum(padded)
    start_pad = end_pad - padded
    dest = start_pad[idx[:, 0:2]] + idx[:, 2:4]
    n_slots = n * TOP_K + N_EXPERTS * TM_MOE
    tok = jnp.repeat(jnp.arange(n, dtype=jnp.int32), TOP_K)
    slot_tok = jnp.zeros((n_slots,), jnp.int32).at[dest.reshape(-1)].set(tok)
    tile_start = jnp.arange(n_slots // TM_MOE, dtype=jnp.int32) * TM_MOE
    tile_e = jnp.minimum(jnp.searchsorted(end_pad, tile_start, side='right'),
                         N_EXPERTS - 1).astype(jnp.int32)
    tile_valid = (tile_start < end_pad[-1]).astype(jnp.int32)
    xs = a[slot_tok]
    ys = _experts(tile_e, tile_valid, xs, wg, wu, wd)
    return _combine(ys[dest[:, 0]], ys[dest[:, 1]], gate, gpost, h)


def _rotary_tables(pos, dim, theta):
    exponent = jnp.arange(0, dim, 2, dtype=F32) / dim
    inv_freq = jnp.power(jnp.float32(theta), -exponent)
    ang = pos.astype(F32)[:, None] * inv_freq[None, :]
    return jnp.cos(ang), jnp.sin(ang)


def _tables(s):
    t = jnp.arange(s, dtype=jnp.int32)
    cr, sr = _rotary_tables(t // GRID_W, HEAD_DIM // 2, AXIAL_THETA)
    cc, sc = _rotary_tables(t % GRID_W, HEAD_DIM // 2, AXIAL_THETA)
    ca = jnp.tile(jnp.concatenate([cr, cr, cc, cc], axis=1), (1, 2))
    sa = jnp.tile(jnp.concatenate([-sr, sr, -sc, sc], axis=1), (1, 2))
    cm_, sm_ = _rotary_tables(t, MLA_ROPE_DIM, MLA_ROPE_THETA)
    one = jnp.ones((s, 1), F32)
    zero = jnp.zeros((s, 1), F32)
    cm = jnp.concatenate([jnp.tile(one, (1, 64)), cm_, cm_, jnp.tile(one, (1, 32))], axis=1)
    sm = jnp.concatenate([jnp.tile(zero, (1, 64)), -sm_, sm_, jnp.tile(zero, (1, 32))], axis=1)
    ckr = jnp.concatenate([cm_, cm_, jnp.tile(zero, (1, 96))], axis=1)
    skr = jnp.concatenate([-sm_, sm_, jnp.tile(zero, (1, 96))], axis=1)
    cp_, sp_ = _rotary_tables(t, PARTIAL_ROPE_DIM, ROPE_THETA)
    cp = jnp.tile(jnp.concatenate([cp_, cp_, jnp.tile(one, (1, 48))], axis=1), (1, 2))
    sp = jnp.tile(jnp.concatenate([-sp_, sp_, jnp.tile(zero, (1, 48))], axis=1), (1, 2))
    scale = HEAD_DIM ** -0.5 * LOG2_E
    ctab = jnp.stack([cp * scale, cp, jnp.ones_like(cp)])
    stab = jnp.stack([sp * scale, sp, jnp.zeros_like(sp)])
    return ca, sa, cm, sm, ckr, skr, ctab, stab


_GQA_PERM = (0, 4, 1, 5, 2, 6, 3, 7)


def _l0_weights(w_in, w_uq, w_ukv, w_out):
    d = w_in.shape[0]
    q = w_in[:, :512].reshape(d, 8, 64)[:, _GQA_PERM, :].reshape(d, 512)
    win = jnp.concatenate([q, w_in[:, 512:], jnp.zeros((d, AB_IN_PAD - w_in.shape[1]), w_in.dtype)],
                          axis=1).astype(BF16)
    uq = w_uq.reshape(MLA_Q_RANK, MLA_HEADS, MLA_NOPE_DIM + MLA_ROPE_DIM)
    wuq = jnp.pad(uq, ((0, 0), (0, 0), (0, LANES - uq.shape[2]))).reshape(MLA_Q_RANK, -1).astype(BF16)
    ukv = w_ukv.reshape(MLA_KV_RANK, MLA_HEADS, MLA_NOPE_DIM + MLA_V_DIM)
    wk = jnp.pad(ukv[:, :, :MLA_NOPE_DIM], ((0, 0), (0, 0), (0, LANES - MLA_NOPE_DIM)))
    wk = wk.reshape(MLA_KV_RANK, -1).astype(BF16)
    wv = ukv[:, :, MLA_NOPE_DIM:].reshape(MLA_KV_RANK, -1).astype(BF16)
    src = jnp.arange(LANES)[:, None]
    dst = jnp.arange(MLA_HEADS * LANES)[None, :]
    wp = ((src < MLA_ROPE_DIM) & (dst % LANES == src + MLA_NOPE_DIM)).astype(BF16)
    oa = w_out[:512].reshape(8, 64, -1)[_GQA_PERM, :, :].reshape(512, -1)
    wout = jnp.concatenate([oa, w_out[512:]], axis=0).astype(BF16)
    return win, wuq, wk, wp, wv, wout


def kernel(x, p, mix_pre_g, mix_post_g, ffn_pre_g, ffn_post_g, ple_w_proj, ple_gate_norm_g, ple_w_gate, ple_post_g, ab_w_in, gqa_q_norm_g, gqa_k_norm_g, mla_q_norm_g, mla_w_uq, mla_kv_norm_g, mla_w_ukv, ab_w_out, ffn_w_gate, ffn_w_up, ffn_w_down, dil_w_qkv, dil_w_out, moe_w_router, moe_w_gate, moe_w_up, moe_w_down):
    b, s, d = x.shape
    m = b * s
    row = lambda g: g.reshape(1, -1).astype(F32)
    ca, sa, cm, sm, ckr, skr, ctab, stab = _tables(s)
    h = x.reshape(m, d)
    pf = p.reshape(p.shape[0], m, PLE_DIM)

    win, wuq, wk, wp, wv, wout = _l0_weights(ab_w_in[0], mla_w_uq[0], mla_w_ukv[0], ab_w_out[0])
    gq = jnp.tile(row(gqa_q_norm_g[0]), (1, 2))
    gk = jnp.tile(row(gqa_k_norm_g[0]), (1, 2))
    qa, kat, va, qm, kmt, vm = _l0_in(h, row(mix_pre_g[0]), win, gq, gk, ca, sa,
                                      row(mla_q_norm_g[0]), wuq, cm, sm,
                                      row(mla_kv_norm_g[0]), wk, wp, wv, ckr, skr, b, s)
    oa = _gqa_attn(qa, kat, va, b, s)
    om = _mla_attn(qm, kmt, vm, b, s)
    h = _l0_out(oa, om, wout, row(mix_post_g[0]), h)
    h = _ffn(h, row(ffn_pre_g[0]), ffn_w_gate[0].astype(BF16), ffn_w_up[0].astype(BF16),
             ffn_w_down[0].astype(BF16), row(ffn_post_g[0]))
    h = _ple(h, pf[0], row(ple_gate_norm_g[0]), ple_w_gate[0].astype(BF16),
             ple_w_proj[0].astype(BF16), row(ple_post_g[0]))

    qkv = _l1_qkv(h, row(mix_pre_g[1]), dil_w_qkv[0].astype(BF16), ctab, stab, s)
    o = _dil_attn(qkv, b, s)
    h = _l1_out(o, dil_w_out[0].astype(BF16), row(mix_post_g[1]), h)
    wr = jnp.pad(moe_w_router[0].astype(F32), ((0, 0), (0, LANES - N_EXPERTS)))
    h = _moe(h, row(ffn_pre_g[1]), wr, moe_w_gate[0].astype(BF16), moe_w_up[0].astype(BF16),
             moe_w_down[0].astype(BF16), row(ffn_post_g[1]))
    h = _ple(h, pf[1], row(ple_gate_norm_g[1]), ple_w_gate[1].astype(BF16),
             ple_w_proj[1].astype(BF16), row(ple_post_g[1]))
    return h.reshape(b, s, d)
```

```python
import functools

import jax
import jax.numpy as jnp
from jax import lax
from jax.experimental import pallas as pl
from jax.experimental.pallas import tpu as pltpu

F32 = jnp.float32
BF16 = jnp.bfloat16

D_MODEL = 1024
GRID_W = 64
HEAD_DIM = 64
NORM_EPS = 1e-6
NEG_INF = -1e30
ROPE_THETA = 500000.0
PARTIAL_ROPE_DIM = HEAD_DIM // 4
GQA_Q_HEADS = 8
GQA_KV_HEADS = 2
AXIAL_THETA = 10000.0
MLA_HEADS = 8
MLA_Q_RANK = 256
MLA_KV_RANK = 128
MLA_NOPE_DIM = 64
MLA_ROPE_DIM = 32
MLA_V_DIM = 64
MLA_ROPE_THETA = 10000.0
DIL_CONFIGS = ((128, 1), (512, 4), (2048, 16))
DIL_GROUPS = len(DIL_CONFIGS)
DIL_HEADS = 8
DIL_HALF = 64
DIL_QB = 128
LOG2_E = 1.4426950408889634
DIL_IN_WIDTH = DIL_GROUPS * 3 * DIL_HEADS * HEAD_DIM
D_FF = 3584
N_EXPERTS = 8
TOP_K = 2
PLE_DIM = 256

LANES = 128
VMEM_BYTES = 64 * 1024 * 1024
MIB = 1024 * 1024

TM = 512
TQ = 512
KEY_CHUNK = 512
V_ONES = 16
V_ROWS = LANES + V_ONES
TF = 512
TM_FFN = 1024
TM_MOE = 1024
AB_IN_PAD = 1280


def _params(semantics, vmem_mib):
    return pltpu.CompilerParams(dimension_semantics=semantics,
                                vmem_limit_bytes=vmem_mib * MIB)


def _dot(a, b):
    return jnp.dot(a, b, preferred_element_type=F32)


def _rms(x, g):
    return x * lax.rsqrt(jnp.mean(x * x, axis=-1, keepdims=True) + NORM_EPS) * g


def _lane_iota(shape):
    return lax.broadcasted_iota(jnp.int32, shape, len(shape) - 1)


def _swap_halves(x, k):
    w = x.shape[-1]
    fwd = pltpu.roll(x, w - k, 1)
    bwd = pltpu.roll(x, k, 1)
    return jnp.where((_lane_iota(x.shape) % (2 * k)) < k, fwd, bwd)


def _rotate(x, c, s, k):
    return x * c + _swap_halves(x, k) * s


def _head_rms(x, g):
    lo = _lane_iota(x.shape) < HEAD_DIM
    x2 = x * x
    s_all = jnp.sum(x2, axis=-1, keepdims=True)
    s_lo = jnp.sum(jnp.where(lo, x2, 0.0), axis=-1, keepdims=True)
    ms = jnp.where(lo, s_lo, s_all - s_lo) * (1.0 / HEAD_DIM)
    return x * lax.rsqrt(ms + NORM_EPS) * g


def _l0_in_kernel(h_ref, gpre_ref, win_ref, gq_ref, gk_ref, ca_ref, sa_ref,
                  gmq_ref, wuq_ref, cm_ref, sm_ref, gmkv_ref, wk_ref, wp_ref, wv_ref,
                  ckr_ref, skr_ref,
                  qa_ref, ka_ref, vat_ref, qm_ref, km_ref, vmt_ref):
    a = _rms(h_ref[...], gpre_ref[...]).astype(BF16)
    z = _dot(a, win_ref[...])
    ca, sa = ca_ref[...], sa_ref[...]
    for j in range(4):
        xq = _head_rms(z[:, j * LANES:(j + 1) * LANES], gq_ref[...])
        qa_ref[:, j * LANES:(j + 1) * LANES] = (
            _rotate(xq, ca, sa, 16) * (HEAD_DIM ** -0.5 * LOG2_E)).astype(BF16)
    ka_ref[...] = _rotate(_head_rms(z[:, 512:640], gk_ref[...]), ca, sa, 16).astype(BF16)
    ones = jnp.ones((V_ONES, z.shape[0]), BF16)
    vat_ref[0:LANES, :] = z[:, 640:768].T.astype(BF16)
    vat_ref[LANES:, :] = ones
    cq = _rms(z[:, 768:1024], gmq_ref[...]).astype(BF16)
    qm = _dot(cq, wuq_ref[...])
    cm, sm = cm_ref[...], sm_ref[...]
    scale_m = (MLA_NOPE_DIM + MLA_ROPE_DIM) ** -0.5 * LOG2_E
    for hb in range(MLA_HEADS):
        x = qm[:, hb * LANES:(hb + 1) * LANES]
        qm_ref[:, hb * LANES:(hb + 1) * LANES] = (_rotate(x, cm, sm, 16) * scale_m).astype(BF16)
    ckv = _rms(z[:, 1024:1152], gmkv_ref[...]).astype(BF16)
    kr = _rotate(z[:, 1152:1280], ckr_ref[...], skr_ref[...], 16).astype(BF16)
    km = _dot(ckv, wk_ref[...]) + _dot(kr, wp_ref[...])
    km_ref[...] = km.astype(BF16)
    vm = _dot(ckv, wv_ref[...])
    for j in range(4):
        vmt_ref[j * V_ROWS:j * V_ROWS + LANES, :] = vm[:, j * LANES:(j + 1) * LANES].T.astype(BF16)
        vmt_ref[j * V_ROWS + LANES:(j + 1) * V_ROWS, :] = ones


def _l0_in(h, gpre, win, gq, gk, ca, sa, gmq, wuq, cm, sm, gmkv, wk, wp, wv, ckr, skr, b, s):
    m = h.shape[0]
    nt = s // TM
    row = lambda i: (i, 0)
    const = lambda i: (0, 0)
    tab = lambda i: (i % nt, 0)
    full = lambda a: pl.BlockSpec(a.shape, const)
    in_specs = [pl.BlockSpec((TM, D_MODEL), row), full(gpre), full(win), full(gq), full(gk),
                pl.BlockSpec((TM, LANES), tab), pl.BlockSpec((TM, LANES), tab),
                full(gmq), full(wuq), pl.BlockSpec((TM, LANES), tab), pl.BlockSpec((TM, LANES), tab),
                full(gmkv), full(wk), full(wp), full(wv),
                pl.BlockSpec((TM, LANES), tab), pl.BlockSpec((TM, LANES), tab)]
    tr = lambda i: (i // nt, 0, i % nt)
    out_specs = [pl.BlockSpec((TM, 512), row),
                 pl.BlockSpec((TM, LANES), row),
                 pl.BlockSpec((None, V_ROWS, TM), tr),
                 pl.BlockSpec((TM, 1024), row),
                 pl.BlockSpec((TM, 1024), row),
                 pl.BlockSpec((None, 4 * V_ROWS, TM), tr)]
    out_shape = [jax.ShapeDtypeStruct((m, 512), BF16),
                 jax.ShapeDtypeStruct((m, LANES), BF16),
                 jax.ShapeDtypeStruct((b, V_ROWS, s), BF16),
                 jax.ShapeDtypeStruct((m, 1024), BF16),
                 jax.ShapeDtypeStruct((m, 1024), BF16),
                 jax.ShapeDtypeStruct((b, 4 * V_ROWS, s), BF16)]
    return pl.pallas_call(
        _l0_in_kernel, grid=(m // TM,), in_specs=in_specs, out_specs=out_specs,
        out_shape=out_shape, compiler_params=_params(("parallel",), 48), name="l0_in",
    )(h, gpre, win, gq, gk, ca, sa, gmq, wuq, cm, sm, gmkv, wk, wp, wv, ckr, skr)


def _attend_pair(score_t, vt_ref, o_ref):
    n_chunks = vt_ref.shape[1] // KEY_CHUNK
    tq = o_ref.shape[0]
    keys = [slice(c * KEY_CHUNK, (c + 1) * KEY_CHUNK) for c in range(n_chunks)]
    parts = []
    s_next = score_t(keys[0])
    for c in range(n_chunks):
        s_t = s_next
        if c + 1 < n_chunks:
            s_next = score_t(keys[c + 1])
        m = jnp.max(s_t, axis=0, keepdims=True)
        parts.append((m, _dot(vt_ref[:, keys[c]], jnp.exp2(s_t - m).astype(BF16))))
    m_all = functools.reduce(jnp.maximum, [m for m, _ in parts])
    o_t = sum(o_c * jnp.exp2(m_c - m_all) for m_c, o_c in parts)
    o_t = o_t[0:LANES, :] / o_t[LANES:LANES + 1, :]
    first = lax.broadcasted_iota(jnp.int32, (LANES, tq), 0) < HEAD_DIM
    o_ref[...] = jnp.where(first, o_t[:, 0:tq], o_t[:, tq:]).T.astype(BF16)


def _transposed(q):
    return q.astype(F32).T.astype(BF16)


def _gqa_attn_kernel(q_ref, k_ref, vt_ref, o_ref):
    qt = _transposed(q_ref[...])
    first = lax.broadcasted_iota(jnp.int32, qt.shape, 0) < HEAD_DIM
    zero = jnp.zeros_like(qt)
    w = jnp.concatenate([jnp.where(first, qt, zero), jnp.where(first, zero, qt)], axis=1)
    _attend_pair(lambda keys: _dot(k_ref[keys, :], w), vt_ref, o_ref)


def _gqa_attn(qa, ka, vat, b, s):
    m = qa.shape[0]
    nq = s // TQ
    return pl.pallas_call(
        _gqa_attn_kernel, grid=(b, nq, 4),
        in_specs=[pl.BlockSpec((TQ, LANES), lambda bi, i, j: (bi * nq + i, j)),
                  pl.BlockSpec((s, LANES), lambda bi, i, j: (bi, 0)),
                  pl.BlockSpec((None, V_ROWS, s), lambda bi, i, j: (bi, 0, 0))],
        out_specs=pl.BlockSpec((TQ, LANES), lambda bi, i, j: (bi * nq + i, j)),
        out_shape=jax.ShapeDtypeStruct((m, 512), BF16),
        compiler_params=_params(("parallel", "parallel", "parallel"), 32), name="gqa_attn",
    )(qa, ka, vat)


def _mla_attn_kernel(q_ref, k_ref, vt_ref, o_ref):
    qt0 = _transposed(q_ref[:, 0:LANES])
    qt1 = _transposed(q_ref[:, LANES:2 * LANES])

    def score_t(keys):
        return jnp.concatenate([_dot(k_ref[keys, 0:LANES], qt0),
                                _dot(k_ref[keys, LANES:2 * LANES], qt1)], axis=1)

    _attend_pair(score_t, vt_ref, o_ref)


def _mla_attn(qm, km, vmt, b, s):
    m = qm.shape[0]
    nq = s // TQ
    return pl.pallas_call(
        _mla_attn_kernel, grid=(b, nq, 4),
        in_specs=[pl.BlockSpec((TQ, 2 * LANES), lambda bi, i, j: (bi * nq + i, j)),
                  pl.BlockSpec((s, 2 * LANES), lambda bi, i, j: (bi, j)),
                  pl.BlockSpec((None, V_ROWS, s), lambda bi, i, j: (bi, j, 0))],
        out_specs=pl.BlockSpec((TQ, LANES), lambda bi, i, j: (bi * nq + i, j)),
        out_shape=jax.ShapeDtypeStruct((m, 512), BF16),
        compiler_params=_params(("parallel", "parallel", "parallel"), 32), name="mla_attn",
    )(qm, km, vmt)


def _l0_out_kernel(oa_ref, om_ref, w_ref, g_ref, h_ref, out_ref):
    k = oa_ref.shape[1]
    y = _dot(oa_ref[...], w_ref[0:k, :]) + _dot(om_ref[...], w_ref[k:, :])
    out_ref[...] = h_ref[...] + _rms(y, g_ref[...])


def _l0_out(oa, om, w, g, h):
    m = h.shape[0]
    row = lambda i: (i, 0)
    const = lambda i: (0, 0)
    return pl.pallas_call(
        _l0_out_kernel, grid=(m // TM,),
        in_specs=[pl.BlockSpec((TM, oa.shape[1]), row), pl.BlockSpec((TM, om.shape[1]), row),
                  pl.BlockSpec(w.shape, const), pl.BlockSpec(g.shape, const),
                  pl.BlockSpec((TM, D_MODEL), row)],
        out_specs=pl.BlockSpec((TM, D_MODEL), row),
        out_shape=jax.ShapeDtypeStruct((m, D_MODEL), F32),
        compiler_params=_params(("parallel",), 32), name="l0_out",
    )(oa, om, w, g, h)


def _silu(x):
    return x * (1.0 / (1.0 + jnp.exp(-x)))


def _ffn_kernel(h_ref, gpre_ref, wg_ref, wu_ref, wd_ref, gpost_ref, out_ref, a_ref, acc_ref):
    j = pl.program_id(1)

    @pl.when(j == 0)
    def _():
        a_ref[...] = _rms(h_ref[...], gpre_ref[...]).astype(BF16)
        acc_ref[...] = jnp.zeros_like(acc_ref)

    a = a_ref[...]
    mid = (_silu(_dot(a, wg_ref[...])) * _dot(a, wu_ref[...])).astype(BF16)
    acc_ref[...] += _dot(mid, wd_ref[...])

    @pl.when(j == pl.num_programs(1) - 1)
    def _():
        out_ref[...] = h_ref[...] + _rms(acc_ref[...], gpost_ref[...])


def _ffn(h, gpre, wg, wu, wd, gpost):
    m = h.shape[0]
    dff = wg.shape[1]
    row = lambda i, j: (i, 0)
    const = lambda i, j: (0, 0)
    return pl.pallas_call(
        _ffn_kernel, grid=(m // TM_FFN, dff // TF),
        in_specs=[pl.BlockSpec((TM_FFN, D_MODEL), row), pl.BlockSpec(gpre.shape, const),
                  pl.BlockSpec((D_MODEL, TF), lambda i, j: (0, j)),
                  pl.BlockSpec((D_MODEL, TF), lambda i, j: (0, j)),
                  pl.BlockSpec((TF, D_MODEL), lambda i, j: (j, 0)),
                  pl.BlockSpec(gpost.shape, const)],
        out_specs=pl.BlockSpec((TM_FFN, D_MODEL), row),
        out_shape=jax.ShapeDtypeStruct((m, D_MODEL), F32),
        scratch_shapes=[pltpu.VMEM((TM_FFN, D_MODEL), BF16), pltpu.VMEM((TM_FFN, D_MODEL), F32)],
        compiler_params=_params(("parallel", "arbitrary"), 48), name="ffn",
    )(h, gpre, wg, wu, wd, gpost)


def _ple_kernel(h_ref, p_ref, ggn_ref, wg_ref, wp_ref, gpost_ref, out_ref):
    h = h_ref[...]
    gate = _dot(_rms(h, ggn_ref[...]).astype(BF16), wg_ref[...])
    gate = 1.0 / (1.0 + jnp.exp(-gate))
    e = _dot(p_ref[...].astype(BF16), wp_ref[...]) * gate
    out_ref[...] = h + _rms(e, gpost_ref[...])


def _ple(h, p, ggn, wg, wp, gpost):
    m = h.shape[0]
    row = lambda i: (i, 0)
    const = lambda i: (0, 0)
    return pl.pallas_call(
        _ple_kernel, grid=(m // TM,),
        in_specs=[pl.BlockSpec((TM, D_MODEL), row), pl.BlockSpec((TM, PLE_DIM), row),
                  pl.BlockSpec(ggn.shape, const), pl.BlockSpec(wg.shape, const),
                  pl.BlockSpec(wp.shape, const), pl.BlockSpec(gpost.shape, const)],
        out_specs=pl.BlockSpec((TM, D_MODEL), row),
        out_shape=jax.ShapeDtypeStruct((m, D_MODEL), F32),
        compiler_params=_params(("parallel",), 32), name="ple",
    )(h, p, ggn, wg, wp, gpost)


def _l1_qkv_kernel(h_ref, gpre_ref, w_ref, c_ref, s_ref, out_ref):
    a = _rms(h_ref[...], gpre_ref[...]).astype(BF16)
    for blk in range(DIL_IN_WIDTH // 512):
        cols = slice(blk * 512, (blk + 1) * 512)
        z = _dot(a, w_ref[:, cols])
        kind = blk % 3
        if kind == 2:
            out_ref[:, cols] = z.astype(BF16)
            continue
        c, s = c_ref[kind], s_ref[kind]
        for sub in range(4):
            x = z[:, sub * LANES:(sub + 1) * LANES]
            out_ref[:, blk * 512 + sub * LANES:blk * 512 + (sub + 1) * LANES] = (
                _rotate(x, c, s, PARTIAL_ROPE_DIM // 2).astype(BF16))


def _l1_qkv(h, gpre, w, ctab, stab, s):
    m = h.shape[0]
    nt = s // TM
    n = w.shape[1]
    const = lambda i: (0, 0)
    return pl.pallas_call(
        _l1_qkv_kernel, grid=(m // TM,),
        in_specs=[pl.BlockSpec((TM, D_MODEL), lambda i: (i, 0)),
                  pl.BlockSpec(gpre.shape, const),
                  pl.BlockSpec(w.shape, const, pipeline_mode=pl.Buffered(1)),
                  pl.BlockSpec((2, TM, LANES), lambda i: (0, i % nt, 0)),
                  pl.BlockSpec((2, TM, LANES), lambda i: (0, i % nt, 0))],
        out_specs=pl.BlockSpec((TM, n), lambda i: (i, 0)),
        out_shape=jax.ShapeDtypeStruct((m, n), BF16),
        compiler_params=_params(("parallel",), 48), name="l1_qkv",
    )(h, gpre, w, ctab, stab)


def _dil_attn_kernel(q0_ref, k0_ref, v0_ref, q1_ref, k1_ref, v1_ref, q2_ref, k2_ref, v2_ref, o_ref,
                     stage_ref, qd_ref, kd_ref, vd_ref, ra_ref, rl_ref, rm_ref,
                     acc_ref, sum_ref, max_ref):
    seq = o_ref.shape[0]
    lo = _lane_iota((DIL_QB, LANES)) < HEAD_DIM
    vd_ref[:, LANES:2 * LANES] = jnp.ones((seq, LANES), BF16)
    groups = ((q0_ref, k0_ref, v0_ref), (q1_ref, k1_ref, v1_ref), (q2_ref, k2_ref, v2_ref))
    for gi, (q_ref, k_ref, v_ref) in enumerate(groups):
        dil = DIL_CONFIGS[gi][1]
        length = seq // dil
        width = min(DIL_QB + 2 * DIL_HALF, length)
        nblk = length // DIL_QB
        for src, dst, c1 in ((q_ref, qd_ref, LANES), (k_ref, kd_ref, LANES), (v_ref, vd_ref, LANES)):
            if dil == 1:
                dst[:, 0:c1] = src[...]
            else:
                stage_ref[...] = src[...].astype(F32)
                for r in range(dil):
                    dst[r * length:(r + 1) * length, 0:c1] = (
                        stage_ref[pl.ds(r, length, stride=dil), :].astype(BF16))
        first = gi == 0
        ta, tl, tm_ = (acc_ref, sum_ref, max_ref) if first else (ra_ref, rl_ref, rm_ref)
        row = lax.broadcasted_iota(jnp.int32, (2 * DIL_QB, width), 0) % DIL_QB
        col = lax.broadcasted_iota(jnp.int32, (2 * DIL_QB, width), 1)
        rel = row - col

        def body(it, carry, length=length, width=width, nblk=nblk, ta=ta, tl=tl, tm_=tm_, rel=rel):
            r = it // nblk
            i = it % nblk
            q0 = i * DIL_QB
            start = jnp.clip(q0 - DIL_HALF, 0, length - width)
            qrow = pl.multiple_of(r * length + q0, DIL_HALF)
            krow = pl.multiple_of(r * length + start, DIL_HALF)
            q = qd_ref[pl.ds(qrow, DIL_QB), :]
            kw = kd_ref[pl.ds(krow, width), :]
            vw = vd_ref[pl.ds(krow, width), :]
            zero = jnp.zeros_like(q)
            q2 = jnp.concatenate([jnp.where(lo, q, zero), jnp.where(lo, zero, q)], axis=0)
            sc = lax.dot_general(q2, kw, (((1,), (1,)), ((), ())), preferred_element_type=F32)
            sc = jnp.where(jnp.abs(rel + (q0 - start)) <= DIL_HALF, sc, NEG_INF)
            m = jnp.max(sc, axis=-1, keepdims=True)
            p = jnp.exp2(sc - m).astype(BF16)
            o2 = _dot(p, vw)
            ta[pl.ds(qrow, DIL_QB), :] = jnp.where(lo, o2[:DIL_QB, 0:LANES], o2[DIL_QB:, 0:LANES])
            tl[pl.ds(qrow, DIL_QB), :] = jnp.where(lo, o2[:DIL_QB, LANES:], o2[DIL_QB:, LANES:])
            tm_[pl.ds(qrow, DIL_QB), :] = jnp.where(lo, m[:DIL_QB], m[DIL_QB:])
            return carry

        lax.fori_loop(0, dil * nblk, body, 0, unroll=4)
        if first:
            continue
        for r in range(dil):
            rows = pl.ds(r, length, stride=dil)
            blk = slice(r * length, (r + 1) * length)
            m_old, m_blk = max_ref[rows, :], rm_ref[blk, :]
            m_new = jnp.maximum(m_old, m_blk)
            w_old, w_blk = jnp.exp2(m_old - m_new), jnp.exp2(m_blk - m_new)
            acc_ref[rows, :] = acc_ref[rows, :] * w_old + ra_ref[blk, :] * w_blk
            sum_ref[rows, :] = sum_ref[rows, :] * w_old + rl_ref[blk, :] * w_blk
            max_ref[rows, :] = m_new
    o_ref[...] = (acc_ref[...] / sum_ref[...]).astype(BF16)


def _dil_attn(qkv, b, s):
    m = qkv.shape[0]

    def spec(gi, j):
        return pl.BlockSpec((s, LANES), lambda bi, hp: (bi, gi * 12 + 4 * j + hp))

    stat = pltpu.VMEM((s, LANES), F32)
    return pl.pallas_call(
        _dil_attn_kernel, grid=(b, 4),
        in_specs=[spec(gi, j) for gi in range(DIL_GROUPS) for j in range(3)],
        out_specs=pl.BlockSpec((s, LANES), lambda bi, hp: (bi, hp)),
        out_shape=jax.ShapeDtypeStruct((m, 512), BF16),
        scratch_shapes=[stat, pltpu.VMEM((s, LANES), BF16), pltpu.VMEM((s, LANES), BF16),
                        pltpu.VMEM((s, 2 * LANES), BF16), stat, stat, stat, stat, stat, stat],
        compiler_params=_params(("parallel", "parallel"), 40), name="dil_attn",
    )(*([qkv] * 9))


def _l1_out_kernel(o_ref, w_ref, g_ref, h_ref, out_ref):
    out_ref[...] = h_ref[...] + _rms(_dot(o_ref[...], w_ref[...]), g_ref[...])


def _l1_out(o, w, g, h):
    m = h.shape[0]
    row = lambda i: (i, 0)
    const = lambda i: (0, 0)
    return pl.pallas_call(
        _l1_out_kernel, grid=(m // TM,),
        in_specs=[pl.BlockSpec((TM, 512), row), pl.BlockSpec(w.shape, const),
                  pl.BlockSpec(g.shape, const), pl.BlockSpec((TM, D_MODEL), row)],
        out_specs=pl.BlockSpec((TM, D_MODEL), row),
        out_shape=jax.ShapeDtypeStruct((m, D_MODEL), F32),
        compiler_params=_params(("parallel",), 32), name="l1_out",
    )(o, w, g, h)


def _router_kernel(h_ref, gpre_ref, wr_ref, a_ref, idx_ref, gate_ref, cnt_ref, carry_ref):
    @pl.when(pl.program_id(0) == 0)
    def _():
        carry_ref[...] = jnp.zeros_like(carry_ref)

    a = _rms(h_ref[...], gpre_ref[...])
    a_ref[...] = a.astype(BF16)
    logits = jnp.dot(a, wr_ref[...], precision=lax.Precision.HIGHEST, preferred_element_type=F32)
    tm = logits.shape[0]
    lane = _lane_iota(logits.shape)
    lanef = lane.astype(F32)
    lg = jnp.where(lane < N_EXPERTS, logits, -jnp.inf)
    m1 = jnp.max(lg, axis=-1, keepdims=True)
    i1 = jnp.min(jnp.where(lg == m1, lanef, float(LANES)), axis=-1, keepdims=True)
    lg2 = jnp.where(lanef == i1, -jnp.inf, lg)
    m2 = jnp.max(lg2, axis=-1, keepdims=True)
    i2 = jnp.min(jnp.where(lg2 == m2, lanef, float(LANES)), axis=-1, keepdims=True)
    t = jnp.exp(m2 - m1)
    g1 = 1.0 / (1.0 + t)
    g2 = t / (1.0 + t)
    hit1 = lanef == i1
    hit2 = lanef == i2
    onehot = jnp.where(hit1 | hit2, 1.0, 0.0)
    r = lax.broadcasted_iota(jnp.int32, (tm, tm), 0)
    c = lax.broadcasted_iota(jnp.int32, (tm, tm), 1)
    tri = jnp.where(c < r, 1.0, 0.0).astype(BF16)
    before = _dot(tri, onehot.astype(BF16)) + carry_ref[...]
    rank1 = jnp.sum(jnp.where(hit1, before, 0.0), axis=-1, keepdims=True)
    rank2 = jnp.sum(jnp.where(hit2, before, 0.0), axis=-1, keepdims=True)
    info = jnp.where(lane == 0, i1, jnp.where(lane == 1, i2, jnp.where(lane == 2, rank1, rank2)))
    idx_ref[...] = info.astype(jnp.int32)
    gate_ref[...] = jnp.where(lane == 0, g1, g2)
    carry_ref[...] += jnp.sum(onehot, axis=0, keepdims=True)
    cnt_ref[...] = carry_ref[...]


def _router(h, gpre, wr):
    m = h.shape[0]
    row = lambda i: (i, 0)
    const = lambda i: (0, 0)
    return pl.pallas_call(
        _router_kernel, grid=(m // TM,),
        in_specs=[pl.BlockSpec((TM, D_MODEL), row), pl.BlockSpec(gpre.shape, const),
                  pl.BlockSpec(wr.shape, const)],
        out_specs=[pl.BlockSpec((TM, D_MODEL), row), pl.BlockSpec((TM, LANES), row),
                   pl.BlockSpec((TM, LANES), row), pl.BlockSpec((1, LANES), const)],
        out_shape=[jax.ShapeDtypeStruct((m, D_MODEL), BF16),
                   jax.ShapeDtypeStruct((m, LANES), jnp.int32),
                   jax.ShapeDtypeStruct((m, LANES), F32),
                   jax.ShapeDtypeStruct((1, LANES), F32)],
        scratch_shapes=[pltpu.VMEM((1, LANES), F32)],
        compiler_params=_params(("arbitrary",), 32), name="moe_router",
    )(h, gpre, wr)


def _expert_kernel(te_ref, tv_ref, x_ref, wg_ref, wu_ref, wd_ref, y_ref, acc_ref):
    i = pl.program_id(0)
    j = pl.program_id(1)
    last = pl.num_programs(1) - 1
    valid = tv_ref[i] > 0

    @pl.when(valid & (j == 0))
    def _():
        acc_ref[...] = jnp.zeros_like(acc_ref)

    @pl.when(valid)
    def _():
        x = x_ref[...]
        mid = (_silu(_dot(x, wg_ref[...])) * _dot(x, wu_ref[...])).astype(BF16)
        acc_ref[...] += _dot(mid, wd_ref[...])

    @pl.when(valid & (j == last))
    def _():
        y_ref[...] = acc_ref[...].astype(BF16)

    @pl.when(jnp.logical_not(valid) & (j == last))
    def _():
        y_ref[...] = jnp.zeros_like(y_ref)


def _experts(tile_e, tile_valid, xs, wg, wu, wd):
    n_slots = xs.shape[0]
    wspec = lambda shape: pl.BlockSpec(shape, lambda i, j, te, tv: (te[i], j * tv[i], 0, 0))
    grid_spec = pltpu.PrefetchScalarGridSpec(
        num_scalar_prefetch=2, grid=(n_slots // TM_MOE, wg.shape[1]),
        in_specs=[pl.BlockSpec((TM_MOE, D_MODEL), lambda i, j, te, tv: (i, 0)),
                  wspec((None, None, D_MODEL, TF)), wspec((None, None, D_MODEL, TF)),
                  wspec((None, None, TF, D_MODEL))],
        out_specs=pl.BlockSpec((TM_MOE, D_MODEL), lambda i, j, te, tv: (i, 0)),
        scratch_shapes=[pltpu.VMEM((TM_MOE, D_MODEL), F32)])
    return pl.pallas_call(
        _expert_kernel, grid_spec=grid_spec,
        out_shape=jax.ShapeDtypeStruct((n_slots, D_MODEL), BF16),
        compiler_params=_params(("parallel", "arbitrary"), 40), name="moe_experts",
    )(tile_e, tile_valid, xs, wg, wu, wd)


def _combine_kernel(y0_ref, y1_ref, gate_ref, g_ref, h_ref, out_ref):
    gate = gate_ref[...]
    f = y0_ref[...].astype(F32) * gate[:, 0:1] + y1_ref[...].astype(F32) * gate[:, 1:2]
    out_ref[...] = h_ref[...] + _rms(f, g_ref[...])


def _combine(y0, y1, gate, g, h):
    m = h.shape[0]
    row = lambda i: (i, 0)
    const = lambda i: (0, 0)
    act = pl.BlockSpec((TM, D_MODEL), row)
    return pl.pallas_call(
        _combine_kernel, grid=(m // TM,),
        in_specs=[act, act, pl.BlockSpec((TM, LANES), row), pl.BlockSpec(g.shape, const), act],
        out_specs=act,
        out_shape=jax.ShapeDtypeStruct((m, D_MODEL), F32),
        compiler_params=_params(("parallel",), 32), name="moe_combine",
    )(y0, y1, gate, g, h)


def _moe(h, gpre, wr, wg, wu, wd, gpost):
    n = h.shape[0]
    a, idx, gate, cnt = _router(h, gpre, wr)
    counts = cnt[0, :N_EXPERTS].astype(jnp.int32)
    padded = (counts + TM_MOE - 1) // TM_MOE * TM_MOE
    end_pad = jnp.cumsum(padded)
    start_pad = end_pad - padded
    dest = start_pad[idx[:, 0:2]] + idx[:, 2:4]
    n_slots = n * TOP_K + N_EXPERTS * TM_MOE
    tok = jnp.repeat(jnp.arange(n, dtype=jnp.int32), TOP_K)
    slot_tok = jnp.zeros((n_slots,), jnp.int32).at[dest.reshape(-1)].set(tok)
    tile_start = jnp.arange(n_slots // TM_MOE, dtype=jnp.int32) * TM_MOE
    tile_e = jnp.minimum(jnp.searchsorted(end_pad, tile_start, side='right'),
                         N_EXPERTS - 1).astype(jnp.int32)
    tile_valid = (tile_start < end_pad[-1]).astype(jnp.int32)
    xs = a[slot_tok]
    ys = _experts(tile_e, tile_valid, xs, wg, wu, wd)
    return _combine(ys[dest[:, 0]], ys[dest[:, 1]], gate, gpost, h)


def _rotary_tables(pos, dim, theta):
    exponent = jnp.arange(0, dim, 2, dtype=F32) / dim
    inv_freq = jnp.power(jnp.float32(theta), -exponent)
    ang = pos.astype(F32)[:, None] * inv_freq[None, :]
    return jnp.cos(ang), jnp.sin(ang)


def _tables(s):
    t = jnp.arange(s, dtype=jnp.int32)
    cr, sr = _rotary_tables(t // GRID_W, HEAD_DIM // 2, AXIAL_THETA)
    cc, sc = _rotary_tables(t % GRID_W, HEAD_DIM // 2, AXIAL_THETA)
    ca = jnp.tile(jnp.concatenate([cr, cr, cc, cc], axis=1), (1, 2))
    sa = jnp.tile(jnp.concatenate([-sr, sr, -sc, sc], axis=1), (1, 2))
    cm_, sm_ = _rotary_tables(t, MLA_ROPE_DIM, MLA_ROPE_THETA)
    one = jnp.ones((s, 1), F32)
    zero = jnp.zeros((s, 1), F32)
    cm = jnp.concatenate([jnp.tile(one, (1, 64)), cm_, cm_, jnp.tile(one, (1, 32))], axis=1)
    sm = jnp.concatenate([jnp.tile(zero, (1, 64)), -sm_, sm_, jnp.tile(zero, (1, 32))], axis=1)
    ckr = jnp.concatenate([cm_, cm_, jnp.tile(zero, (1, 96))], axis=1)
    skr = jnp.concatenate([-sm_, sm_, jnp.tile(zero, (1, 96))], axis=1)
    cp_, sp_ = _rotary_tables(t, PARTIAL_ROPE_DIM, ROPE_THETA)
    cp = jnp.tile(jnp.concatenate([cp_, cp_, jnp.tile(one, (1, 48))], axis=1), (1, 2))
    sp = jnp.tile(jnp.concatenate([-sp_, sp_, jnp.tile(zero, (1, 48))], axis=1), (1, 2))
    scale = HEAD_DIM ** -0.5 * LOG2_E
    ctab = jnp.stack([cp * scale, cp])
    stab = jnp.stack([sp * scale, sp])
    return ca, sa, cm, sm, ckr, skr, ctab, stab


def _col_tiles(w):
    e, d, f = w.shape
    return w.astype(BF16).reshape(e, d, f // TF, TF).transpose(0, 2, 1, 3)


def _row_tiles(w):
    e, f, d = w.shape
    return w.astype(BF16).reshape(e, f // TF, TF, d)


_GQA_PERM = (0, 4, 1, 5, 2, 6, 3, 7)


def _l0_weights(w_in, w_uq, w_ukv, w_out):
    d = w_in.shape[0]
    q = w_in[:, :512].reshape(d, 8, 64)[:, _GQA_PERM, :].reshape(d, 512)
    win = jnp.concatenate([q, w_in[:, 512:], jnp.zeros((d, AB_IN_PAD - w_in.shape[1]), w_in.dtype)],
                          axis=1).astype(BF16)
    uq = w_uq.reshape(MLA_Q_RANK, MLA_HEADS, MLA_NOPE_DIM + MLA_ROPE_DIM)
    wuq = jnp.pad(uq, ((0, 0), (0, 0), (0, LANES - uq.shape[2]))).reshape(MLA_Q_RANK, -1).astype(BF16)
    ukv = w_ukv.reshape(MLA_KV_RANK, MLA_HEADS, MLA_NOPE_DIM + MLA_V_DIM)
    wk = jnp.pad(ukv[:, :, :MLA_NOPE_DIM], ((0, 0), (0, 0), (0, LANES - MLA_NOPE_DIM)))
    wk = wk.reshape(MLA_KV_RANK, -1).astype(BF16)
    wv = ukv[:, :, MLA_NOPE_DIM:].reshape(MLA_KV_RANK, -1).astype(BF16)
    src = jnp.arange(LANES)[:, None]
    dst = jnp.arange(MLA_HEADS * LANES)[None, :]
    wp = ((src < MLA_ROPE_DIM) & (dst % LANES == src + MLA_NOPE_DIM)).astype(BF16)
    oa = w_out[:512].reshape(8, 64, -1)[_GQA_PERM, :, :].reshape(512, -1)
    wout = jnp.concatenate([oa, w_out[512:]], axis=0).astype(BF16)
    return win, wuq, wk, wp, wv, wout


def kernel(x, p, mix_pre_g, mix_post_g, ffn_pre_g, ffn_post_g, ple_w_proj, ple_gate_norm_g, ple_w_gate, ple_post_g, ab_w_in, gqa_q_norm_g, gqa_k_norm_g, mla_q_norm_g, mla_w_uq, mla_kv_norm_g, mla_w_ukv, ab_w_out, ffn_w_gate, ffn_w_up, ffn_w_down, dil_w_qkv, dil_w_out, moe_w_router, moe_w_gate, moe_w_up, moe_w_down):
    b, s, d = x.shape
    m = b * s
    row = lambda g: g.reshape(1, -1).astype(F32)
    ca, sa, cm, sm, ckr, skr, ctab, stab = _tables(s)
    h = x.reshape(m, d)
    pf = p.reshape(p.shape[0], m, PLE_DIM)

    win, wuq, wk, wp, wv, wout = _l0_weights(ab_w_in[0], mla_w_uq[0], mla_w_ukv[0], ab_w_out[0])
    gq = jnp.tile(row(gqa_q_norm_g[0]), (1, 2))
    gk = jnp.tile(row(gqa_k_norm_g[0]), (1, 2))
    qa, ka, vat, qm, km, vmt = _l0_in(h, row(mix_pre_g[0]), win, gq, gk, ca, sa,
                                      row(mla_q_norm_g[0]), wuq, cm, sm,
                                      row(mla_kv_norm_g[0]), wk, wp, wv, ckr, skr, b, s)
    oa = _gqa_attn(qa, ka, vat, b, s)
    om = _mla_attn(qm, km, vmt, b, s)
    h = _l0_out(oa, om, wout, row(mix_post_g[0]), h)
    h = _ffn(h, row(ffn_pre_g[0]), ffn_w_gate[0].astype(BF16), ffn_w_up[0].astype(BF16),
             ffn_w_down[0].astype(BF16), row(ffn_post_g[0]))
    h = _ple(h, pf[0], row(ple_gate_norm_g[0]), ple_w_gate[0].astype(BF16),
             ple_w_proj[0].astype(BF16), row(ple_post_g[0]))

    qkv = _l1_qkv(h, row(mix_pre_g[1]), dil_w_qkv[0].astype(BF16), ctab, stab, s)
    o = _dil_attn(qkv, b, s)
    h = _l1_out(o, dil_w_out[0].astype(BF16), row(mix_post_g[1]), h)
    wr = jnp.pad(moe_w_router[0].astype(F32), ((0, 0), (0, LANES - N_EXPERTS)))
    h = _moe(h, row(ffn_pre_g[1]), wr, _col_tiles(moe_w_gate[0]), _col_tiles(moe_w_up[0]),
             _row_tiles(moe_w_down[0]), row(ffn_post_g[1]))
    h = _ple(h, pf[1], row(ple_gate_norm_g[1]), ple_w_gate[1].astype(BF16),
             ple_w_proj[1].astype(BF16), row(ple_post_g[1]))
    return h.reshape(b, s, d)
```

```python
import functools

import jax
import jax.numpy as jnp
from jax import lax
from jax.experimental import pallas as pl
from jax.experimental.pallas import tpu as pltpu

F32 = jnp.float32
BF16 = jnp.bfloat16

D_MODEL = 1024
GRID_W = 64
HEAD_DIM = 64
NORM_EPS = 1e-6
NEG_INF = -1e30
ROPE_THETA = 500000.0
PARTIAL_ROPE_DIM = HEAD_DIM // 4
GQA_Q_HEADS = 8
GQA_KV_HEADS = 2
AXIAL_THETA = 10000.0
MLA_HEADS = 8
MLA_Q_RANK = 256
MLA_KV_RANK = 128
MLA_NOPE_DIM = 64
MLA_ROPE_DIM = 32
MLA_V_DIM = 64
MLA_ROPE_THETA = 10000.0
DIL_CONFIGS = ((128, 1), (512, 4), (2048, 16))
DIL_GROUPS = len(DIL_CONFIGS)
DIL_HEADS = 8
DIL_HALF = 64
DIL_QB = 128
LOG2_E = 1.4426950408889634
DIL_IN_WIDTH = DIL_GROUPS * 3 * DIL_HEADS * HEAD_DIM
D_FF = 3584
N_EXPERTS = 8
TOP_K = 2
PLE_DIM = 256

LANES = 128
VMEM_BYTES = 64 * 1024 * 1024
MIB = 1024 * 1024

TM = 512
TQ = 512
KEY_CHUNK = 512
V_ONES = 16
V_ROWS = LANES + V_ONES
TF = 512
TM_FFN = 1024
TM_MOE = 1024
AB_IN_PAD = 1280


def _params(semantics, vmem_mib):
    return pltpu.CompilerParams(dimension_semantics=semantics,
                                vmem_limit_bytes=vmem_mib * MIB)


def _dot(a, b):
    return jnp.dot(a, b, preferred_element_type=F32)


def _rms(x, g):
    return x * lax.rsqrt(jnp.mean(x * x, axis=-1, keepdims=True) + NORM_EPS) * g


def _lane_iota(shape):
    return lax.broadcasted_iota(jnp.int32, shape, len(shape) - 1)


def _swap_halves(x, k):
    w = x.shape[-1]
    fwd = pltpu.roll(x, w - k, 1)
    bwd = pltpu.roll(x, k, 1)
    return jnp.where((_lane_iota(x.shape) % (2 * k)) < k, fwd, bwd)


def _rotate(x, c, s, k):
    return x * c + _swap_halves(x, k) * s


def _head_rms(x, g):
    lo = _lane_iota(x.shape) < HEAD_DIM
    x2 = x * x
    s_all = jnp.sum(x2, axis=-1, keepdims=True)
    s_lo = jnp.sum(jnp.where(lo, x2, 0.0), axis=-1, keepdims=True)
    ms = jnp.where(lo, s_lo, s_all - s_lo) * (1.0 / HEAD_DIM)
    return x * lax.rsqrt(ms + NORM_EPS) * g


def _l0_in_kernel(h_ref, gpre_ref, win_ref, gq_ref, gk_ref, ca_ref, sa_ref,
                  gmq_ref, wuq_ref, cm_ref, sm_ref, gmkv_ref, wk_ref, wp_ref, wv_ref,
                  ckr_ref, skr_ref,
                  qa_ref, ka_ref, vat_ref, qm_ref, km_ref, vmt_ref):
    a = _rms(h_ref[...], gpre_ref[...]).astype(BF16)
    z = _dot(a, win_ref[...])
    ca, sa = ca_ref[...], sa_ref[...]
    for j in range(4):
        xq = _head_rms(z[:, j * LANES:(j + 1) * LANES], gq_ref[...])
        qa_ref[:, j * LANES:(j + 1) * LANES] = (
            _rotate(xq, ca, sa, 16) * (HEAD_DIM ** -0.5 * LOG2_E)).astype(BF16)
    ka_ref[...] = _rotate(_head_rms(z[:, 512:640], gk_ref[...]), ca, sa, 16).astype(BF16)
    ones = jnp.ones((V_ONES, z.shape[0]), BF16)
    vat_ref[0:LANES, :] = z[:, 640:768].T.astype(BF16)
    vat_ref[LANES:, :] = ones
    cq = _rms(z[:, 768:1024], gmq_ref[...]).astype(BF16)
    qm = _dot(cq, wuq_ref[...])
    cm, sm = cm_ref[...], sm_ref[...]
    scale_m = (MLA_NOPE_DIM + MLA_ROPE_DIM) ** -0.5 * LOG2_E
    for hb in range(MLA_HEADS):
        x = qm[:, hb * LANES:(hb + 1) * LANES]
        qm_ref[:, hb * LANES:(hb + 1) * LANES] = (_rotate(x, cm, sm, 16) * scale_m).astype(BF16)
    ckv = _rms(z[:, 1024:1152], gmkv_ref[...]).astype(BF16)
    kr = _rotate(z[:, 1152:1280], ckr_ref[...], skr_ref[...], 16).astype(BF16)
    km = _dot(ckv, wk_ref[...]) + _dot(kr, wp_ref[...])
    km_ref[...] = km.astype(BF16)
    vm = _dot(ckv, wv_ref[...])
    for j in range(4):
        vmt_ref[j * V_ROWS:j * V_ROWS + LANES, :] = vm[:, j * LANES:(j + 1) * LANES].T.astype(BF16)
        vmt_ref[j * V_ROWS + LANES:(j + 1) * V_ROWS, :] = ones


def _l0_in(h, gpre, win, gq, gk, ca, sa, gmq, wuq, cm, sm, gmkv, wk, wp, wv, ckr, skr, b, s):
    m = h.shape[0]
    nt = s // TM
    row = lambda i: (i, 0)
    const = lambda i: (0, 0)
    tab = lambda i: (i % nt, 0)
    full = lambda a: pl.BlockSpec(a.shape, const)
    in_specs = [pl.BlockSpec((TM, D_MODEL), row), full(gpre), full(win), full(gq), full(gk),
                pl.BlockSpec((TM, LANES), tab), pl.BlockSpec((TM, LANES), tab),
                full(gmq), full(wuq), pl.BlockSpec((TM, LANES), tab), pl.BlockSpec((TM, LANES), tab),
                full(gmkv), full(wk), full(wp), full(wv),
                pl.BlockSpec((TM, LANES), tab), pl.BlockSpec((TM, LANES), tab)]
    tr = lambda i: (i // nt, 0, i % nt)
    out_specs = [pl.BlockSpec((TM, 512), row),
                 pl.BlockSpec((TM, LANES), row),
                 pl.BlockSpec((None, V_ROWS, TM), tr),
                 pl.BlockSpec((TM, 1024), row),
                 pl.BlockSpec((TM, 1024), row),
                 pl.BlockSpec((None, 4 * V_ROWS, TM), tr)]
    out_shape = [jax.ShapeDtypeStruct((m, 512), BF16),
                 jax.ShapeDtypeStruct((m, LANES), BF16),
                 jax.ShapeDtypeStruct((b, V_ROWS, s), BF16),
                 jax.ShapeDtypeStruct((m, 1024), BF16),
                 jax.ShapeDtypeStruct((m, 1024), BF16),
                 jax.ShapeDtypeStruct((b, 4 * V_ROWS, s), BF16)]
    return pl.pallas_call(
        _l0_in_kernel, grid=(m // TM,), in_specs=in_specs, out_specs=out_specs,
        out_shape=out_shape, compiler_params=_params(("parallel",), 48), name="l0_in",
    )(h, gpre, win, gq, gk, ca, sa, gmq, wuq, cm, sm, gmkv, wk, wp, wv, ckr, skr)


def _attend_pair(score_t, vt_ref, o_ref):
    n_chunks = vt_ref.shape[1] // KEY_CHUNK
    tq = o_ref.shape[0]
    keys = [slice(c * KEY_CHUNK, (c + 1) * KEY_CHUNK) for c in range(n_chunks)]
    parts = []
    s_next = score_t(keys[0])
    for c in range(n_chunks):
        s_t = s_next
        if c + 1 < n_chunks:
            s_next = score_t(keys[c + 1])
        m = jnp.max(s_t, axis=0, keepdims=True)
        parts.append((m, _dot(vt_ref[:, keys[c]], jnp.exp2(s_t - m).astype(BF16))))
    m_all = functools.reduce(jnp.maximum, [m for m, _ in parts])
    o_t = sum(o_c * jnp.exp2(m_c - m_all) for m_c, o_c in parts)
    o_t = o_t[0:LANES, :] / o_t[LANES:LANES + 1, :]
    first = lax.broadcasted_iota(jnp.int32, (LANES, tq), 0) < HEAD_DIM
    o_ref[...] = jnp.where(first, o_t[:, 0:tq], o_t[:, tq:]).T.astype(BF16)


def _transposed(q):
    return q.astype(F32).T.astype(BF16)


def _gqa_attn_kernel(q_ref, k_ref, vt_ref, o_ref):
    qt = _transposed(q_ref[...])
    first = lax.broadcasted_iota(jnp.int32, qt.shape, 0) < HEAD_DIM
    zero = jnp.zeros_like(qt)
    w = jnp.concatenate([jnp.where(first, qt, zero), jnp.where(first, zero, qt)], axis=1)
    _attend_pair(lambda keys: _dot(k_ref[keys, :], w), vt_ref, o_ref)


def _gqa_attn(qa, ka, vat, b, s):
    m = qa.shape[0]
    nq = s // TQ
    return pl.pallas_call(
        _gqa_attn_kernel, grid=(b, nq, 4),
        in_specs=[pl.BlockSpec((TQ, LANES), lambda bi, i, j: (bi * nq + i, j)),
                  pl.BlockSpec((s, LANES), lambda bi, i, j: (bi, 0)),
                  pl.BlockSpec((None, V_ROWS, s), lambda bi, i, j: (bi, 0, 0))],
        out_specs=pl.BlockSpec((TQ, LANES), lambda bi, i, j: (bi * nq + i, j)),
        out_shape=jax.ShapeDtypeStruct((m, 512), BF16),
        compiler_params=_params(("parallel", "parallel", "parallel"), 32), name="gqa_attn",
    )(qa, ka, vat)


def _mla_attn_kernel(q_ref, k_ref, vt_ref, o_ref):
    qt0 = _transposed(q_ref[:, 0:LANES])
    qt1 = _transposed(q_ref[:, LANES:2 * LANES])

    def score_t(keys):
        return jnp.concatenate([_dot(k_ref[keys, 0:LANES], qt0),
                                _dot(k_ref[keys, LANES:2 * LANES], qt1)], axis=1)

    _attend_pair(score_t, vt_ref, o_ref)


def _mla_attn(qm, km, vmt, b, s):
    m = qm.shape[0]
    nq = s // TQ
    return pl.pallas_call(
        _mla_attn_kernel, grid=(b, nq, 4),
        in_specs=[pl.BlockSpec((TQ, 2 * LANES), lambda bi, i, j: (bi * nq + i, j)),
                  pl.BlockSpec((s, 2 * LANES), lambda bi, i, j: (bi, j)),
                  pl.BlockSpec((None, V_ROWS, s), lambda bi, i, j: (bi, j, 0))],
        out_specs=pl.BlockSpec((TQ, LANES), lambda bi, i, j: (bi * nq + i, j)),
        out_shape=jax.ShapeDtypeStruct((m, 512), BF16),
        compiler_params=_params(("parallel", "parallel", "parallel"), 32), name="mla_attn",
    )(qm, km, vmt)


def _silu(x):
    return x * (1.0 / (1.0 + jnp.exp(-x)))


def _ple_update(h, p, ggn, wg, wp, gpost):
    gate = _dot(_rms(h, ggn).astype(BF16), wg)
    gate = 1.0 / (1.0 + jnp.exp(-gate))
    e = _dot(p.astype(BF16), wp) * gate
    return h + _rms(e, gpost)


def _l0_tail_kernel(oa_ref, om_ref, wo_ref, gmix_ref, h_ref, gpre_ref, wg_ref, wu_ref, wd_ref,
                    gpost_ref, p_ref, ggn_ref, wpg_ref, wpp_ref, gple_ref, out_ref, a_ref, acc_ref):
    j = pl.program_id(1)

    @pl.when(j == 0)
    def _():
        k = oa_ref.shape[1]
        y = _dot(oa_ref[...], wo_ref[0:k, :]) + _dot(om_ref[...], wo_ref[k:, :])
        h1 = h_ref[...] + _rms(y, gmix_ref[...])
        out_ref[...] = h1
        a_ref[...] = _rms(h1, gpre_ref[...]).astype(BF16)
        acc_ref[...] = jnp.zeros_like(acc_ref)

    a = a_ref[...]
    mid = (_silu(_dot(a, wg_ref[...])) * _dot(a, wu_ref[...])).astype(BF16)
    acc_ref[...] += _dot(mid, wd_ref[...])

    @pl.when(j == pl.num_programs(1) - 1)
    def _():
        h2 = out_ref[...] + _rms(acc_ref[...], gpost_ref[...])
        out_ref[...] = _ple_update(h2, p_ref[...], ggn_ref[...], wpg_ref[...], wpp_ref[...],
                                   gple_ref[...])


def _l0_tail(oa, om, wo, gmix, h, gpre, wg, wu, wd, gpost, p, ggn, wpg, wpp, gple):
    m = h.shape[0]
    dff = wg.shape[1]
    row = lambda i, j: (i, 0)
    const = lambda i, j: (0, 0)
    vec = lambda g: pl.BlockSpec(g.shape, const)
    once = lambda w: pl.BlockSpec(w.shape, const, pipeline_mode=pl.Buffered(1))
    return pl.pallas_call(
        _l0_tail_kernel, grid=(m // TM_FFN, dff // TF),
        in_specs=[pl.BlockSpec((TM_FFN, oa.shape[1]), row), pl.BlockSpec((TM_FFN, om.shape[1]), row),
                  once(wo), vec(gmix), pl.BlockSpec((TM_FFN, D_MODEL), row), vec(gpre),
                  pl.BlockSpec((D_MODEL, TF), lambda i, j: (0, j)),
                  pl.BlockSpec((D_MODEL, TF), lambda i, j: (0, j)),
                  pl.BlockSpec((TF, D_MODEL), lambda i, j: (j, 0)),
                  vec(gpost), pl.BlockSpec((TM_FFN, PLE_DIM), row), vec(ggn), once(wpg), once(wpp),
                  vec(gple)],
        out_specs=pl.BlockSpec((TM_FFN, D_MODEL), row),
        out_shape=jax.ShapeDtypeStruct((m, D_MODEL), F32),
        scratch_shapes=[pltpu.VMEM((TM_FFN, D_MODEL), BF16), pltpu.VMEM((TM_FFN, D_MODEL), F32)],
        compiler_params=_params(("parallel", "arbitrary"), 56), name="l0_tail",
    )(oa, om, wo, gmix, h, gpre, wg, wu, wd, gpost, p, ggn, wpg, wpp, gple)


def _l1_qkv_kernel(h_ref, gpre_ref, w_ref, c_ref, s_ref, out_ref):
    a = _rms(h_ref[...], gpre_ref[...]).astype(BF16)
    for blk in range(DIL_IN_WIDTH // 512):
        cols = slice(blk * 512, (blk + 1) * 512)
        z = _dot(a, w_ref[:, cols])
        kind = blk % 3
        if kind == 2:
            out_ref[:, cols] = z.astype(BF16)
            continue
        c, s = c_ref[kind], s_ref[kind]
        for sub in range(4):
            x = z[:, sub * LANES:(sub + 1) * LANES]
            out_ref[:, blk * 512 + sub * LANES:blk * 512 + (sub + 1) * LANES] = (
                _rotate(x, c, s, PARTIAL_ROPE_DIM // 2).astype(BF16))


def _l1_qkv(h, gpre, w, ctab, stab, s):
    m = h.shape[0]
    nt = s // TM
    n = w.shape[1]
    const = lambda i: (0, 0)
    return pl.pallas_call(
        _l1_qkv_kernel, grid=(m // TM,),
        in_specs=[pl.BlockSpec((TM, D_MODEL), lambda i: (i, 0)),
                  pl.BlockSpec(gpre.shape, const),
                  pl.BlockSpec(w.shape, const, pipeline_mode=pl.Buffered(1)),
                  pl.BlockSpec((2, TM, LANES), lambda i: (0, i % nt, 0)),
                  pl.BlockSpec((2, TM, LANES), lambda i: (0, i % nt, 0))],
        out_specs=pl.BlockSpec((TM, n), lambda i: (i, 0)),
        out_shape=jax.ShapeDtypeStruct((m, n), BF16),
        compiler_params=_params(("parallel",), 48), name="l1_qkv",
    )(h, gpre, w, ctab, stab)


def _dil_attn_kernel(q0_ref, k0_ref, v0_ref, q1_ref, k1_ref, v1_ref, q2_ref, k2_ref, v2_ref, o_ref,
                     stage_ref, qd_ref, kd_ref, vd_ref, ra_ref, rl_ref, rm_ref,
                     acc_ref, sum_ref, max_ref):
    seq = o_ref.shape[0]
    lo = _lane_iota((DIL_QB, LANES)) < HEAD_DIM
    vd_ref[:, LANES:2 * LANES] = jnp.ones((seq, LANES), BF16)
    groups = ((q0_ref, k0_ref, v0_ref), (q1_ref, k1_ref, v1_ref), (q2_ref, k2_ref, v2_ref))
    for gi, (q_ref, k_ref, v_ref) in enumerate(groups):
        dil = DIL_CONFIGS[gi][1]
        length = seq // dil
        width = min(DIL_QB + 2 * DIL_HALF, length)
        nblk = length // DIL_QB
        for src, dst, c1 in ((q_ref, qd_ref, LANES), (k_ref, kd_ref, LANES), (v_ref, vd_ref, LANES)):
            if dil == 1:
                dst[:, 0:c1] = src[...]
            else:
                stage_ref[...] = src[...].astype(F32)
                for r in range(dil):
                    dst[r * length:(r + 1) * length, 0:c1] = (
                        stage_ref[pl.ds(r, length, stride=dil), :].astype(BF16))
        first = gi == 0
        ta, tl, tm_ = (acc_ref, sum_ref, max_ref) if first else (ra_ref, rl_ref, rm_ref)
        row = lax.broadcasted_iota(jnp.int32, (2 * DIL_QB, width), 0) % DIL_QB
        col = lax.broadcasted_iota(jnp.int32, (2 * DIL_QB, width), 1)
        rel = row - col

        def body(it, carry, length=length, width=width, nblk=nblk, ta=ta, tl=tl, tm_=tm_, rel=rel):
            r = it // nblk
            i = it % nblk
            q0 = i * DIL_QB
            start = jnp.clip(q0 - DIL_HALF, 0, length - width)
            qrow = pl.multiple_of(r * length + q0, DIL_HALF)
            krow = pl.multiple_of(r * length + start, DIL_HALF)
            q = qd_ref[pl.ds(qrow, DIL_QB), :]
            kw = kd_ref[pl.ds(krow, width), :]
            vw = vd_ref[pl.ds(krow, width), :]
            zero = jnp.zeros_like(q)
            q2 = jnp.concatenate([jnp.where(lo, q, zero), jnp.where(lo, zero, q)], axis=0)
            sc = lax.dot_general(q2, kw, (((1,), (1,)), ((), ())), preferred_element_type=F32)
            sc = jnp.where(jnp.abs(rel + (q0 - start)) <= DIL_HALF, sc, NEG_INF)
            m = jnp.max(sc, axis=-1, keepdims=True)
            p = jnp.exp2(sc - m).astype(BF16)
            o2 = _dot(p, vw)
            ta[pl.ds(qrow, DIL_QB), :] = jnp.where(lo, o2[:DIL_QB, 0:LANES], o2[DIL_QB:, 0:LANES])
            tl[pl.ds(qrow, DIL_QB), :] = jnp.where(lo, o2[:DIL_QB, LANES:], o2[DIL_QB:, LANES:])
            tm_[pl.ds(qrow, DIL_QB), :] = jnp.where(lo, m[:DIL_QB], m[DIL_QB:])
            return carry

        lax.fori_loop(0, dil * nblk, body, 0, unroll=4)
        if first:
            continue
        for r in range(dil):
            rows = pl.ds(r, length, stride=dil)
            blk = slice(r * length, (r + 1) * length)
            m_old, m_blk = max_ref[rows, :], rm_ref[blk, :]
            m_new = jnp.maximum(m_old, m_blk)
            w_old, w_blk = jnp.exp2(m_old - m_new), jnp.exp2(m_blk - m_new)
            acc_ref[rows, :] = acc_ref[rows, :] * w_old + ra_ref[blk, :] * w_blk
            sum_ref[rows, :] = sum_ref[rows, :] * w_old + rl_ref[blk, :] * w_blk
            max_ref[rows, :] = m_new
    o_ref[...] = (acc_ref[...] / sum_ref[...]).astype(BF16)


def _dil_attn(qkv, b, s):
    m = qkv.shape[0]

    def spec(gi, j):
        return pl.BlockSpec((s, LANES), lambda bi, hp: (bi, gi * 12 + 4 * j + hp))

    stat = pltpu.VMEM((s, LANES), F32)
    return pl.pallas_call(
        _dil_attn_kernel, grid=(b, 4),
        in_specs=[spec(gi, j) for gi in range(DIL_GROUPS) for j in range(3)],
        out_specs=pl.BlockSpec((s, LANES), lambda bi, hp: (bi, hp)),
        out_shape=jax.ShapeDtypeStruct((m, 512), BF16),
        scratch_shapes=[stat, pltpu.VMEM((s, LANES), BF16), pltpu.VMEM((s, LANES), BF16),
                        pltpu.VMEM((s, 2 * LANES), BF16), stat, stat, stat, stat, stat, stat],
        compiler_params=_params(("parallel", "parallel"), 40), name="dil_attn",
    )(*([qkv] * 9))


def _l1_router_kernel(o_ref, wo_ref, gmix_ref, h_ref, gpre_ref, wr_ref,
                      h1_ref, a_ref, idx_ref, gate_ref, cnt_ref, carry_ref):
    @pl.when(pl.program_id(0) == 0)
    def _():
        carry_ref[...] = jnp.zeros_like(carry_ref)

    h1 = h_ref[...] + _rms(_dot(o_ref[...], wo_ref[...]), gmix_ref[...])
    h1_ref[...] = h1
    a = _rms(h1, gpre_ref[...]).astype(BF16)
    a_ref[...] = a
    logits = _dot(a, wr_ref[...])
    tm = logits.shape[0]
    lane = _lane_iota(logits.shape)
    lanef = lane.astype(F32)
    lg = jnp.where(lane < N_EXPERTS, logits, -jnp.inf)
    m1 = jnp.max(lg, axis=-1, keepdims=True)
    i1 = jnp.min(jnp.where(lg == m1, lanef, float(LANES)), axis=-1, keepdims=True)
    lg2 = jnp.where(lanef == i1, -jnp.inf, lg)
    m2 = jnp.max(lg2, axis=-1, keepdims=True)
    i2 = jnp.min(jnp.where(lg2 == m2, lanef, float(LANES)), axis=-1, keepdims=True)
    t = jnp.exp(m2 - m1)
    g1 = 1.0 / (1.0 + t)
    g2 = t / (1.0 + t)
    hit1 = lanef == i1
    hit2 = lanef == i2
    onehot = jnp.where(hit1 | hit2, 1.0, 0.0)
    r = lax.broadcasted_iota(jnp.int32, (tm, tm), 0)
    c = lax.broadcasted_iota(jnp.int32, (tm, tm), 1)
    tri = jnp.where(c < r, 1.0, 0.0).astype(BF16)
    before = _dot(tri, onehot.astype(BF16)) + carry_ref[...]
    rank1 = jnp.sum(jnp.where(hit1, before, 0.0), axis=-1, keepdims=True)
    rank2 = jnp.sum(jnp.where(hit2, before, 0.0), axis=-1, keepdims=True)
    info = jnp.where(lane == 0, i1, jnp.where(lane == 1, i2, jnp.where(lane == 2, rank1, rank2)))
    idx_ref[...] = info.astype(jnp.int32)
    gate_ref[...] = jnp.where(lane == 0, g1, g2)
    carry_ref[...] += jnp.sum(onehot, axis=0, keepdims=True)
    cnt_ref[...] = carry_ref[...]


def _l1_router(o, wo, gmix, h, gpre, wr):
    m = h.shape[0]
    row = lambda i: (i, 0)
    const = lambda i: (0, 0)
    full = lambda x: pl.BlockSpec(x.shape, const)
    return pl.pallas_call(
        _l1_router_kernel, grid=(m // TM,),
        in_specs=[pl.BlockSpec((TM, o.shape[1]), row), full(wo), full(gmix),
                  pl.BlockSpec((TM, D_MODEL), row), full(gpre), full(wr)],
        out_specs=[pl.BlockSpec((TM, D_MODEL), row), pl.BlockSpec((TM, D_MODEL), row),
                   pl.BlockSpec((TM, LANES), row), pl.BlockSpec((TM, LANES), row),
                   pl.BlockSpec((1, LANES), const)],
        out_shape=[jax.ShapeDtypeStruct((m, D_MODEL), F32),
                   jax.ShapeDtypeStruct((m, D_MODEL), BF16),
                   jax.ShapeDtypeStruct((m, LANES), jnp.int32),
                   jax.ShapeDtypeStruct((m, LANES), F32),
                   jax.ShapeDtypeStruct((1, LANES), F32)],
        scratch_shapes=[pltpu.VMEM((1, LANES), F32)],
        compiler_params=_params(("arbitrary",), 32), name="l1_router",
    )(o, wo, gmix, h, gpre, wr)


def _expert_kernel(te_ref, tv_ref, x_ref, wg_ref, wu_ref, wd_ref, y_ref, acc_ref):
    i = pl.program_id(0)
    j = pl.program_id(1)
    last = pl.num_programs(1) - 1
    valid = tv_ref[i] > 0

    @pl.when(valid & (j == 0))
    def _():
        acc_ref[...] = jnp.zeros_like(acc_ref)

    @pl.when(valid)
    def _():
        x = x_ref[...]
        gate = _dot(x, wg_ref[...].astype(BF16))
        up = _dot(x, wu_ref[...].astype(BF16))
        acc_ref[...] += _dot((_silu(gate) * up).astype(BF16), wd_ref[...].astype(BF16))

    @pl.when(valid & (j == last))
    def _():
        y_ref[...] = acc_ref[...].astype(BF16)

    @pl.when(jnp.logical_not(valid) & (j == last))
    def _():
        y_ref[...] = jnp.zeros_like(y_ref)


def _experts(tile_e, tile_valid, xs, wg, wu, wd):
    n_slots = xs.shape[0]
    dff = wg.shape[2]
    grid_spec = pltpu.PrefetchScalarGridSpec(
        num_scalar_prefetch=2, grid=(n_slots // TM_MOE, dff // TF),
        in_specs=[pl.BlockSpec((TM_MOE, D_MODEL), lambda i, j, te, tv: (i, 0)),
                  pl.BlockSpec((None, D_MODEL, TF), lambda i, j, te, tv: (te[i], 0, j * tv[i])),
                  pl.BlockSpec((None, D_MODEL, TF), lambda i, j, te, tv: (te[i], 0, j * tv[i])),
                  pl.BlockSpec((None, TF, D_MODEL), lambda i, j, te, tv: (te[i], j * tv[i], 0))],
        out_specs=pl.BlockSpec((TM_MOE, D_MODEL), lambda i, j, te, tv: (i, 0)),
        scratch_shapes=[pltpu.VMEM((TM_MOE, D_MODEL), F32)])
    return pl.pallas_call(
        _expert_kernel, grid_spec=grid_spec,
        out_shape=jax.ShapeDtypeStruct((n_slots, D_MODEL), BF16),
        compiler_params=_params(("parallel", "arbitrary"), 48), name="moe_experts",
    )(tile_e, tile_valid, xs, wg, wu, wd)


def _combine_ple_kernel(y0_ref, y1_ref, gate_ref, g_ref, h_ref, p_ref, ggn_ref, wpg_ref, wpp_ref,
                        gple_ref, out_ref):
    gate = gate_ref[...]
    f = y0_ref[...].astype(F32) * gate[:, 0:1] + y1_ref[...].astype(F32) * gate[:, 1:2]
    h2 = h_ref[...] + _rms(f, g_ref[...])
    out_ref[...] = _ple_update(h2, p_ref[...], ggn_ref[...], wpg_ref[...], wpp_ref[...], gple_ref[...])


def _combine_ple(y0, y1, gate, g, h, p, ggn, wpg, wpp, gple):
    m = h.shape[0]
    row = lambda i: (i, 0)
    const = lambda i: (0, 0)
    full = lambda x: pl.BlockSpec(x.shape, const)
    act = pl.BlockSpec((TM, D_MODEL), row)
    return pl.pallas_call(
        _combine_ple_kernel, grid=(m // TM,),
        in_specs=[act, act, pl.BlockSpec((TM, LANES), row), full(g), act,
                  pl.BlockSpec((TM, PLE_DIM), row), full(ggn), full(wpg), full(wpp), full(gple)],
        out_specs=act,
        out_shape=jax.ShapeDtypeStruct((m, D_MODEL), F32),
        compiler_params=_params(("parallel",), 40), name="combine_ple",
    )(y0, y1, gate, g, h, p, ggn, wpg, wpp, gple)


def _moe_plan(idx, cnt, n):
    counts = cnt[0, :N_EXPERTS].astype(jnp.int32)
    padded = (counts + TM_MOE - 1) // TM_MOE * TM_MOE
    end_pad = jnp.cumsum(padded)
    start_pad = end_pad - padded
    experts = jnp.arange(N_EXPERTS, dtype=jnp.int32)
    start = jnp.sum(jnp.where(idx[:, 0:2, None] == experts, start_pad, 0), axis=-1)
    dest = start + idx[:, 2:4]
    n_slots = n * TOP_K + N_EXPERTS * TM_MOE
    tok = jnp.repeat(jnp.arange(n, dtype=jnp.int32), TOP_K)
    slot_tok = jnp.zeros((n_slots,), jnp.int32).at[dest.reshape(-1)].set(tok, unique_indices=True)
    tile_start = jnp.arange(n_slots // TM_MOE, dtype=jnp.int32) * TM_MOE
    tile_e = jnp.minimum(jnp.sum(tile_start[:, None] >= end_pad[None, :], axis=1),
                         N_EXPERTS - 1).astype(jnp.int32)
    tile_valid = (tile_start < end_pad[-1]).astype(jnp.int32)
    return dest, slot_tok, tile_e, tile_valid


def _rotary_tables(pos, dim, theta):
    exponent = jnp.arange(0, dim, 2, dtype=F32) / dim
    inv_freq = jnp.power(jnp.float32(theta), -exponent)
    ang = pos.astype(F32)[:, None] * inv_freq[None, :]
    return jnp.cos(ang), jnp.sin(ang)


def _tables(s):
    t = jnp.arange(s, dtype=jnp.int32)
    cr, sr = _rotary_tables(t // GRID_W, HEAD_DIM // 2, AXIAL_THETA)
    cc, sc = _rotary_tables(t % GRID_W, HEAD_DIM // 2, AXIAL_THETA)
    ca = jnp.tile(jnp.concatenate([cr, cr, cc, cc], axis=1), (1, 2))
    sa = jnp.tile(jnp.concatenate([-sr, sr, -sc, sc], axis=1), (1, 2))
    cm_, sm_ = _rotary_tables(t, MLA_ROPE_DIM, MLA_ROPE_THETA)
    one = jnp.ones((s, 1), F32)
    zero = jnp.zeros((s, 1), F32)
    cm = jnp.concatenate([jnp.tile(one, (1, 64)), cm_, cm_, jnp.tile(one, (1, 32))], axis=1)
    sm = jnp.concatenate([jnp.tile(zero, (1, 64)), -sm_, sm_, jnp.tile(zero, (1, 32))], axis=1)
    ckr = jnp.concatenate([cm_, cm_, jnp.tile(zero, (1, 96))], axis=1)
    skr = jnp.concatenate([-sm_, sm_, jnp.tile(zero, (1, 96))], axis=1)
    cp_, sp_ = _rotary_tables(t, PARTIAL_ROPE_DIM, ROPE_THETA)
    cp = jnp.tile(jnp.concatenate([cp_, cp_, jnp.tile(one, (1, 48))], axis=1), (1, 2))
    sp = jnp.tile(jnp.concatenate([-sp_, sp_, jnp.tile(zero, (1, 48))], axis=1), (1, 2))
    scale = HEAD_DIM ** -0.5 * LOG2_E
    ctab = jnp.stack([cp * scale, cp])
    stab = jnp.stack([sp * scale, sp])
    return ca, sa, cm, sm, ckr, skr, ctab, stab


_GQA_PERM = (0, 4, 1, 5, 2, 6, 3, 7)


def _l0_weights(w_in, w_uq, w_ukv, w_out):
    d = w_in.shape[0]
    q = w_in[:, :512].reshape(d, 8, 64)[:, _GQA_PERM, :].reshape(d, 512)
    win = jnp.concatenate([q, w_in[:, 512:], jnp.zeros((d, AB_IN_PAD - w_in.shape[1]), w_in.dtype)],
                          axis=1).astype(BF16)
    uq = w_uq.reshape(MLA_Q_RANK, MLA_HEADS, MLA_NOPE_DIM + MLA_ROPE_DIM)
    wuq = jnp.pad(uq, ((0, 0), (0, 0), (0, LANES - uq.shape[2]))).reshape(MLA_Q_RANK, -1).astype(BF16)
    ukv = w_ukv.reshape(MLA_KV_RANK, MLA_HEADS, MLA_NOPE_DIM + MLA_V_DIM)
    wk = jnp.pad(ukv[:, :, :MLA_NOPE_DIM], ((0, 0), (0, 0), (0, LANES - MLA_NOPE_DIM)))
    wk = wk.reshape(MLA_KV_RANK, -1).astype(BF16)
    wv = ukv[:, :, MLA_NOPE_DIM:].reshape(MLA_KV_RANK, -1).astype(BF16)
    src = jnp.arange(LANES)[:, None]
    dst = jnp.arange(MLA_HEADS * LANES)[None, :]
    wp = ((src < MLA_ROPE_DIM) & (dst % LANES == src + MLA_NOPE_DIM)).astype(BF16)
    oa = w_out[:512].reshape(8, 64, -1)[_GQA_PERM, :, :].reshape(512, -1)
    wout = jnp.concatenate([oa, w_out[512:]], axis=0).astype(BF16)
    return win, wuq, wk, wp, wv, wout


def kernel(x, p, mix_pre_g, mix_post_g, ffn_pre_g, ffn_post_g, ple_w_proj, ple_gate_norm_g, ple_w_gate, ple_post_g, ab_w_in, gqa_q_norm_g, gqa_k_norm_g, mla_q_norm_g, mla_w_uq, mla_kv_norm_g, mla_w_ukv, ab_w_out, ffn_w_gate, ffn_w_up, ffn_w_down, dil_w_qkv, dil_w_out, moe_w_router, moe_w_gate, moe_w_up, moe_w_down):
    b, s, d = x.shape
    m = b * s
    row = lambda g: g.reshape(1, -1).astype(F32)
    ca, sa, cm, sm, ckr, skr, ctab, stab = _tables(s)
    h = x.reshape(m, d)
    pf = p.reshape(p.shape[0], m, PLE_DIM)

    win, wuq, wk, wp, wv, wout = _l0_weights(ab_w_in[0], mla_w_uq[0], mla_w_ukv[0], ab_w_out[0])
    gq = jnp.tile(row(gqa_q_norm_g[0]), (1, 2))
    gk = jnp.tile(row(gqa_k_norm_g[0]), (1, 2))
    qa, ka, vat, qm, km, vmt = _l0_in(h, row(mix_pre_g[0]), win, gq, gk, ca, sa,
                                      row(mla_q_norm_g[0]), wuq, cm, sm,
                                      row(mla_kv_norm_g[0]), wk, wp, wv, ckr, skr, b, s)
    oa = _gqa_attn(qa, ka, vat, b, s)
    om = _mla_attn(qm, km, vmt, b, s)
    h = _l0_tail(oa, om, wout, row(mix_post_g[0]), h, row(ffn_pre_g[0]),
                 ffn_w_gate[0].astype(BF16), ffn_w_up[0].astype(BF16), ffn_w_down[0].astype(BF16),
                 row(ffn_post_g[0]), pf[0], row(ple_gate_norm_g[0]), ple_w_gate[0].astype(BF16),
                 ple_w_proj[0].astype(BF16), row(ple_post_g[0]))

    qkv = _l1_qkv(h, row(mix_pre_g[1]), dil_w_qkv[0].astype(BF16), ctab, stab, s)
    o = _dil_attn(qkv, b, s)
    wr = jnp.pad(moe_w_router[0].astype(BF16), ((0, 0), (0, LANES - N_EXPERTS)))
    h, a, idx, gate, cnt = _l1_router(o, dil_w_out[0].astype(BF16), row(mix_post_g[1]), h,
                                      row(ffn_pre_g[1]), wr)
    dest, slot_tok, tile_e, tile_valid = _moe_plan(idx, cnt, m)
    ys = _experts(tile_e, tile_valid, a[slot_tok], moe_w_gate[0], moe_w_up[0], moe_w_down[0])
    h = _combine_ple(ys[dest[:, 0]], ys[dest[:, 1]], gate, row(ffn_post_g[1]), h, pf[1],
                     row(ple_gate_norm_g[1]), ple_w_gate[1].astype(BF16), ple_w_proj[1].astype(BF16),
                     row(ple_post_g[1]))
    return h.reshape(b, s, d)
```

```python
import functools

import jax
import jax.numpy as jnp
from jax import lax
from jax.experimental import pallas as pl
from jax.experimental.pallas import tpu as pltpu

F32 = jnp.float32
BF16 = jnp.bfloat16

D_MODEL = 1024
GRID_W = 64
HEAD_DIM = 64
NORM_EPS = 1e-6
NEG_INF = -1e30
ROPE_THETA = 500000.0
PARTIAL_ROPE_DIM = HEAD_DIM // 4
GQA_Q_HEADS = 8
GQA_KV_HEADS = 2
AXIAL_THETA = 10000.0
MLA_HEADS = 8
MLA_Q_RANK = 256
MLA_KV_RANK = 128
MLA_NOPE_DIM = 64
MLA_ROPE_DIM = 32
MLA_V_DIM = 64
MLA_ROPE_THETA = 10000.0
DIL_CONFIGS = ((128, 1), (512, 4), (2048, 16))
DIL_GROUPS = len(DIL_CONFIGS)
DIL_HEADS = 8
DIL_HALF = 64
DIL_QB = 128
LOG2_E = 1.4426950408889634
DIL_IN_WIDTH = DIL_GROUPS * 3 * DIL_HEADS * HEAD_DIM
D_FF = 3584
N_EXPERTS = 8
TOP_K = 2
PLE_DIM = 256

LANES = 128
VMEM_BYTES = 64 * 1024 * 1024
MIB = 1024 * 1024

TM = 512
TQ = 512
KEY_CHUNK = 256
V_ONES = 16
V_ROWS = LANES + V_ONES
TF = 512
TM_FFN = 1024
TM_MOE = 1024
AB_IN_PAD = 1280
MOE_CHUNKS = (1, 3, 4, 4)


def _params(semantics, vmem_mib):
    return pltpu.CompilerParams(dimension_semantics=semantics,
                                vmem_limit_bytes=vmem_mib * MIB)


def _dot(a, b):
    return jnp.dot(a, b, preferred_element_type=F32)


def _rms(x, g):
    return x * lax.rsqrt(jnp.mean(x * x, axis=-1, keepdims=True) + NORM_EPS) * g


def _lane_iota(shape):
    return lax.broadcasted_iota(jnp.int32, shape, len(shape) - 1)


def _swap_halves(x, k):
    w = x.shape[-1]
    fwd = pltpu.roll(x, w - k, 1)
    bwd = pltpu.roll(x, k, 1)
    return jnp.where((_lane_iota(x.shape) % (2 * k)) < k, fwd, bwd)


def _rotate(x, c, s, k):
    return x * c + _swap_halves(x, k) * s


def _head_rms(x, g):
    lo = _lane_iota(x.shape) < HEAD_DIM
    x2 = x * x
    s_all = jnp.sum(x2, axis=-1, keepdims=True)
    s_lo = jnp.sum(jnp.where(lo, x2, 0.0), axis=-1, keepdims=True)
    ms = jnp.where(lo, s_lo, s_all - s_lo) * (1.0 / HEAD_DIM)
    return x * lax.rsqrt(ms + NORM_EPS) * g


def _l0_in_kernel(h_ref, gpre_ref, win_ref, gq_ref, gk_ref, ca_ref, sa_ref,
                  gmq_ref, wuq_ref, cm_ref, sm_ref, gmkv_ref, wk_ref, wp_ref, wv_ref,
                  ckr_ref, skr_ref,
                  qa_ref, ka_ref, vat_ref, qm_ref, km_ref, vmt_ref):
    a = _rms(h_ref[...], gpre_ref[...]).astype(BF16)
    z = _dot(a, win_ref[...])
    ca, sa = ca_ref[...], sa_ref[...]
    for j in range(4):
        xq = _head_rms(z[:, j * LANES:(j + 1) * LANES], gq_ref[...])
        qa_ref[:, j * LANES:(j + 1) * LANES] = (
            _rotate(xq, ca, sa, 16) * (HEAD_DIM ** -0.5 * LOG2_E)).astype(BF16)
    ka_ref[...] = _rotate(_head_rms(z[:, 512:640], gk_ref[...]), ca, sa, 16).astype(BF16)
    ones = jnp.ones((V_ONES, z.shape[0]), BF16)
    vat_ref[0:LANES, :] = z[:, 640:768].T.astype(BF16)
    vat_ref[LANES:, :] = ones
    cq = _rms(z[:, 768:1024], gmq_ref[...]).astype(BF16)
    qm = _dot(cq, wuq_ref[...])
    cm, sm = cm_ref[...], sm_ref[...]
    scale_m = (MLA_NOPE_DIM + MLA_ROPE_DIM) ** -0.5 * LOG2_E
    for hb in range(MLA_HEADS):
        x = qm[:, hb * LANES:(hb + 1) * LANES]
        qm_ref[:, hb * LANES:(hb + 1) * LANES] = (_rotate(x, cm, sm, 16) * scale_m).astype(BF16)
    ckv = _rms(z[:, 1024:1152], gmkv_ref[...]).astype(BF16)
    kr = _rotate(z[:, 1152:1280], ckr_ref[...], skr_ref[...], 16).astype(BF16)
    km = _dot(ckv, wk_ref[...]) + _dot(kr, wp_ref[...])
    km_ref[...] = km.astype(BF16)
    vm = _dot(ckv, wv_ref[...])
    for j in range(4):
        vmt_ref[j * V_ROWS:j * V_ROWS + LANES, :] = vm[:, j * LANES:(j + 1) * LANES].T.astype(BF16)
        vmt_ref[j * V_ROWS + LANES:(j + 1) * V_ROWS, :] = ones


def _l0_in(h, gpre, win, gq, gk, ca, sa, gmq, wuq, cm, sm, gmkv, wk, wp, wv, ckr, skr, b, s):
    m = h.shape[0]
    nt = s // TM
    row = lambda i: (i, 0)
    const = lambda i: (0, 0)
    tab = lambda i: (i % nt, 0)
    full = lambda a: pl.BlockSpec(a.shape, const)
    in_specs = [pl.BlockSpec((TM, D_MODEL), row), full(gpre), full(win), full(gq), full(gk),
                pl.BlockSpec((TM, LANES), tab), pl.BlockSpec((TM, LANES), tab),
                full(gmq), full(wuq), pl.BlockSpec((TM, LANES), tab), pl.BlockSpec((TM, LANES), tab),
                full(gmkv), full(wk), full(wp), full(wv),
                pl.BlockSpec((TM, LANES), tab), pl.BlockSpec((TM, LANES), tab)]
    tr = lambda i: (i // nt, 0, i % nt)
    out_specs = [pl.BlockSpec((TM, 512), row),
                 pl.BlockSpec((TM, LANES), row),
                 pl.BlockSpec((None, V_ROWS, TM), tr),
                 pl.BlockSpec((TM, 1024), row),
                 pl.BlockSpec((TM, 1024), row),
                 pl.BlockSpec((None, 4 * V_ROWS, TM), tr)]
    out_shape = [jax.ShapeDtypeStruct((m, 512), BF16),
                 jax.ShapeDtypeStruct((m, LANES), BF16),
                 jax.ShapeDtypeStruct((b, V_ROWS, s), BF16),
                 jax.ShapeDtypeStruct((m, 1024), BF16),
                 jax.ShapeDtypeStruct((m, 1024), BF16),
                 jax.ShapeDtypeStruct((b, 4 * V_ROWS, s), BF16)]
    return pl.pallas_call(
        _l0_in_kernel, grid=(m // TM,), in_specs=in_specs, out_specs=out_specs,
        out_shape=out_shape, compiler_params=_params(("parallel",), 48), name="l0_in",
    )(h, gpre, win, gq, gk, ca, sa, gmq, wuq, cm, sm, gmkv, wk, wp, wv, ckr, skr)


def _attend_pair(score_t, vt_ref, o_ref):
    n_chunks = vt_ref.shape[1] // KEY_CHUNK
    tq = o_ref.shape[0]
    keys = [slice(c * KEY_CHUNK, (c + 1) * KEY_CHUNK) for c in range(n_chunks)]
    parts = []
    s_next = score_t(keys[0])
    for c in range(n_chunks):
        s_t = s_next
        if c + 1 < n_chunks:
            s_next = score_t(keys[c + 1])
        m = jnp.max(s_t, axis=0, keepdims=True)
        parts.append((m, _dot(vt_ref[:, keys[c]], jnp.exp2(s_t - m).astype(BF16))))
    m_all = functools.reduce(jnp.maximum, [m for m, _ in parts])
    o_t = sum(o_c * jnp.exp2(m_c - m_all) for m_c, o_c in parts)
    o_t = o_t[0:LANES, :] / o_t[LANES:LANES + 1, :]
    first = lax.broadcasted_iota(jnp.int32, (LANES, tq), 0) < HEAD_DIM
    o_ref[...] = jnp.where(first, o_t[:, 0:tq], o_t[:, tq:]).T.astype(BF16)


def _transposed(q):
    return q.astype(F32).T.astype(BF16)


def _gqa_attn_kernel(q_ref, k_ref, vt_ref, o_ref):
    qt = _transposed(q_ref[...])
    first = lax.broadcasted_iota(jnp.int32, qt.shape, 0) < HEAD_DIM
    zero = jnp.zeros_like(qt)
    w = jnp.concatenate([jnp.where(first, qt, zero), jnp.where(first, zero, qt)], axis=1)
    _attend_pair(lambda keys: _dot(k_ref[keys, :], w), vt_ref, o_ref)


def _gqa_attn(qa, ka, vat, b, s):
    m = qa.shape[0]
    nq = s // TQ
    return pl.pallas_call(
        _gqa_attn_kernel, grid=(b, nq, 4),
        in_specs=[pl.BlockSpec((TQ, LANES), lambda bi, i, j: (bi * nq + i, j)),
                  pl.BlockSpec((s, LANES), lambda bi, i, j: (bi, 0)),
                  pl.BlockSpec((None, V_ROWS, s), lambda bi, i, j: (bi, 0, 0))],
        out_specs=pl.BlockSpec((TQ, LANES), lambda bi, i, j: (bi * nq + i, j)),
        out_shape=jax.ShapeDtypeStruct((m, 512), BF16),
        compiler_params=_params(("parallel", "parallel", "parallel"), 32), name="gqa_attn",
    )(qa, ka, vat)


def _mla_attn_kernel(q_ref, k_ref, vt_ref, o_ref):
    qt0 = _transposed(q_ref[:, 0:LANES])
    qt1 = _transposed(q_ref[:, LANES:2 * LANES])

    def score_t(keys):
        return jnp.concatenate([_dot(k_ref[keys, 0:LANES], qt0),
                                _dot(k_ref[keys, LANES:2 * LANES], qt1)], axis=1)

    _attend_pair(score_t, vt_ref, o_ref)


def _mla_attn(qm, km, vmt, b, s):
    m = qm.shape[0]
    nq = s // TQ
    return pl.pallas_call(
        _mla_attn_kernel, grid=(b, nq, 4),
        in_specs=[pl.BlockSpec((TQ, 2 * LANES), lambda bi, i, j: (bi * nq + i, j)),
                  pl.BlockSpec((s, 2 * LANES), lambda bi, i, j: (bi, j)),
                  pl.BlockSpec((None, V_ROWS, s), lambda bi, i, j: (bi, j, 0))],
        out_specs=pl.BlockSpec((TQ, LANES), lambda bi, i, j: (bi * nq + i, j)),
        out_shape=jax.ShapeDtypeStruct((m, 512), BF16),
        compiler_params=_params(("parallel", "parallel", "parallel"), 32), name="mla_attn",
    )(qm, km, vmt)


def _silu(x):
    return x * (1.0 / (1.0 + jnp.exp(-x)))


def _ple_update(h, p, ggn, wg, wp, gpost):
    gate = _dot(_rms(h, ggn).astype(BF16), wg)
    gate = 1.0 / (1.0 + jnp.exp(-gate))
    e = _dot(p.astype(BF16), wp) * gate
    return h + _rms(e, gpost)


def _l0_tail_kernel(oa_ref, om_ref, wo_ref, gmix_ref, h_ref, gpre_ref, wg_ref, wu_ref, wd_ref,
                    gpost_ref, p_ref, ggn_ref, wpg_ref, wpp_ref, gple_ref, out_ref, a_ref, acc_ref):
    j = pl.program_id(1)

    @pl.when(j == 0)
    def _():
        k = oa_ref.shape[1]
        y = _dot(oa_ref[...], wo_ref[0:k, :]) + _dot(om_ref[...], wo_ref[k:, :])
        h1 = h_ref[...] + _rms(y, gmix_ref[...])
        out_ref[...] = h1
        a_ref[...] = _rms(h1, gpre_ref[...]).astype(BF16)
        acc_ref[...] = jnp.zeros_like(acc_ref)

    a = a_ref[...]
    mid = (_silu(_dot(a, wg_ref[...])) * _dot(a, wu_ref[...])).astype(BF16)
    acc_ref[...] += _dot(mid, wd_ref[...])

    @pl.when(j == pl.num_programs(1) - 1)
    def _():
        h2 = out_ref[...] + _rms(acc_ref[...], gpost_ref[...])
        out_ref[...] = _ple_update(h2, p_ref[...], ggn_ref[...], wpg_ref[...], wpp_ref[...],
                                   gple_ref[...])


def _l0_tail(oa, om, wo, gmix, h, gpre, wg, wu, wd, gpost, p, ggn, wpg, wpp, gple):
    m = h.shape[0]
    dff = wg.shape[1]
    row = lambda i, j: (i, 0)
    const = lambda i, j: (0, 0)
    vec = lambda g: pl.BlockSpec(g.shape, const)
    once = lambda w: pl.BlockSpec(w.shape, const, pipeline_mode=pl.Buffered(1))
    return pl.pallas_call(
        _l0_tail_kernel, grid=(m // TM_FFN, dff // TF),
        in_specs=[pl.BlockSpec((TM_FFN, oa.shape[1]), row), pl.BlockSpec((TM_FFN, om.shape[1]), row),
                  once(wo), vec(gmix), pl.BlockSpec((TM_FFN, D_MODEL), row), vec(gpre),
                  pl.BlockSpec((D_MODEL, TF), lambda i, j: (0, j)),
                  pl.BlockSpec((D_MODEL, TF), lambda i, j: (0, j)),
                  pl.BlockSpec((TF, D_MODEL), lambda i, j: (j, 0)),
                  vec(gpost), pl.BlockSpec((TM_FFN, PLE_DIM), row), vec(ggn), once(wpg), once(wpp),
                  vec(gple)],
        out_specs=pl.BlockSpec((TM_FFN, D_MODEL), row),
        out_shape=jax.ShapeDtypeStruct((m, D_MODEL), F32),
        scratch_shapes=[pltpu.VMEM((TM_FFN, D_MODEL), BF16), pltpu.VMEM((TM_FFN, D_MODEL), F32)],
        compiler_params=_params(("parallel", "arbitrary"), 56), name="l0_tail",
    )(oa, om, wo, gmix, h, gpre, wg, wu, wd, gpost, p, ggn, wpg, wpp, gple)


def _l1_qkv_kernel(h_ref, gpre_ref, w_ref, c_ref, s_ref, out0_ref, out1_ref, out2_ref, stage_ref):
    a = _rms(h_ref[...], gpre_ref[...]).astype(BF16)
    tm = a.shape[0]
    outs = (out0_ref, out1_ref, out2_ref)
    for blk in range(DIL_IN_WIDTH // 512):
        gi, kind = divmod(blk, 3)
        dil = DIL_CONFIGS[gi][1]
        z = _dot(a, w_ref[:, blk * 512:(blk + 1) * 512])
        if kind < 2:
            c, s = c_ref[kind], s_ref[kind]
            z = jnp.concatenate(
                [_rotate(z[:, sub * LANES:(sub + 1) * LANES], c, s, PARTIAL_ROPE_DIM // 2)
                 for sub in range(4)], axis=1)
        cols = slice(kind * 512, (kind + 1) * 512)
        if dil == 1:
            outs[gi][0, :, cols] = z.astype(BF16)
            continue
        for sub in range(4):
            stage = stage_ref.at[(blk % 2) * 4 + sub]
            stage[...] = z[:, sub * LANES:(sub + 1) * LANES]
            lanes = slice(kind * 512 + sub * LANES, kind * 512 + (sub + 1) * LANES)
            for r in range(dil):
                outs[gi][r, :, lanes] = stage[pl.ds(r, tm // dil, stride=dil), :].astype(BF16)


def _l1_qkv(h, gpre, w, ctab, stab, b, s):
    m = h.shape[0]
    nt = s // TM
    const = lambda i: (0, 0)
    out_specs, out_shape = [], []
    for _, dil in DIL_CONFIGS:
        out_specs.append(pl.BlockSpec((None, dil, TM // dil, 1536), lambda i: (i // nt, 0, i % nt, 0)))
        out_shape.append(jax.ShapeDtypeStruct((b, dil, s // dil, 1536), BF16))
    return pl.pallas_call(
        _l1_qkv_kernel, grid=(m // TM,),
        in_specs=[pl.BlockSpec((TM, D_MODEL), lambda i: (i, 0)),
                  pl.BlockSpec(gpre.shape, const),
                  pl.BlockSpec(w.shape, const, pipeline_mode=pl.Buffered(1)),
                  pl.BlockSpec((2, TM, LANES), lambda i: (0, i % nt, 0)),
                  pl.BlockSpec((2, TM, LANES), lambda i: (0, i % nt, 0))],
        out_specs=out_specs, out_shape=out_shape,
        scratch_shapes=[pltpu.VMEM((8, TM, LANES), F32)],
        compiler_params=_params(("parallel",), 48), name="l1_qkv",
    )(h, gpre, w, ctab, stab)


def _dil_attn_kernel(q0_ref, k0_ref, v0_ref, q1_ref, k1_ref, v1_ref, q2_ref, k2_ref, v2_ref, o_ref,
                     ra_ref, rl_ref, rm_ref, acc_ref, sum_ref, max_ref):
    lo = _lane_iota((DIL_QB, LANES)) < HEAD_DIM
    groups = ((q0_ref, k0_ref, v0_ref), (q1_ref, k1_ref, v1_ref), (q2_ref, k2_ref, v2_ref))
    for gi, (q_ref, k_ref, v_ref) in enumerate(groups):
        dil, length = q_ref.shape[0], q_ref.shape[1]
        width = min(DIL_QB + 2 * DIL_HALF, length)
        nblk = length // DIL_QB
        first = gi == 0
        ta, tl, tm_ = (acc_ref, sum_ref, max_ref) if first else (ra_ref, rl_ref, rm_ref)
        row = lax.broadcasted_iota(jnp.int32, (2 * DIL_QB, width), 0) % DIL_QB
        col = lax.broadcasted_iota(jnp.int32, (2 * DIL_QB, width), 1)
        rel = row - col
        ones = jnp.ones((width, LANES), BF16)

        def body(it, carry, length=length, width=width, nblk=nblk, ta=ta, tl=tl, tm_=tm_, rel=rel,
                 ones=ones, q_ref=q_ref, k_ref=k_ref, v_ref=v_ref):
            r = it // nblk
            i = it % nblk
            q0 = pl.multiple_of(i * DIL_QB, DIL_QB)
            start = pl.multiple_of(jnp.clip(q0 - DIL_HALF, 0, length - width), DIL_HALF)
            qrow = pl.multiple_of(r * length + q0, DIL_HALF)
            q = q_ref[r, pl.ds(q0, DIL_QB), :]
            kw = k_ref[r, pl.ds(start, width), :]
            vw = jnp.concatenate([v_ref[r, pl.ds(start, width), :], ones], axis=1)
            zero = jnp.zeros_like(q)
            q2 = jnp.concatenate([jnp.where(lo, q, zero), jnp.where(lo, zero, q)], axis=0)
            sc = lax.dot_general(q2, kw, (((1,), (1,)), ((), ())), preferred_element_type=F32)
            sc = jnp.where(jnp.abs(rel + (q0 - start)) <= DIL_HALF, sc, NEG_INF)
            m = jnp.max(sc, axis=-1, keepdims=True)
            p = jnp.exp2(sc - m).astype(BF16)
            o2 = _dot(p, vw)
            ta[pl.ds(qrow, DIL_QB), :] = jnp.where(lo, o2[:DIL_QB, 0:LANES], o2[DIL_QB:, 0:LANES])
            tl[pl.ds(qrow, DIL_QB), :] = jnp.where(lo, o2[:DIL_QB, LANES:], o2[DIL_QB:, LANES:])
            tm_[pl.ds(qrow, DIL_QB), :] = jnp.where(lo, m[:DIL_QB], m[DIL_QB:])
            return carry

        lax.fori_loop(0, dil * nblk, body, 0, unroll=4)
        if first:
            continue
        for r in range(dil):
            rows = pl.ds(r, length, stride=dil)
            blk = slice(r * length, (r + 1) * length)
            m_old, m_blk = max_ref[rows, :], rm_ref[blk, :]
            m_new = jnp.maximum(m_old, m_blk)
            w_old, w_blk = jnp.exp2(m_old - m_new), jnp.exp2(m_blk - m_new)
            acc_ref[rows, :] = acc_ref[rows, :] * w_old + ra_ref[blk, :] * w_blk
            sum_ref[rows, :] = sum_ref[rows, :] * w_old + rl_ref[blk, :] * w_blk
            max_ref[rows, :] = m_new
    o_ref[...] = (acc_ref[...] / sum_ref[...]).astype(BF16)


def _dil_attn(qkvs, b, s):
    in_specs, args = [], []
    for x in qkvs:
        for j in range(3):
            in_specs.append(pl.BlockSpec((None,) + x.shape[1:3] + (LANES,),
                                         lambda bi, hp, j=j: (bi, 0, 0, 4 * j + hp)))
            args.append(x)
    stat = pltpu.VMEM((s, LANES), F32)
    return pl.pallas_call(
        _dil_attn_kernel, grid=(b, 4), in_specs=in_specs,
        out_specs=pl.BlockSpec((s, LANES), lambda bi, hp: (bi, hp)),
        out_shape=jax.ShapeDtypeStruct((b * s, 512), BF16),
        scratch_shapes=[stat] * 6,
        compiler_params=_params(("parallel", "parallel"), 40), name="dil_attn",
    )(*args)


def _l1_router_kernel(o_ref, wo_ref, gmix_ref, h_ref, gpre_ref, wr_ref,
                      h1_ref, a_ref, idx_ref, gate_ref, cnt_ref, carry_ref):
    @pl.when(pl.program_id(0) == 0)
    def _():
        carry_ref[...] = jnp.zeros_like(carry_ref)

    h1 = h_ref[...] + _rms(_dot(o_ref[...], wo_ref[...]), gmix_ref[...])
    h1_ref[...] = h1
    a = _rms(h1, gpre_ref[...]).astype(BF16)
    a_ref[...] = a
    logits = _dot(a, wr_ref[...])
    tm = logits.shape[0]
    lane = _lane_iota(logits.shape)
    lanef = lane.astype(F32)
    lg = jnp.where(lane < N_EXPERTS, logits, -jnp.inf)
    m1 = jnp.max(lg, axis=-1, keepdims=True)
    i1 = jnp.min(jnp.where(lg == m1, lanef, float(LANES)), axis=-1, keepdims=True)
    lg2 = jnp.where(lanef == i1, -jnp.inf, lg)
    m2 = jnp.max(lg2, axis=-1, keepdims=True)
    i2 = jnp.min(jnp.where(lg2 == m2, lanef, float(LANES)), axis=-1, keepdims=True)
    t = jnp.exp(m2 - m1)
    g1 = 1.0 / (1.0 + t)
    g2 = t / (1.0 + t)
    hit1 = lanef == i1
    hit2 = lanef == i2
    onehot = jnp.where(hit1 | hit2, 1.0, 0.0)
    r = lax.broadcasted_iota(jnp.int32, (tm, tm), 0)
    c = lax.broadcasted_iota(jnp.int32, (tm, tm), 1)
    tri = jnp.where(c < r, 1.0, 0.0).astype(BF16)
    before = _dot(tri, onehot.astype(BF16)) + carry_ref[...]
    rank1 = jnp.sum(jnp.where(hit1, before, 0.0), axis=-1, keepdims=True)
    rank2 = jnp.sum(jnp.where(hit2, before, 0.0), axis=-1, keepdims=True)
    info = jnp.where(lane == 0, i1, jnp.where(lane == 1, i2, jnp.where(lane == 2, rank1, rank2)))
    idx_ref[...] = info.astype(jnp.int32)
    gate_ref[...] = jnp.where(lane == 0, g1, g2)
    carry_ref[...] += jnp.sum(onehot, axis=0, keepdims=True)
    cnt_ref[...] = carry_ref[...]


def _l1_router(o, wo, gmix, h, gpre, wr):
    m = h.shape[0]
    row = lambda i: (i, 0)
    const = lambda i: (0, 0)
    full = lambda x: pl.BlockSpec(x.shape, const)
    return pl.pallas_call(
        _l1_router_kernel, grid=(m // TM,),
        in_specs=[pl.BlockSpec((TM, o.shape[1]), row), full(wo), full(gmix),
                  pl.BlockSpec((TM, D_MODEL), row), full(gpre), full(wr)],
        out_specs=[pl.BlockSpec((TM, D_MODEL), row), pl.BlockSpec((TM, D_MODEL), row),
                   pl.BlockSpec((TM, LANES), row), pl.BlockSpec((TM, LANES), row),
                   pl.BlockSpec((1, LANES), const)],
        out_shape=[jax.ShapeDtypeStruct((m, D_MODEL), F32),
                   jax.ShapeDtypeStruct((m, D_MODEL), BF16),
                   jax.ShapeDtypeStruct((m, LANES), jnp.int32),
                   jax.ShapeDtypeStruct((m, LANES), F32),
                   jax.ShapeDtypeStruct((1, LANES), F32)],
        scratch_shapes=[pltpu.VMEM((1, LANES), F32)],
        compiler_params=_params(("arbitrary",), 32), name="l1_router",
    )(o, wo, gmix, h, gpre, wr)


def _expert_kernel(te_ref, tv_ref, x_ref, wg_ref, wu_ref, wd_ref, *refs):
    y_ref, acc_ref = refs[-2:]
    i = pl.program_id(0)
    j = pl.program_id(1)
    last = pl.num_programs(1) - 1
    valid = tv_ref[i] > 0

    @pl.when(valid & (j == 0))
    def _():
        acc_ref[...] = jnp.zeros_like(acc_ref)

    @pl.when(valid)
    def _():
        x = x_ref[...]
        gate = _dot(x, wg_ref[...].astype(BF16))
        up = _dot(x, wu_ref[...].astype(BF16))
        acc_ref[...] += _dot((_silu(gate) * up).astype(BF16), wd_ref[...].astype(BF16))

    @pl.when(valid & (j == last))
    def _():
        y_ref[...] = acc_ref[...].astype(BF16)

    @pl.when(jnp.logical_not(valid) & (j == last))
    def _():
        y_ref[...] = jnp.zeros_like(y_ref)


def _experts(tile_e, tile_valid, xs, wg, wu, wd, ys, first_tile, n_slots):
    n_tiles = xs.shape[0] // TM_MOE
    dff = wg.shape[2]
    te = tile_e[first_tile:first_tile + n_tiles]
    tv = tile_valid[first_tile:first_tile + n_tiles]
    in_specs = [pl.BlockSpec((TM_MOE, D_MODEL), lambda i, j, te, tv: (i, 0)),
                pl.BlockSpec((None, D_MODEL, TF), lambda i, j, te, tv: (te[i], 0, j * tv[i])),
                pl.BlockSpec((None, D_MODEL, TF), lambda i, j, te, tv: (te[i], 0, j * tv[i])),
                pl.BlockSpec((None, TF, D_MODEL), lambda i, j, te, tv: (te[i], j * tv[i], 0))]
    args = [te, tv, xs, wg, wu, wd]
    aliases = {}
    if ys is not None:
        in_specs.append(pl.BlockSpec(memory_space=pl.ANY))
        aliases = {len(args): 0}
        args.append(ys)
    grid_spec = pltpu.PrefetchScalarGridSpec(
        num_scalar_prefetch=2, grid=(n_tiles, dff // TF), in_specs=in_specs,
        out_specs=pl.BlockSpec((TM_MOE, D_MODEL), lambda i, j, te, tv: (first_tile + i, 0)),
        scratch_shapes=[pltpu.VMEM((TM_MOE, D_MODEL), F32)])
    return pl.pallas_call(
        _expert_kernel, grid_spec=grid_spec,
        out_shape=jax.ShapeDtypeStruct((n_slots, D_MODEL), BF16),
        input_output_aliases=aliases,
        compiler_params=_params(("parallel", "arbitrary"), 48), name="moe_experts",
    )(*args)


def _combine_ple_kernel(y0_ref, y1_ref, gate_ref, g_ref, h_ref, p_ref, ggn_ref, wpg_ref, wpp_ref,
                        gple_ref, out_ref):
    gate = gate_ref[...]
    f = y0_ref[...].astype(F32) * gate[:, 0:1] + y1_ref[...].astype(F32) * gate[:, 1:2]
    h2 = h_ref[...] + _rms(f, g_ref[...])
    out_ref[...] = _ple_update(h2, p_ref[...], ggn_ref[...], wpg_ref[...], wpp_ref[...], gple_ref[...])


def _combine_ple(y0, y1, gate, g, h, p, ggn, wpg, wpp, gple):
    m = h.shape[0]
    row = lambda i: (i, 0)
    const = lambda i: (0, 0)
    full = lambda x: pl.BlockSpec(x.shape, const)
    act = pl.BlockSpec((TM, D_MODEL), row)
    return pl.pallas_call(
        _combine_ple_kernel, grid=(m // TM,),
        in_specs=[act, act, pl.BlockSpec((TM, LANES), row), full(g), act,
                  pl.BlockSpec((TM, PLE_DIM), row), full(ggn), full(wpg), full(wpp), full(gple)],
        out_specs=act,
        out_shape=jax.ShapeDtypeStruct((m, D_MODEL), F32),
        compiler_params=_params(("parallel",), 40), name="combine_ple",
    )(y0, y1, gate, g, h, p, ggn, wpg, wpp, gple)


def _moe_plan(idx, cnt, n):
    counts = cnt[0, :N_EXPERTS].astype(jnp.int32)
    padded = (counts + TM_MOE - 1) // TM_MOE * TM_MOE
    end_pad = jnp.cumsum(padded)
    start_pad = end_pad - padded
    experts = jnp.arange(N_EXPERTS, dtype=jnp.int32)
    start = jnp.sum(jnp.where(idx[:, 0:2, None] == experts, start_pad, 0), axis=-1)
    dest = start + idx[:, 2:4]
    n_slots = n * TOP_K + N_EXPERTS * TM_MOE
    tok = jnp.repeat(jnp.arange(n, dtype=jnp.int32), TOP_K)
    slot_tok = jnp.zeros((n_slots,), jnp.int32).at[dest.reshape(-1)].set(tok, unique_indices=True)
    tile_start = jnp.arange(n_slots // TM_MOE, dtype=jnp.int32) * TM_MOE
    tile_e = jnp.minimum(jnp.sum(tile_start[:, None] >= end_pad[None, :], axis=1),
                         N_EXPERTS - 1).astype(jnp.int32)
    tile_valid = (tile_start < end_pad[-1]).astype(jnp.int32)
    return dest, slot_tok, tile_e, tile_valid


def _rotary_tables(pos, dim, theta):
    exponent = jnp.arange(0, dim, 2, dtype=F32) / dim
    inv_freq = jnp.power(jnp.float32(theta), -exponent)
    ang = pos.astype(F32)[:, None] * inv_freq[None, :]
    return jnp.cos(ang), jnp.sin(ang)


def _tables(s):
    t = jnp.arange(s, dtype=jnp.int32)
    cr, sr = _rotary_tables(t // GRID_W, HEAD_DIM // 2, AXIAL_THETA)
    cc, sc = _rotary_tables(t % GRID_W, HEAD_DIM // 2, AXIAL_THETA)
    ca = jnp.tile(jnp.concatenate([cr, cr, cc, cc], axis=1), (1, 2))
    sa = jnp.tile(jnp.concatenate([-sr, sr, -sc, sc], axis=1), (1, 2))
    cm_, sm_ = _rotary_tables(t, MLA_ROPE_DIM, MLA_ROPE_THETA)
    one = jnp.ones((s, 1), F32)
    zero = jnp.zeros((s, 1), F32)
    cm = jnp.concatenate([jnp.tile(one, (1, 64)), cm_, cm_, jnp.tile(one, (1, 32))], axis=1)
    sm = jnp.concatenate([jnp.tile(zero, (1, 64)), -sm_, sm_, jnp.tile(zero, (1, 32))], axis=1)
    ckr = jnp.concatenate([cm_, cm_, jnp.tile(zero, (1, 96))], axis=1)
    skr = jnp.concatenate([-sm_, sm_, jnp.tile(zero, (1, 96))], axis=1)
    cp_, sp_ = _rotary_tables(t, PARTIAL_ROPE_DIM, ROPE_THETA)
    cp = jnp.tile(jnp.concatenate([cp_, cp_, jnp.tile(one, (1, 48))], axis=1), (1, 2))
    sp = jnp.tile(jnp.concatenate([-sp_, sp_, jnp.tile(zero, (1, 48))], axis=1), (1, 2))
    scale = HEAD_DIM ** -0.5 * LOG2_E
    ctab = jnp.stack([cp * scale, cp])
    stab = jnp.stack([sp * scale, sp])
    return ca, sa, cm, sm, ckr, skr, ctab, stab


_GQA_PERM = (0, 4, 1, 5, 2, 6, 3, 7)


def _l0_weights(w_in, w_uq, w_ukv, w_out):
    d = w_in.shape[0]
    q = w_in[:, :512].reshape(d, 8, 64)[:, _GQA_PERM, :].reshape(d, 512)
    win = jnp.concatenate([q, w_in[:, 512:], jnp.zeros((d, AB_IN_PAD - w_in.shape[1]), w_in.dtype)],
                          axis=1).astype(BF16)
    uq = w_uq.reshape(MLA_Q_RANK, MLA_HEADS, MLA_NOPE_DIM + MLA_ROPE_DIM)
    wuq = jnp.pad(uq, ((0, 0), (0, 0), (0, LANES - uq.shape[2]))).reshape(MLA_Q_RANK, -1).astype(BF16)
    ukv = w_ukv.reshape(MLA_KV_RANK, MLA_HEADS, MLA_NOPE_DIM + MLA_V_DIM)
    wk = jnp.pad(ukv[:, :, :MLA_NOPE_DIM], ((0, 0), (0, 0), (0, LANES - MLA_NOPE_DIM)))
    wk = wk.reshape(MLA_KV_RANK, -1).astype(BF16)
    wv = ukv[:, :, MLA_NOPE_DIM:].reshape(MLA_KV_RANK, -1).astype(BF16)
    src = jnp.arange(LANES)[:, None]
    dst = jnp.arange(MLA_HEADS * LANES)[None, :]
    wp = ((src < MLA_ROPE_DIM) & (dst % LANES == src + MLA_NOPE_DIM)).astype(BF16)
    oa = w_out[:512].reshape(8, 64, -1)[_GQA_PERM, :, :].reshape(512, -1)
    wout = jnp.concatenate([oa, w_out[512:]], axis=0).astype(BF16)
    return win, wuq, wk, wp, wv, wout


def kernel(x, p, mix_pre_g, mix_post_g, ffn_pre_g, ffn_post_g, ple_w_proj, ple_gate_norm_g, ple_w_gate, ple_post_g, ab_w_in, gqa_q_norm_g, gqa_k_norm_g, mla_q_norm_g, mla_w_uq, mla_kv_norm_g, mla_w_ukv, ab_w_out, ffn_w_gate, ffn_w_up, ffn_w_down, dil_w_qkv, dil_w_out, moe_w_router, moe_w_gate, moe_w_up, moe_w_down):
    b, s, d = x.shape
    m = b * s
    row = lambda g: g.reshape(1, -1).astype(F32)
    ca, sa, cm, sm, ckr, skr, ctab, stab = _tables(s)
    h = x.reshape(m, d)
    pf = p.reshape(p.shape[0], m, PLE_DIM)

    win, wuq, wk, wp, wv, wout = _l0_weights(ab_w_in[0], mla_w_uq[0], mla_w_ukv[0], ab_w_out[0])
    gq = jnp.tile(row(gqa_q_norm_g[0]), (1, 2))
    gk = jnp.tile(row(gqa_k_norm_g[0]), (1, 2))
    qa, ka, vat, qm, km, vmt = _l0_in(h, row(mix_pre_g[0]), win, gq, gk, ca, sa,
                                      row(mla_q_norm_g[0]), wuq, cm, sm,
                                      row(mla_kv_norm_g[0]), wk, wp, wv, ckr, skr, b, s)
    oa = _gqa_attn(qa, ka, vat, b, s)
    om = _mla_attn(qm, km, vmt, b, s)
    h = _l0_tail(oa, om, wout, row(mix_post_g[0]), h, row(ffn_pre_g[0]),
                 ffn_w_gate[0].astype(BF16), ffn_w_up[0].astype(BF16), ffn_w_down[0].astype(BF16),
                 row(ffn_post_g[0]), pf[0], row(ple_gate_norm_g[0]), ple_w_gate[0].astype(BF16),
                 ple_w_proj[0].astype(BF16), row(ple_post_g[0]))

    qkvs = _l1_qkv(h, row(mix_pre_g[1]), dil_w_qkv[0].astype(BF16), ctab, stab, b, s)
    o = _dil_attn(qkvs, b, s)
    wr = jnp.pad(moe_w_router[0].astype(BF16), ((0, 0), (0, LANES - N_EXPERTS)))
    h, a, idx, gate, cnt = _l1_router(o, dil_w_out[0].astype(BF16), row(mix_post_g[1]), h,
                                      row(ffn_pre_g[1]), wr)
    dest, slot_tok, tile_e, tile_valid = _moe_plan(idx, cnt, m)
    n_slots = slot_tok.shape[0]
    ys, first_tile = None, 0
    assert (n_slots // TM_MOE) % sum(MOE_CHUNKS) == 0
    for share in MOE_CHUNKS:
        n_tiles = n_slots // TM_MOE * share // sum(MOE_CHUNKS)
        rows = slice(first_tile * TM_MOE, (first_tile + n_tiles) * TM_MOE)
        ys = _experts(tile_e, tile_valid, a[slot_tok[rows]], moe_w_gate[0], moe_w_up[0],
                      moe_w_down[0], ys, first_tile, n_slots)
        first_tile += n_tiles
    h = _combine_ple(ys[dest[:, 0]], ys[dest[:, 1]], gate, row(ffn_post_g[1]), h, pf[1],
                     row(ple_gate_norm_g[1]), ple_w_gate[1].astype(BF16), ple_w_proj[1].astype(BF16),
                     row(ple_post_g[1]))
    return h.reshape(b, s, d)
```

```python
import functools

import jax
import jax.numpy as jnp
from jax import lax
from jax.experimental import pallas as pl
from jax.experimental.pallas import tpu as pltpu

F32 = jnp.float32
BF16 = jnp.bfloat16

D_MODEL = 1024
GRID_W = 64
HEAD_DIM = 64
NORM_EPS = 1e-6
NEG_INF = -1e30
ROPE_THETA = 500000.0
PARTIAL_ROPE_DIM = HEAD_DIM // 4
GQA_Q_HEADS = 8
GQA_KV_HEADS = 2
AXIAL_THETA = 10000.0
MLA_HEADS = 8
MLA_Q_RANK = 256
MLA_KV_RANK = 128
MLA_NOPE_DIM = 64
MLA_ROPE_DIM = 32
MLA_V_DIM = 64
MLA_ROPE_THETA = 10000.0
DIL_CONFIGS = ((128, 1), (512, 4), (2048, 16))
DIL_GROUPS = len(DIL_CONFIGS)
DIL_HEADS = 8
DIL_HALF = 64
DIL_QB = 128
LOG2_E = 1.4426950408889634
DIL_IN_WIDTH = DIL_GROUPS * 3 * DIL_HEADS * HEAD_DIM
D_FF = 3584
N_EXPERTS = 8
TOP_K = 2
PLE_DIM = 256

LANES = 128
VMEM_BYTES = 64 * 1024 * 1024
MIB = 1024 * 1024

TM = 512
TQ = 512
KEY_CHUNK = 256
V_ONES = 16
V_ROWS = LANES + V_ONES
TF = 512
TM_FFN = 1024
TM_MOE = 1024
AB_IN_PAD = 1280
DISPATCH_TOKENS = 512


def _params(semantics, vmem_mib):
    return pltpu.CompilerParams(dimension_semantics=semantics,
                                vmem_limit_bytes=vmem_mib * MIB)


def _dot(a, b):
    return jnp.dot(a, b, preferred_element_type=F32)


def _rms(x, g):
    return x * lax.rsqrt(jnp.mean(x * x, axis=-1, keepdims=True) + NORM_EPS) * g


def _lane_iota(shape):
    return lax.broadcasted_iota(jnp.int32, shape, len(shape) - 1)


def _swap_halves(x, k):
    w = x.shape[-1]
    fwd = pltpu.roll(x, w - k, 1)
    bwd = pltpu.roll(x, k, 1)
    return jnp.where((_lane_iota(x.shape) % (2 * k)) < k, fwd, bwd)


def _rotate(x, c, s, k):
    return x * c + _swap_halves(x, k) * s


def _head_rms(x, g):
    lo = _lane_iota(x.shape) < HEAD_DIM
    x2 = x * x
    s_all = jnp.sum(x2, axis=-1, keepdims=True)
    s_lo = jnp.sum(jnp.where(lo, x2, 0.0), axis=-1, keepdims=True)
    ms = jnp.where(lo, s_lo, s_all - s_lo) * (1.0 / HEAD_DIM)
    return x * lax.rsqrt(ms + NORM_EPS) * g


def _l0_in_kernel(h_ref, gpre_ref, win_ref, gq_ref, gk_ref, ca_ref, sa_ref,
                  gmq_ref, wuq_ref, cm_ref, sm_ref, gmkv_ref, wk_ref, wp_ref, wv_ref,
                  ckr_ref, skr_ref,
                  qa_ref, ka_ref, vat_ref, qm_ref, km_ref, vmt_ref):
    a = _rms(h_ref[...], gpre_ref[...]).astype(BF16)
    z = _dot(a, win_ref[...])
    ca, sa = ca_ref[...], sa_ref[...]
    for j in range(4):
        xq = _head_rms(z[:, j * LANES:(j + 1) * LANES], gq_ref[...])
        qa_ref[:, j * LANES:(j + 1) * LANES] = (
            _rotate(xq, ca, sa, 16) * (HEAD_DIM ** -0.5 * LOG2_E)).astype(BF16)
    ka_ref[...] = _rotate(_head_rms(z[:, 512:640], gk_ref[...]), ca, sa, 16).astype(BF16)
    ones = jnp.ones((V_ONES, z.shape[0]), BF16)
    vat_ref[0:LANES, :] = z[:, 640:768].T.astype(BF16)
    vat_ref[LANES:, :] = ones
    cq = _rms(z[:, 768:1024], gmq_ref[...]).astype(BF16)
    qm = _dot(cq, wuq_ref[...])
    cm, sm = cm_ref[...], sm_ref[...]
    scale_m = (MLA_NOPE_DIM + MLA_ROPE_DIM) ** -0.5 * LOG2_E
    for hb in range(MLA_HEADS):
        x = qm[:, hb * LANES:(hb + 1) * LANES]
        qm_ref[:, hb * LANES:(hb + 1) * LANES] = (_rotate(x, cm, sm, 16) * scale_m).astype(BF16)
    ckv = _rms(z[:, 1024:1152], gmkv_ref[...]).astype(BF16)
    kr = _rotate(z[:, 1152:1280], ckr_ref[...], skr_ref[...], 16).astype(BF16)
    km = _dot(ckv, wk_ref[...]) + _dot(kr, wp_ref[...])
    km_ref[...] = km.astype(BF16)
    vm = _dot(ckv, wv_ref[...])
    for j in range(4):
        vmt_ref[j * V_ROWS:j * V_ROWS + LANES, :] = vm[:, j * LANES:(j + 1) * LANES].T.astype(BF16)
        vmt_ref[j * V_ROWS + LANES:(j + 1) * V_ROWS, :] = ones


def _l0_in(h, gpre, win, gq, gk, ca, sa, gmq, wuq, cm, sm, gmkv, wk, wp, wv, ckr, skr, b, s):
    m = h.shape[0]
    nt = s // TM
    row = lambda i: (i, 0)
    const = lambda i: (0, 0)
    tab = lambda i: (i % nt, 0)
    full = lambda a: pl.BlockSpec(a.shape, const)
    in_specs = [pl.BlockSpec((TM, D_MODEL), row), full(gpre), full(win), full(gq), full(gk),
                pl.BlockSpec((TM, LANES), tab), pl.BlockSpec((TM, LANES), tab),
                full(gmq), full(wuq), pl.BlockSpec((TM, LANES), tab), pl.BlockSpec((TM, LANES), tab),
                full(gmkv), full(wk), full(wp), full(wv),
                pl.BlockSpec((TM, LANES), tab), pl.BlockSpec((TM, LANES), tab)]
    tr = lambda i: (i // nt, 0, i % nt)
    out_specs = [pl.BlockSpec((TM, 512), row),
                 pl.BlockSpec((TM, LANES), row),
                 pl.BlockSpec((None, V_ROWS, TM), tr),
                 pl.BlockSpec((TM, 1024), row),
                 pl.BlockSpec((TM, 1024), row),
                 pl.BlockSpec((None, 4 * V_ROWS, TM), tr)]
    out_shape = [jax.ShapeDtypeStruct((m, 512), BF16),
                 jax.ShapeDtypeStruct((m, LANES), BF16),
                 jax.ShapeDtypeStruct((b, V_ROWS, s), BF16),
                 jax.ShapeDtypeStruct((m, 1024), BF16),
                 jax.ShapeDtypeStruct((m, 1024), BF16),
                 jax.ShapeDtypeStruct((b, 4 * V_ROWS, s), BF16)]
    return pl.pallas_call(
        _l0_in_kernel, grid=(m // TM,), in_specs=in_specs, out_specs=out_specs,
        out_shape=out_shape, compiler_params=_params(("parallel",), 48), name="l0_in",
    )(h, gpre, win, gq, gk, ca, sa, gmq, wuq, cm, sm, gmkv, wk, wp, wv, ckr, skr)


def _attend_pair(score_t, vt_ref, o_ref):
    n_chunks = vt_ref.shape[1] // KEY_CHUNK
    tq = o_ref.shape[0]
    keys = [slice(c * KEY_CHUNK, (c + 1) * KEY_CHUNK) for c in range(n_chunks)]
    parts = []
    s_next = score_t(keys[0])
    for c in range(n_chunks):
        s_t = s_next
        if c + 1 < n_chunks:
            s_next = score_t(keys[c + 1])
        m = jnp.max(s_t, axis=0, keepdims=True)
        parts.append((m, _dot(vt_ref[:, keys[c]], jnp.exp2(s_t - m).astype(BF16))))
    m_all = functools.reduce(jnp.maximum, [m for m, _ in parts])
    o_t = sum(o_c * jnp.exp2(m_c - m_all) for m_c, o_c in parts)
    o_t = o_t[0:LANES, :] / o_t[LANES:LANES + 1, :]
    first = lax.broadcasted_iota(jnp.int32, (LANES, tq), 0) < HEAD_DIM
    o_ref[...] = jnp.where(first, o_t[:, 0:tq], o_t[:, tq:]).T.astype(BF16)


def _transposed(q):
    return q.astype(F32).T.astype(BF16)


def _gqa_attn_kernel(q_ref, k_ref, vt_ref, o_ref):
    qt = _transposed(q_ref[...])
    first = lax.broadcasted_iota(jnp.int32, qt.shape, 0) < HEAD_DIM
    zero = jnp.zeros_like(qt)
    w = jnp.concatenate([jnp.where(first, qt, zero), jnp.where(first, zero, qt)], axis=1)
    _attend_pair(lambda keys: _dot(k_ref[keys, :], w), vt_ref, o_ref)


def _gqa_attn(qa, ka, vat, b, s):
    m = qa.shape[0]
    nq = s // TQ
    return pl.pallas_call(
        _gqa_attn_kernel, grid=(b, nq, 4),
        in_specs=[pl.BlockSpec((TQ, LANES), lambda bi, i, j: (bi * nq + i, j)),
                  pl.BlockSpec((s, LANES), lambda bi, i, j: (bi, 0)),
                  pl.BlockSpec((None, V_ROWS, s), lambda bi, i, j: (bi, 0, 0))],
        out_specs=pl.BlockSpec((TQ, LANES), lambda bi, i, j: (bi * nq + i, j)),
        out_shape=jax.ShapeDtypeStruct((m, 512), BF16),
        compiler_params=_params(("parallel", "parallel", "parallel"), 32), name="gqa_attn",
    )(qa, ka, vat)


def _mla_attn_kernel(q_ref, k_ref, vt_ref, o_ref):
    qt0 = _transposed(q_ref[:, 0:LANES])
    qt1 = _transposed(q_ref[:, LANES:2 * LANES])

    def score_t(keys):
        return jnp.concatenate([_dot(k_ref[keys, 0:LANES], qt0),
                                _dot(k_ref[keys, LANES:2 * LANES], qt1)], axis=1)

    _attend_pair(score_t, vt_ref, o_ref)


def _mla_attn(qm, km, vmt, b, s):
    m = qm.shape[0]
    nq = s // TQ
    return pl.pallas_call(
        _mla_attn_kernel, grid=(b, nq, 4),
        in_specs=[pl.BlockSpec((TQ, 2 * LANES), lambda bi, i, j: (bi * nq + i, j)),
                  pl.BlockSpec((s, 2 * LANES), lambda bi, i, j: (bi, j)),
                  pl.BlockSpec((None, V_ROWS, s), lambda bi, i, j: (bi, j, 0))],
        out_specs=pl.BlockSpec((TQ, LANES), lambda bi, i, j: (bi * nq + i, j)),
        out_shape=jax.ShapeDtypeStruct((m, 512), BF16),
        compiler_params=_params(("parallel", "parallel", "parallel"), 32), name="mla_attn",
    )(qm, km, vmt)


def _silu(x):
    return x * (1.0 / (1.0 + jnp.exp(-x)))


def _ple_update(h, p, ggn, wg, wp, gpost):
    gate = _dot(_rms(h, ggn).astype(BF16), wg)
    gate = 1.0 / (1.0 + jnp.exp(-gate))
    e = _dot(p.astype(BF16), wp) * gate
    return h + _rms(e, gpost)


def _l0_tail_kernel(oa_ref, om_ref, wo_ref, gmix_ref, h_ref, gpre_ref, wg_ref, wu_ref, wd_ref,
                    gpost_ref, p_ref, ggn_ref, wpg_ref, wpp_ref, gple_ref, out_ref, a_ref, acc_ref):
    j = pl.program_id(1)

    @pl.when(j == 0)
    def _():
        k = oa_ref.shape[1]
        y = _dot(oa_ref[...], wo_ref[0:k, :]) + _dot(om_ref[...], wo_ref[k:, :])
        h1 = h_ref[...] + _rms(y, gmix_ref[...])
        out_ref[...] = h1
        a_ref[...] = _rms(h1, gpre_ref[...]).astype(BF16)
        acc_ref[...] = jnp.zeros_like(acc_ref)

    a = a_ref[...]
    mid = (_silu(_dot(a, wg_ref[...])) * _dot(a, wu_ref[...])).astype(BF16)
    acc_ref[...] += _dot(mid, wd_ref[...])

    @pl.when(j == pl.num_programs(1) - 1)
    def _():
        h2 = out_ref[...] + _rms(acc_ref[...], gpost_ref[...])
        out_ref[...] = _ple_update(h2, p_ref[...], ggn_ref[...], wpg_ref[...], wpp_ref[...],
                                   gple_ref[...])


def _l0_tail(oa, om, wo, gmix, h, gpre, wg, wu, wd, gpost, p, ggn, wpg, wpp, gple):
    m = h.shape[0]
    dff = wg.shape[1]
    row = lambda i, j: (i, 0)
    const = lambda i, j: (0, 0)
    vec = lambda g: pl.BlockSpec(g.shape, const)
    once = lambda w: pl.BlockSpec(w.shape, const, pipeline_mode=pl.Buffered(1))
    return pl.pallas_call(
        _l0_tail_kernel, grid=(m // TM_FFN, dff // TF),
        in_specs=[pl.BlockSpec((TM_FFN, oa.shape[1]), row), pl.BlockSpec((TM_FFN, om.shape[1]), row),
                  once(wo), vec(gmix), pl.BlockSpec((TM_FFN, D_MODEL), row), vec(gpre),
                  pl.BlockSpec((D_MODEL, TF), lambda i, j: (0, j)),
                  pl.BlockSpec((D_MODEL, TF), lambda i, j: (0, j)),
                  pl.BlockSpec((TF, D_MODEL), lambda i, j: (j, 0)),
                  vec(gpost), pl.BlockSpec((TM_FFN, PLE_DIM), row), vec(ggn), once(wpg), once(wpp),
                  vec(gple)],
        out_specs=pl.BlockSpec((TM_FFN, D_MODEL), row),
        out_shape=jax.ShapeDtypeStruct((m, D_MODEL), F32),
        scratch_shapes=[pltpu.VMEM((TM_FFN, D_MODEL), BF16), pltpu.VMEM((TM_FFN, D_MODEL), F32)],
        compiler_params=_params(("parallel", "arbitrary"), 56), name="l0_tail",
    )(oa, om, wo, gmix, h, gpre, wg, wu, wd, gpost, p, ggn, wpg, wpp, gple)


def _l1_qkv_kernel(h_ref, gpre_ref, w_ref, c_ref, s_ref, out0_ref, out1_ref, out2_ref, stage_ref):
    a = _rms(h_ref[...], gpre_ref[...]).astype(BF16)
    tm = a.shape[0]
    outs = (out0_ref, out1_ref, out2_ref)
    for blk in range(DIL_IN_WIDTH // 512):
        gi, kind = divmod(blk, 3)
        dil = DIL_CONFIGS[gi][1]
        z = _dot(a, w_ref[:, blk * 512:(blk + 1) * 512])
        if kind < 2:
            c, s = c_ref[kind], s_ref[kind]
            z = jnp.concatenate(
                [_rotate(z[:, sub * LANES:(sub + 1) * LANES], c, s, PARTIAL_ROPE_DIM // 2)
                 for sub in range(4)], axis=1)
        cols = slice(kind * 512, (kind + 1) * 512)
        if dil == 1:
            outs[gi][0, :, cols] = z.astype(BF16)
            continue
        for sub in range(4):
            stage = stage_ref.at[(blk % 2) * 4 + sub]
            stage[...] = z[:, sub * LANES:(sub + 1) * LANES]
            lanes = slice(kind * 512 + sub * LANES, kind * 512 + (sub + 1) * LANES)
            for r in range(dil):
                outs[gi][r, :, lanes] = stage[pl.ds(r, tm // dil, stride=dil), :].astype(BF16)


def _l1_qkv(h, gpre, w, ctab, stab, b, s):
    m = h.shape[0]
    nt = s // TM
    const = lambda i: (0, 0)
    out_specs, out_shape = [], []
    for _, dil in DIL_CONFIGS:
        out_specs.append(pl.BlockSpec((None, dil, TM // dil, 1536), lambda i: (i // nt, 0, i % nt, 0)))
        out_shape.append(jax.ShapeDtypeStruct((b, dil, s // dil, 1536), BF16))
    return pl.pallas_call(
        _l1_qkv_kernel, grid=(m // TM,),
        in_specs=[pl.BlockSpec((TM, D_MODEL), lambda i: (i, 0)),
                  pl.BlockSpec(gpre.shape, const),
                  pl.BlockSpec(w.shape, const, pipeline_mode=pl.Buffered(1)),
                  pl.BlockSpec((2, TM, LANES), lambda i: (0, i % nt, 0)),
                  pl.BlockSpec((2, TM, LANES), lambda i: (0, i % nt, 0))],
        out_specs=out_specs, out_shape=out_shape,
        scratch_shapes=[pltpu.VMEM((8, TM, LANES), F32)],
        compiler_params=_params(("parallel",), 48), name="l1_qkv",
    )(h, gpre, w, ctab, stab)


def _dil_attn_kernel(q0_ref, k0_ref, v0_ref, q1_ref, k1_ref, v1_ref, q2_ref, k2_ref, v2_ref, o_ref,
                     ra_ref, rl_ref, rm_ref, acc_ref, sum_ref, max_ref):
    lo = _lane_iota((DIL_QB, LANES)) < HEAD_DIM
    groups = ((q0_ref, k0_ref, v0_ref), (q1_ref, k1_ref, v1_ref), (q2_ref, k2_ref, v2_ref))
    for gi, (q_ref, k_ref, v_ref) in enumerate(groups):
        dil, length = q_ref.shape[0], q_ref.shape[1]
        width = min(DIL_QB + 2 * DIL_HALF, length)
        nblk = length // DIL_QB
        first = gi == 0
        ta, tl, tm_ = (acc_ref, sum_ref, max_ref) if first else (ra_ref, rl_ref, rm_ref)
        row = lax.broadcasted_iota(jnp.int32, (2 * DIL_QB, width), 0) % DIL_QB
        col = lax.broadcasted_iota(jnp.int32, (2 * DIL_QB, width), 1)
        rel = row - col
        ones = jnp.ones((width, LANES), BF16)

        def body(it, carry, length=length, width=width, nblk=nblk, ta=ta, tl=tl, tm_=tm_, rel=rel,
                 ones=ones, q_ref=q_ref, k_ref=k_ref, v_ref=v_ref):
            r = it // nblk
            i = it % nblk
            q0 = pl.multiple_of(i * DIL_QB, DIL_QB)
            start = pl.multiple_of(jnp.clip(q0 - DIL_HALF, 0, length - width), DIL_HALF)
            qrow = pl.multiple_of(r * length + q0, DIL_HALF)
            q = q_ref[r, pl.ds(q0, DIL_QB), :]
            kw = k_ref[r, pl.ds(start, width), :]
            vw = jnp.concatenate([v_ref[r, pl.ds(start, width), :], ones], axis=1)
            zero = jnp.zeros_like(q)
            q2 = jnp.concatenate([jnp.where(lo, q, zero), jnp.where(lo, zero, q)], axis=0)
            sc = lax.dot_general(q2, kw, (((1,), (1,)), ((), ())), preferred_element_type=F32)
            sc = jnp.where(jnp.abs(rel + (q0 - start)) <= DIL_HALF, sc, NEG_INF)
            m = jnp.max(sc, axis=-1, keepdims=True)
            p = jnp.exp2(sc - m).astype(BF16)
            o2 = _dot(p, vw)
            ta[pl.ds(qrow, DIL_QB), :] = jnp.where(lo, o2[:DIL_QB, 0:LANES], o2[DIL_QB:, 0:LANES])
            tl[pl.ds(qrow, DIL_QB), :] = jnp.where(lo, o2[:DIL_QB, LANES:], o2[DIL_QB:, LANES:])
            tm_[pl.ds(qrow, DIL_QB), :] = jnp.where(lo, m[:DIL_QB], m[DIL_QB:])
            return carry

        lax.fori_loop(0, dil * nblk, body, 0, unroll=4)
        if first:
            continue
        for r in range(dil):
            rows = pl.ds(r, length, stride=dil)
            blk = slice(r * length, (r + 1) * length)
            m_old, m_blk = max_ref[rows, :], rm_ref[blk, :]
            m_new = jnp.maximum(m_old, m_blk)
            w_old, w_blk = jnp.exp2(m_old - m_new), jnp.exp2(m_blk - m_new)
            acc_ref[rows, :] = acc_ref[rows, :] * w_old + ra_ref[blk, :] * w_blk
            sum_ref[rows, :] = sum_ref[rows, :] * w_old + rl_ref[blk, :] * w_blk
            max_ref[rows, :] = m_new
    o_ref[...] = (acc_ref[...] / sum_ref[...]).astype(BF16)


def _dil_attn(qkvs, b, s):
    in_specs, args = [], []
    for x in qkvs:
        for j in range(3):
            in_specs.append(pl.BlockSpec((None,) + x.shape[1:3] + (LANES,),
                                         lambda bi, hp, j=j: (bi, 0, 0, 4 * j + hp)))
            args.append(x)
    stat = pltpu.VMEM((s, LANES), F32)
    return pl.pallas_call(
        _dil_attn_kernel, grid=(b, 4), in_specs=in_specs,
        out_specs=pl.BlockSpec((s, LANES), lambda bi, hp: (bi, hp)),
        out_shape=jax.ShapeDtypeStruct((b * s, 512), BF16),
        scratch_shapes=[stat] * 6,
        compiler_params=_params(("parallel", "parallel"), 40), name="dil_attn",
    )(*args)


def _l1_router_kernel(o_ref, wo_ref, gmix_ref, h_ref, gpre_ref, wr_ref,
                      h1_ref, a_ref, idx_ref, gate_ref, cnt_ref, carry_ref):
    @pl.when(pl.program_id(0) == 0)
    def _():
        carry_ref[...] = jnp.zeros_like(carry_ref)

    h1 = h_ref[...] + _rms(_dot(o_ref[...], wo_ref[...]), gmix_ref[...])
    h1_ref[...] = h1
    a = _rms(h1, gpre_ref[...]).astype(BF16)
    bits = lax.bitcast_convert_type(a.astype(F32), jnp.uint32)
    half = bits.shape[1] // 2
    a_ref[...] = (bits[:, :half] >> 16) | (bits[:, half:] & jnp.uint32(0xFFFF0000))
    logits = _dot(a, wr_ref[...])
    tm = logits.shape[0]
    lane = _lane_iota(logits.shape)
    lanef = lane.astype(F32)
    lg = jnp.where(lane < N_EXPERTS, logits, -jnp.inf)
    m1 = jnp.max(lg, axis=-1, keepdims=True)
    i1 = jnp.min(jnp.where(lg == m1, lanef, float(LANES)), axis=-1, keepdims=True)
    lg2 = jnp.where(lanef == i1, -jnp.inf, lg)
    m2 = jnp.max(lg2, axis=-1, keepdims=True)
    i2 = jnp.min(jnp.where(lg2 == m2, lanef, float(LANES)), axis=-1, keepdims=True)
    t = jnp.exp(m2 - m1)
    g1 = 1.0 / (1.0 + t)
    g2 = t / (1.0 + t)
    hit1 = lanef == i1
    hit2 = lanef == i2
    onehot = jnp.where(hit1 | hit2, 1.0, 0.0)
    r = lax.broadcasted_iota(jnp.int32, (tm, tm), 0)
    c = lax.broadcasted_iota(jnp.int32, (tm, tm), 1)
    tri = jnp.where(c < r, 1.0, 0.0).astype(BF16)
    before = _dot(tri, onehot.astype(BF16)) + carry_ref[...]
    rank1 = jnp.sum(jnp.where(hit1, before, 0.0), axis=-1, keepdims=True)
    rank2 = jnp.sum(jnp.where(hit2, before, 0.0), axis=-1, keepdims=True)
    info = jnp.where(lane == 0, i1, jnp.where(lane == 1, i2, jnp.where(lane == 2, rank1, rank2)))
    idx_ref[...] = info.astype(jnp.int32)
    gate_ref[...] = jnp.where(lane == 0, g1, g2)
    carry_ref[...] += jnp.sum(onehot, axis=0, keepdims=True)
    cnt_ref[...] = carry_ref[...]


def _l1_router(o, wo, gmix, h, gpre, wr):
    m = h.shape[0]
    row = lambda i: (i, 0)
    const = lambda i: (0, 0)
    full = lambda x: pl.BlockSpec(x.shape, const)
    return pl.pallas_call(
        _l1_router_kernel, grid=(m // TM,),
        in_specs=[pl.BlockSpec((TM, o.shape[1]), row), full(wo), full(gmix),
                  pl.BlockSpec((TM, D_MODEL), row), full(gpre), full(wr)],
        out_specs=[pl.BlockSpec((TM, D_MODEL), row), pl.BlockSpec((TM, D_MODEL // 2), row),
                   pl.BlockSpec((TM, LANES), row), pl.BlockSpec((TM, LANES), row),
                   pl.BlockSpec((1, LANES), const)],
        out_shape=[jax.ShapeDtypeStruct((m, D_MODEL), F32),
                   jax.ShapeDtypeStruct((m, D_MODEL // 2), jnp.uint32),
                   jax.ShapeDtypeStruct((m, LANES), jnp.int32),
                   jax.ShapeDtypeStruct((m, LANES), F32),
                   jax.ShapeDtypeStruct((1, LANES), F32)],
        scratch_shapes=[pltpu.VMEM((1, LANES), F32)],
        compiler_params=_params(("arbitrary",), 32), name="l1_router",
    )(o, wo, gmix, h, gpre, wr)


def _dispatch_kernel(dest_ref, a_ref, xs_in_ref, xs_ref, sem):
    del xs_in_ref
    step = pl.program_id(0)
    slot = step % 2
    n_copies = 2 * DISPATCH_TOKENS

    def issue(j8, carry):
        for u in range(8):
            j = j8 * 8 + u
            tok = step * DISPATCH_TOKENS + lax.shift_right_logical(j, 1)
            pltpu.make_async_copy(a_ref.at[pl.ds(tok, 1)], xs_ref.at[pl.ds(dest_ref[j], 1)],
                                  sem.at[slot]).start(priority=u % 2)
        return carry

    lax.fori_loop(0, n_copies // 8, issue, 0)

    def wait_step(s):
        pltpu.make_async_copy(a_ref.at[pl.ds(0, n_copies)], xs_ref.at[pl.ds(0, n_copies)],
                              sem.at[s]).wait()

    @pl.when(step > 0)
    def _():
        wait_step(1 - slot)

    @pl.when(step == pl.num_programs(0) - 1)
    def _():
        wait_step(slot)


def _dispatch(dest, a, n_slots):
    n = a.shape[0]
    grid_spec = pl.GridSpec(
        grid=(n // DISPATCH_TOKENS,),
        in_specs=[pl.BlockSpec((2 * DISPATCH_TOKENS,), lambda i: (i,), memory_space=pltpu.SMEM),
                  pl.BlockSpec(memory_space=pl.ANY), pl.BlockSpec(memory_space=pl.ANY)],
        out_specs=pl.BlockSpec(memory_space=pl.ANY),
        scratch_shapes=[pltpu.SemaphoreType.DMA((2,))])
    return pl.pallas_call(
        _dispatch_kernel, grid_spec=grid_spec,
        out_shape=jax.ShapeDtypeStruct((n_slots, a.shape[1]), a.dtype),
        input_output_aliases={2: 0},
        compiler_params=_params(("arbitrary",), 16), name="moe_dispatch",
    )(dest, a, jnp.zeros((n_slots, a.shape[1]), a.dtype))


def _expert_kernel(te_ref, tv_ref, x_ref, wg_ref, wu_ref, wd_ref, y_ref, acc_ref, xb_ref):
    i = pl.program_id(0)
    j = pl.program_id(1)
    last = pl.num_programs(1) - 1
    valid = tv_ref[i] > 0

    @pl.when(valid & (j == 0))
    def _():
        acc_ref[...] = jnp.zeros_like(acc_ref)
        w = x_ref[...]
        half = w.shape[1]
        xb_ref[:, :half] = lax.bitcast_convert_type(w << 16, F32).astype(BF16)
        xb_ref[:, half:] = lax.bitcast_convert_type(w & jnp.uint32(0xFFFF0000), F32).astype(BF16)

    @pl.when(valid)
    def _():
        x = xb_ref[...]
        gate = _dot(x, wg_ref[...].astype(BF16))
        up = _dot(x, wu_ref[...].astype(BF16))
        acc_ref[...] += _dot((_silu(gate) * up).astype(BF16), wd_ref[...].astype(BF16))

    @pl.when(valid & (j == last))
    def _():
        y_ref[...] = acc_ref[...].astype(BF16)

    @pl.when(jnp.logical_not(valid) & (j == last))
    def _():
        y_ref[...] = jnp.zeros_like(y_ref)


def _experts(tile_e, tile_valid, xs, wg, wu, wd):
    n_slots = xs.shape[0]
    dff = wg.shape[2]
    grid_spec = pltpu.PrefetchScalarGridSpec(
        num_scalar_prefetch=2, grid=(n_slots // TM_MOE, dff // TF),
        in_specs=[pl.BlockSpec((TM_MOE, xs.shape[1]), lambda i, j, te, tv: (i, 0)),
                  pl.BlockSpec((None, D_MODEL, TF), lambda i, j, te, tv: (te[i], 0, j * tv[i])),
                  pl.BlockSpec((None, D_MODEL, TF), lambda i, j, te, tv: (te[i], 0, j * tv[i])),
                  pl.BlockSpec((None, TF, D_MODEL), lambda i, j, te, tv: (te[i], j * tv[i], 0))],
        out_specs=pl.BlockSpec((TM_MOE, D_MODEL), lambda i, j, te, tv: (i, 0)),
        scratch_shapes=[pltpu.VMEM((TM_MOE, D_MODEL), F32), pltpu.VMEM((TM_MOE, D_MODEL), BF16)])
    return pl.pallas_call(
        _expert_kernel, grid_spec=grid_spec,
        out_shape=jax.ShapeDtypeStruct((n_slots, D_MODEL), BF16),
        compiler_params=_params(("parallel", "arbitrary"), 48), name="moe_experts",
    )(tile_e, tile_valid, xs, wg, wu, wd)


def _combine_ple_kernel(y0_ref, y1_ref, gate_ref, g_ref, h_ref, p_ref, ggn_ref, wpg_ref, wpp_ref,
                        gple_ref, out_ref):
    gate = gate_ref[...]
    f = y0_ref[...].astype(F32) * gate[:, 0:1] + y1_ref[...].astype(F32) * gate[:, 1:2]
    h2 = h_ref[...] + _rms(f, g_ref[...])
    out_ref[...] = _ple_update(h2, p_ref[...], ggn_ref[...], wpg_ref[...], wpp_ref[...], gple_ref[...])


def _combine_ple(y0, y1, gate, g, h, p, ggn, wpg, wpp, gple):
    m = h.shape[0]
    row = lambda i: (i, 0)
    const = lambda i: (0, 0)
    full = lambda x: pl.BlockSpec(x.shape, const)
    act = pl.BlockSpec((TM, D_MODEL), row)
    return pl.pallas_call(
        _combine_ple_kernel, grid=(m // TM,),
        in_specs=[act, act, pl.BlockSpec((TM, LANES), row), full(g), act,
                  pl.BlockSpec((TM, PLE_DIM), row), full(ggn), full(wpg), full(wpp), full(gple)],
        out_specs=act,
        out_shape=jax.ShapeDtypeStruct((m, D_MODEL), F32),
        compiler_params=_params(("parallel",), 40), name="combine_ple",
    )(y0, y1, gate, g, h, p, ggn, wpg, wpp, gple)


def _moe_plan(idx, cnt, n):
    counts = cnt[0, :N_EXPERTS].astype(jnp.int32)
    padded = (counts + TM_MOE - 1) // TM_MOE * TM_MOE
    end_pad = jnp.cumsum(padded)
    start_pad = end_pad - padded
    experts = jnp.arange(N_EXPERTS, dtype=jnp.int32)
    start = jnp.sum(jnp.where(idx[:, 0:2, None] == experts, start_pad, 0), axis=-1)
    dest = start + idx[:, 2:4]
    n_slots = n * TOP_K + N_EXPERTS * TM_MOE
    tile_start = jnp.arange(n_slots // TM_MOE, dtype=jnp.int32) * TM_MOE
    tile_e = jnp.minimum(jnp.sum(tile_start[:, None] >= end_pad[None, :], axis=1),
                         N_EXPERTS - 1).astype(jnp.int32)
    tile_valid = (tile_start < end_pad[-1]).astype(jnp.int32)
    return dest, n_slots, tile_e, tile_valid


def _rotary_tables(pos, dim, theta):
    exponent = jnp.arange(0, dim, 2, dtype=F32) / dim
    inv_freq = jnp.power(jnp.float32(theta), -exponent)
    ang = pos.astype(F32)[:, None] * inv_freq[None, :]
    return jnp.cos(ang), jnp.sin(ang)


def _tables(s):
    t = jnp.arange(s, dtype=jnp.int32)
    cr, sr = _rotary_tables(t // GRID_W, HEAD_DIM // 2, AXIAL_THETA)
    cc, sc = _rotary_tables(t % GRID_W, HEAD_DIM // 2, AXIAL_THETA)
    ca = jnp.tile(jnp.concatenate([cr, cr, cc, cc], axis=1), (1, 2))
    sa = jnp.tile(jnp.concatenate([-sr, sr, -sc, sc], axis=1), (1, 2))
    cm_, sm_ = _rotary_tables(t, MLA_ROPE_DIM, MLA_ROPE_THETA)
    one = jnp.ones((s, 1), F32)
    zero = jnp.zeros((s, 1), F32)
    cm = jnp.concatenate([jnp.tile(one, (1, 64)), cm_, cm_, jnp.tile(one, (1, 32))], axis=1)
    sm = jnp.concatenate([jnp.tile(zero, (1, 64)), -sm_, sm_, jnp.tile(zero, (1, 32))], axis=1)
    ckr = jnp.concatenate([cm_, cm_, jnp.tile(zero, (1, 96))], axis=1)
    skr = jnp.concatenate([-sm_, sm_, jnp.tile(zero, (1, 96))], axis=1)
    cp_, sp_ = _rotary_tables(t, PARTIAL_ROPE_DIM, ROPE_THETA)
    cp = jnp.tile(jnp.concatenate([cp_, cp_, jnp.tile(one, (1, 48))], axis=1), (1, 2))
    sp = jnp.tile(jnp.concatenate([-sp_, sp_, jnp.tile(zero, (1, 48))], axis=1), (1, 2))
    scale = HEAD_DIM ** -0.5 * LOG2_E
    ctab = jnp.stack([cp * scale, cp])
    stab = jnp.stack([sp * scale, sp])
    return ca, sa, cm, sm, ckr, skr, ctab, stab


_GQA_PERM = (0, 4, 1, 5, 2, 6, 3, 7)


def _l0_weights(w_in, w_uq, w_ukv, w_out):
    d = w_in.shape[0]
    q = w_in[:, :512].reshape(d, 8, 64)[:, _GQA_PERM, :].reshape(d, 512)
    win = jnp.concatenate([q, w_in[:, 512:], jnp.zeros((d, AB_IN_PAD - w_in.shape[1]), w_in.dtype)],
                          axis=1).astype(BF16)
    uq = w_uq.reshape(MLA_Q_RANK, MLA_HEADS, MLA_NOPE_DIM + MLA_ROPE_DIM)
    wuq = jnp.pad(uq, ((0, 0), (0, 0), (0, LANES - uq.shape[2]))).reshape(MLA_Q_RANK, -1).astype(BF16)
    ukv = w_ukv.reshape(MLA_KV_RANK, MLA_HEADS, MLA_NOPE_DIM + MLA_V_DIM)
    wk = jnp.pad(ukv[:, :, :MLA_NOPE_DIM], ((0, 0), (0, 0), (0, LANES - MLA_NOPE_DIM)))
    wk = wk.reshape(MLA_KV_RANK, -1).astype(BF16)
    wv = ukv[:, :, MLA_NOPE_DIM:].reshape(MLA_KV_RANK, -1).astype(BF16)
    src = jnp.arange(LANES)[:, None]
    dst = jnp.arange(MLA_HEADS * LANES)[None, :]
    wp = ((src < MLA_ROPE_DIM) & (dst % LANES == src + MLA_NOPE_DIM)).astype(BF16)
    oa = w_out[:512].reshape(8, 64, -1)[_GQA_PERM, :, :].reshape(512, -1)
    wout = jnp.concatenate([oa, w_out[512:]], axis=0).astype(BF16)
    return win, wuq, wk, wp, wv, wout


def kernel(x, p, mix_pre_g, mix_post_g, ffn_pre_g, ffn_post_g, ple_w_proj, ple_gate_norm_g, ple_w_gate, ple_post_g, ab_w_in, gqa_q_norm_g, gqa_k_norm_g, mla_q_norm_g, mla_w_uq, mla_kv_norm_g, mla_w_ukv, ab_w_out, ffn_w_gate, ffn_w_up, ffn_w_down, dil_w_qkv, dil_w_out, moe_w_router, moe_w_gate, moe_w_up, moe_w_down):
    b, s, d = x.shape
    m = b * s
    row = lambda g: g.reshape(1, -1).astype(F32)
    ca, sa, cm, sm, ckr, skr, ctab, stab = _tables(s)
    h = x.reshape(m, d)
    pf = p.reshape(p.shape[0], m, PLE_DIM)

    win, wuq, wk, wp, wv, wout = _l0_weights(ab_w_in[0], mla_w_uq[0], mla_w_ukv[0], ab_w_out[0])
    gq = jnp.tile(row(gqa_q_norm_g[0]), (1, 2))
    gk = jnp.tile(row(gqa_k_norm_g[0]), (1, 2))
    qa, ka, vat, qm, km, vmt = _l0_in(h, row(mix_pre_g[0]), win, gq, gk, ca, sa,
                                      row(mla_q_norm_g[0]), wuq, cm, sm,
                                      row(mla_kv_norm_g[0]), wk, wp, wv, ckr, skr, b, s)
    oa = _gqa_attn(qa, ka, vat, b, s)
    om = _mla_attn(qm, km, vmt, b, s)
    h = _l0_tail(oa, om, wout, row(mix_post_g[0]), h, row(ffn_pre_g[0]),
                 ffn_w_gate[0].astype(BF16), ffn_w_up[0].astype(BF16), ffn_w_down[0].astype(BF16),
                 row(ffn_post_g[0]), pf[0], row(ple_gate_norm_g[0]), ple_w_gate[0].astype(BF16),
                 ple_w_proj[0].astype(BF16), row(ple_post_g[0]))

    qkvs = _l1_qkv(h, row(mix_pre_g[1]), dil_w_qkv[0].astype(BF16), ctab, stab, b, s)
    o = _dil_attn(qkvs, b, s)
    wr = jnp.pad(moe_w_router[0].astype(BF16), ((0, 0), (0, LANES - N_EXPERTS)))
    h, a, idx, gate, cnt = _l1_router(o, dil_w_out[0].astype(BF16), row(mix_post_g[1]), h,
                                      row(ffn_pre_g[1]), wr)
    dest, n_slots, tile_e, tile_valid = _moe_plan(idx, cnt, m)
    xs = _dispatch(dest.reshape(-1), a, n_slots)
    ys = _experts(tile_e, tile_valid, xs, moe_w_gate[0], moe_w_up[0], moe_w_down[0])
    h = _combine_ple(ys[dest[:, 0]], ys[dest[:, 1]], gate, row(ffn_post_g[1]), h, pf[1],
                     row(ple_gate_norm_g[1]), ple_w_gate[1].astype(BF16), ple_w_proj[1].astype(BF16),
                     row(ple_post_g[1]))
    return h.reshape(b, s, d)
```

```python
import functools

import jax
import jax.numpy as jnp
from jax import lax
from jax.experimental import pallas as pl
from jax.experimental.pallas import tpu as pltpu

F32 = jnp.float32
BF16 = jnp.bfloat16

D_MODEL = 1024
GRID_W = 64
HEAD_DIM = 64
NORM_EPS = 1e-6
NEG_INF = -1e30
ROPE_THETA = 500000.0
PARTIAL_ROPE_DIM = HEAD_DIM // 4
GQA_Q_HEADS = 8
GQA_KV_HEADS = 2
AXIAL_THETA = 10000.0
MLA_HEADS = 8
MLA_Q_RANK = 256
MLA_KV_RANK = 128
MLA_NOPE_DIM = 64
MLA_ROPE_DIM = 32
MLA_V_DIM = 64
MLA_ROPE_THETA = 10000.0
DIL_CONFIGS = ((128, 1), (512, 4), (2048, 16))
DIL_GROUPS = len(DIL_CONFIGS)
DIL_HEADS = 8
DIL_HALF = 64
DIL_QB = 128
LOG2_E = 1.4426950408889634
DIL_IN_WIDTH = DIL_GROUPS * 3 * DIL_HEADS * HEAD_DIM
D_FF = 3584
N_EXPERTS = 8
TOP_K = 2
PLE_DIM = 256

LANES = 128
VMEM_BYTES = 64 * 1024 * 1024
MIB = 1024 * 1024

TM = 512
TQ = 1024
KEY_CHUNK = 256
V_ONES = 16
V_ROWS = LANES + V_ONES
TF = 512
TM_FFN = 1024
TAIL_ROWS = 256
TM_MOE = 1024
AB_IN_PAD = 1280
MOE_CHUNKS = (1, 3, 4, 4)


def _params(semantics, vmem_mib):
    return pltpu.CompilerParams(dimension_semantics=semantics,
                                vmem_limit_bytes=vmem_mib * MIB)


def _dot(a, b):
    return jnp.dot(a, b, preferred_element_type=F32)


def _rms(x, g):
    return x * lax.rsqrt(jnp.mean(x * x, axis=-1, keepdims=True) + NORM_EPS) * g


def _lane_iota(shape):
    return lax.broadcasted_iota(jnp.int32, shape, len(shape) - 1)


def _swap_halves(x, k):
    w = x.shape[-1]
    fwd = pltpu.roll(x, w - k, 1)
    bwd = pltpu.roll(x, k, 1)
    return jnp.where((_lane_iota(x.shape) % (2 * k)) < k, fwd, bwd)


def _rotate(x, c, s, k):
    return x * c + _swap_halves(x, k) * s


def _head_rms(x, g):
    lo = _lane_iota(x.shape) < HEAD_DIM
    x2 = x * x
    s_all = jnp.sum(x2, axis=-1, keepdims=True)
    s_lo = jnp.sum(jnp.where(lo, x2, 0.0), axis=-1, keepdims=True)
    ms = jnp.where(lo, s_lo, s_all - s_lo) * (1.0 / HEAD_DIM)
    return x * lax.rsqrt(ms + NORM_EPS) * g


def _l0_in_kernel(h_ref, gpre_ref, win_ref, gq_ref, gk_ref, ca_ref, sa_ref,
                  gmq_ref, wuq_ref, cm_ref, sm_ref, gmkv_ref, wk_ref, wp_ref, wv_ref,
                  ckr_ref, skr_ref,
                  qa_ref, ka_ref, vat_ref, qm_ref, km_ref, vmt_ref):
    a = _rms(h_ref[...], gpre_ref[...]).astype(BF16)
    z = _dot(a, win_ref[...])
    ca, sa = ca_ref[...], sa_ref[...]
    for j in range(4):
        xq = _head_rms(z[:, j * LANES:(j + 1) * LANES], gq_ref[...])
        qa_ref[:, j * LANES:(j + 1) * LANES] = (
            _rotate(xq, ca, sa, 16) * (HEAD_DIM ** -0.5 * LOG2_E)).astype(BF16)
    ka_ref[...] = _rotate(_head_rms(z[:, 512:640], gk_ref[...]), ca, sa, 16).astype(BF16)
    ones = jnp.ones((V_ONES, z.shape[0]), BF16)
    vat_ref[0:LANES, :] = z[:, 640:768].T.astype(BF16)
    vat_ref[LANES:, :] = ones
    cq = _rms(z[:, 768:1024], gmq_ref[...]).astype(BF16)
    qm = _dot(cq, wuq_ref[...])
    cm, sm = cm_ref[...], sm_ref[...]
    scale_m = (MLA_NOPE_DIM + MLA_ROPE_DIM) ** -0.5 * LOG2_E
    for hb in range(MLA_HEADS):
        x = qm[:, hb * LANES:(hb + 1) * LANES]
        qm_ref[:, hb * LANES:(hb + 1) * LANES] = (_rotate(x, cm, sm, 16) * scale_m).astype(BF16)
    ckv = _rms(z[:, 1024:1152], gmkv_ref[...]).astype(BF16)
    kr = _rotate(z[:, 1152:1280], ckr_ref[...], skr_ref[...], 16).astype(BF16)
    km = _dot(ckv, wk_ref[...]) + _dot(kr, wp_ref[...])
    km_ref[...] = km.astype(BF16)
    vm = _dot(ckv, wv_ref[...])
    for j in range(4):
        vmt_ref[j * V_ROWS:j * V_ROWS + LANES, :] = vm[:, j * LANES:(j + 1) * LANES].T.astype(BF16)
        vmt_ref[j * V_ROWS + LANES:(j + 1) * V_ROWS, :] = ones


def _l0_in(h, gpre, win, gq, gk, ca, sa, gmq, wuq, cm, sm, gmkv, wk, wp, wv, ckr, skr, b, s):
    m = h.shape[0]
    nt = s // TM
    row = lambda i: (i, 0)
    const = lambda i: (0, 0)
    tab = lambda i: (i % nt, 0)
    full = lambda a: pl.BlockSpec(a.shape, const)
    in_specs = [pl.BlockSpec((TM, D_MODEL), row), full(gpre), full(win), full(gq), full(gk),
                pl.BlockSpec((TM, LANES), tab), pl.BlockSpec((TM, LANES), tab),
                full(gmq), full(wuq), pl.BlockSpec((TM, LANES), tab), pl.BlockSpec((TM, LANES), tab),
                full(gmkv), full(wk), full(wp), full(wv),
                pl.BlockSpec((TM, LANES), tab), pl.BlockSpec((TM, LANES), tab)]
    tr = lambda i: (i // nt, 0, i % nt)
    out_specs = [pl.BlockSpec((TM, 512), row),
                 pl.BlockSpec((TM, LANES), row),
                 pl.BlockSpec((None, V_ROWS, TM), tr),
                 pl.BlockSpec((TM, 1024), row),
                 pl.BlockSpec((TM, 1024), row),
                 pl.BlockSpec((None, 4 * V_ROWS, TM), tr)]
    out_shape = [jax.ShapeDtypeStruct((m, 512), BF16),
                 jax.ShapeDtypeStruct((m, LANES), BF16),
                 jax.ShapeDtypeStruct((b, V_ROWS, s), BF16),
                 jax.ShapeDtypeStruct((m, 1024), BF16),
                 jax.ShapeDtypeStruct((m, 1024), BF16),
                 jax.ShapeDtypeStruct((b, 4 * V_ROWS, s), BF16)]
    return pl.pallas_call(
        _l0_in_kernel, grid=(m // TM,), in_specs=in_specs, out_specs=out_specs,
        out_shape=out_shape, compiler_params=_params(("parallel",), 48), name="l0_in",
    )(h, gpre, win, gq, gk, ca, sa, gmq, wuq, cm, sm, gmkv, wk, wp, wv, ckr, skr)


def _attend_pair(score_t, vt_ref, o_ref):
    n_chunks = vt_ref.shape[1] // KEY_CHUNK
    tq = o_ref.shape[0]
    keys = [slice(c * KEY_CHUNK, (c + 1) * KEY_CHUNK) for c in range(n_chunks)]
    parts = []
    s_next = score_t(keys[0])
    for c in range(n_chunks):
        s_t = s_next
        if c + 1 < n_chunks:
            s_next = score_t(keys[c + 1])
        m = jnp.max(s_t, axis=0, keepdims=True)
        parts.append((m, _dot(vt_ref[:, keys[c]], jnp.exp2(s_t - m).astype(BF16))))
    m_all = functools.reduce(jnp.maximum, [m for m, _ in parts])
    o_t = sum(o_c * jnp.exp2(m_c - m_all) for m_c, o_c in parts)
    o_t = o_t[0:LANES, :] / o_t[LANES:LANES + 1, :]
    first = lax.broadcasted_iota(jnp.int32, (LANES, tq), 0) < HEAD_DIM
    o_ref[...] = jnp.where(first, o_t[:, 0:tq], o_t[:, tq:]).T.astype(BF16)


def _transposed(q):
    return q.astype(F32).T.astype(BF16)


def _gqa_attn_kernel(q_ref, k_ref, vt_ref, o_ref):
    qt = _transposed(q_ref[...])
    first = lax.broadcasted_iota(jnp.int32, qt.shape, 0) < HEAD_DIM
    zero = jnp.zeros_like(qt)
    w = jnp.concatenate([jnp.where(first, qt, zero), jnp.where(first, zero, qt)], axis=1)
    _attend_pair(lambda keys: _dot(k_ref[keys, :], w), vt_ref, o_ref)


def _gqa_attn(qa, ka, vat, b, s):
    m = qa.shape[0]
    nq = s // TQ
    return pl.pallas_call(
        _gqa_attn_kernel, grid=(b, nq, 4),
        in_specs=[pl.BlockSpec((TQ, LANES), lambda bi, i, j: (bi * nq + i, j)),
                  pl.BlockSpec((s, LANES), lambda bi, i, j: (bi, 0)),
                  pl.BlockSpec((None, V_ROWS, s), lambda bi, i, j: (bi, 0, 0))],
        out_specs=pl.BlockSpec((TQ, LANES), lambda bi, i, j: (bi * nq + i, j)),
        out_shape=jax.ShapeDtypeStruct((m, 512), BF16),
        compiler_params=_params(("parallel", "parallel", "parallel"), 32), name="gqa_attn",
    )(qa, ka, vat)


def _mla_attn_kernel(q_ref, k_ref, vt_ref, o_ref):
    qt0 = _transposed(q_ref[:, 0:LANES])
    qt1 = _transposed(q_ref[:, LANES:2 * LANES])

    def score_t(keys):
        return jnp.concatenate([_dot(k_ref[keys, 0:LANES], qt0),
                                _dot(k_ref[keys, LANES:2 * LANES], qt1)], axis=1)

    _attend_pair(score_t, vt_ref, o_ref)


def _mla_attn(qm, km, vmt, b, s):
    m = qm.shape[0]
    nq = s // TQ
    return pl.pallas_call(
        _mla_attn_kernel, grid=(b, nq, 4),
        in_specs=[pl.BlockSpec((TQ, 2 * LANES), lambda bi, i, j: (bi * nq + i, j)),
                  pl.BlockSpec((s, 2 * LANES), lambda bi, i, j: (bi, j)),
                  pl.BlockSpec((None, V_ROWS, s), lambda bi, i, j: (bi, j, 0))],
        out_specs=pl.BlockSpec((TQ, LANES), lambda bi, i, j: (bi * nq + i, j)),
        out_shape=jax.ShapeDtypeStruct((m, 512), BF16),
        compiler_params=_params(("parallel", "parallel", "parallel"), 32), name="mla_attn",
    )(qm, km, vmt)


def _silu(x):
    return x * (1.0 / (1.0 + jnp.exp(-x)))


def _ple_update(h, p, ggn, wg, wp, gpost):
    gate = _dot(_rms(h, ggn).astype(BF16), wg)
    gate = 1.0 / (1.0 + jnp.exp(-gate))
    e = _dot(p.astype(BF16), wp) * gate
    return h + _rms(e, gpost)


def _l0_tail_kernel(oa_ref, om_ref, wo_ref, gmix_ref, h_ref, gpre_ref, wg_ref, wu_ref, wd_ref,
                    gpost_ref, p_ref, ggn_ref, wpg_ref, wpp_ref, gple_ref, out_ref, a_ref, acc_ref):
    j = pl.program_id(1)

    chunks = [slice(r, r + TAIL_ROWS) for r in range(0, out_ref.shape[0], TAIL_ROWS)]

    @pl.when(j == 0)
    def _():
        k = oa_ref.shape[1]
        for rows in chunks:
            y = _dot(oa_ref[rows, :], wo_ref[0:k, :]) + _dot(om_ref[rows, :], wo_ref[k:, :])
            h1 = h_ref[rows, :] + _rms(y, gmix_ref[...])
            out_ref[rows, :] = h1
            a_ref[rows, :] = _rms(h1, gpre_ref[...]).astype(BF16)
        acc_ref[...] = jnp.zeros_like(acc_ref)

    a = a_ref[...]
    mid = (_silu(_dot(a, wg_ref[...])) * _dot(a, wu_ref[...])).astype(BF16)
    acc_ref[...] += _dot(mid, wd_ref[...])

    @pl.when(j == pl.num_programs(1) - 1)
    def _():
        for rows in chunks:
            h2 = out_ref[rows, :] + _rms(acc_ref[rows, :], gpost_ref[...])
            out_ref[rows, :] = _ple_update(h2, p_ref[rows, :], ggn_ref[...], wpg_ref[...],
                                           wpp_ref[...], gple_ref[...])


def _l0_tail(oa, om, wo, gmix, h, gpre, wg, wu, wd, gpost, p, ggn, wpg, wpp, gple):
    m = h.shape[0]
    dff = wg.shape[1]
    row = lambda i, j: (i, 0)
    const = lambda i, j: (0, 0)
    vec = lambda g: pl.BlockSpec(g.shape, const)
    once = lambda w: pl.BlockSpec(w.shape, const, pipeline_mode=pl.Buffered(1))
    return pl.pallas_call(
        _l0_tail_kernel, grid=(m // TM_FFN, dff // TF),
        in_specs=[pl.BlockSpec((TM_FFN, oa.shape[1]), row), pl.BlockSpec((TM_FFN, om.shape[1]), row),
                  once(wo), vec(gmix), pl.BlockSpec((TM_FFN, D_MODEL), row), vec(gpre),
                  pl.BlockSpec((D_MODEL, TF), lambda i, j: (0, j)),
                  pl.BlockSpec((D_MODEL, TF), lambda i, j: (0, j)),
                  pl.BlockSpec((TF, D_MODEL), lambda i, j: (j, 0)),
                  vec(gpost), pl.BlockSpec((TM_FFN, PLE_DIM), row), vec(ggn), once(wpg), once(wpp),
                  vec(gple)],
        out_specs=pl.BlockSpec((TM_FFN, D_MODEL), row),
        out_shape=jax.ShapeDtypeStruct((m, D_MODEL), F32),
        scratch_shapes=[pltpu.VMEM((TM_FFN, D_MODEL), BF16), pltpu.VMEM((TM_FFN, D_MODEL), F32)],
        compiler_params=_params(("parallel", "arbitrary"), 56), name="l0_tail",
    )(oa, om, wo, gmix, h, gpre, wg, wu, wd, gpost, p, ggn, wpg, wpp, gple)


def _l1_qkv_kernel(h_ref, gpre_ref, w_ref, c_ref, s_ref, out0_ref, out1_ref, out2_ref, stage_ref):
    a = _rms(h_ref[...], gpre_ref[...]).astype(BF16)
    tm = a.shape[0]
    outs = (out0_ref, out1_ref, out2_ref)
    for blk in range(DIL_IN_WIDTH // 512):
        gi, kind = divmod(blk, 3)
        dil = DIL_CONFIGS[gi][1]
        z = _dot(a, w_ref[:, blk * 512:(blk + 1) * 512])
        if kind < 2:
            c, s = c_ref[kind], s_ref[kind]
            z = jnp.concatenate(
                [_rotate(z[:, sub * LANES:(sub + 1) * LANES], c, s, PARTIAL_ROPE_DIM // 2)
                 for sub in range(4)], axis=1)
        cols = slice(kind * 512, (kind + 1) * 512)
        if dil == 1:
            outs[gi][0, :, cols] = z.astype(BF16)
            continue
        for sub in range(4):
            stage = stage_ref.at[(blk % 2) * 4 + sub]
            stage[...] = z[:, sub * LANES:(sub + 1) * LANES]
            lanes = slice(kind * 512 + sub * LANES, kind * 512 + (sub + 1) * LANES)
            for r in range(dil):
                outs[gi][r, :, lanes] = stage[pl.ds(r, tm // dil, stride=dil), :].astype(BF16)


def _l1_qkv(h, gpre, w, ctab, stab, b, s):
    m = h.shape[0]
    nt = s // TM
    const = lambda i: (0, 0)
    out_specs, out_shape = [], []
    for _, dil in DIL_CONFIGS:
        out_specs.append(pl.BlockSpec((None, dil, TM // dil, 1536), lambda i: (i // nt, 0, i % nt, 0)))
        out_shape.append(jax.ShapeDtypeStruct((b, dil, s // dil, 1536), BF16))
    return pl.pallas_call(
        _l1_qkv_kernel, grid=(m // TM,),
        in_specs=[pl.BlockSpec((TM, D_MODEL), lambda i: (i, 0)),
                  pl.BlockSpec(gpre.shape, const),
                  pl.BlockSpec(w.shape, const, pipeline_mode=pl.Buffered(1)),
                  pl.BlockSpec((2, TM, LANES), lambda i: (0, i % nt, 0)),
                  pl.BlockSpec((2, TM, LANES), lambda i: (0, i % nt, 0))],
        out_specs=out_specs, out_shape=out_shape,
        scratch_shapes=[pltpu.VMEM((8, TM, LANES), F32)],
        compiler_params=_params(("parallel",), 48), name="l1_qkv",
    )(h, gpre, w, ctab, stab)


def _dil_attn_kernel(q0_ref, k0_ref, v0_ref, q1_ref, k1_ref, v1_ref, q2_ref, k2_ref, v2_ref, o_ref,
                     ra_ref, rl_ref, rm_ref, acc_ref, sum_ref, max_ref):
    lo = _lane_iota((DIL_QB, LANES)) < HEAD_DIM
    groups = ((q0_ref, k0_ref, v0_ref), (q1_ref, k1_ref, v1_ref), (q2_ref, k2_ref, v2_ref))
    for gi, (q_ref, k_ref, v_ref) in enumerate(groups):
        dil, length = q_ref.shape[0], q_ref.shape[1]
        width = min(DIL_QB + 2 * DIL_HALF, length)
        nblk = length // DIL_QB
        first = gi == 0
        ta, tl, tm_ = (acc_ref, sum_ref, max_ref) if first else (ra_ref, rl_ref, rm_ref)
        row = lax.broadcasted_iota(jnp.int32, (2 * DIL_QB, width), 0) % DIL_QB
        col = lax.broadcasted_iota(jnp.int32, (2 * DIL_QB, width), 1)
        rel = row - col
        ones = jnp.ones((width, LANES), BF16)

        def body(it, carry, length=length, width=width, nblk=nblk, ta=ta, tl=tl, tm_=tm_, rel=rel,
                 ones=ones, q_ref=q_ref, k_ref=k_ref, v_ref=v_ref):
            r = it // nblk
            i = it % nblk
            q0 = pl.multiple_of(i * DIL_QB, DIL_QB)
            start = pl.multiple_of(jnp.clip(q0 - DIL_HALF, 0, length - width), DIL_HALF)
            qrow = pl.multiple_of(r * length + q0, DIL_HALF)
            q = q_ref[r, pl.ds(q0, DIL_QB), :]
            kw = k_ref[r, pl.ds(start, width), :]
            vw = jnp.concatenate([v_ref[r, pl.ds(start, width), :], ones], axis=1)
            zero = jnp.zeros_like(q)
            q2 = jnp.concatenate([jnp.where(lo, q, zero), jnp.where(lo, zero, q)], axis=0)
            sc = lax.dot_general(q2, kw, (((1,), (1,)), ((), ())), preferred_element_type=F32)
            sc = jnp.where(jnp.abs(rel + (q0 - start)) <= DIL_HALF, sc, NEG_INF)
            m = jnp.max(sc, axis=-1, keepdims=True)
            p = jnp.exp2(sc - m).astype(BF16)
            o2 = _dot(p, vw)
            ta[pl.ds(qrow, DIL_QB), :] = jnp.where(lo, o2[:DIL_QB, 0:LANES], o2[DIL_QB:, 0:LANES])
            tl[pl.ds(qrow, DIL_QB), :] = jnp.where(lo, o2[:DIL_QB, LANES:], o2[DIL_QB:, LANES:])
            tm_[pl.ds(qrow, DIL_QB), :] = jnp.where(lo, m[:DIL_QB], m[DIL_QB:])
            return carry

        lax.fori_loop(0, dil * nblk, body, 0, unroll=8)
        if first:
            continue
        for r in range(dil):
            rows = pl.ds(r, length, stride=dil)
            blk = slice(r * length, (r + 1) * length)
            m_old, m_blk = max_ref[rows, :], rm_ref[blk, :]
            m_new = jnp.maximum(m_old, m_blk)
            w_old, w_blk = jnp.exp2(m_old - m_new), jnp.exp2(m_blk - m_new)
            acc_ref[rows, :] = acc_ref[rows, :] * w_old + ra_ref[blk, :] * w_blk
            sum_ref[rows, :] = sum_ref[rows, :] * w_old + rl_ref[blk, :] * w_blk
            max_ref[rows, :] = m_new
    o_ref[...] = (acc_ref[...] / sum_ref[...]).astype(BF16)


def _dil_attn(qkvs, b, s):
    in_specs, args = [], []
    for x in qkvs:
        for j in range(3):
            in_specs.append(pl.BlockSpec((None,) + x.shape[1:3] + (LANES,),
                                         lambda bi, hp, j=j: (bi, 0, 0, 4 * j + hp)))
            args.append(x)
    stat = pltpu.VMEM((s, LANES), F32)
    return pl.pallas_call(
        _dil_attn_kernel, grid=(b, 4), in_specs=in_specs,
        out_specs=pl.BlockSpec((s, LANES), lambda bi, hp: (bi, hp)),
        out_shape=jax.ShapeDtypeStruct((b * s, 512), BF16),
        scratch_shapes=[stat] * 6,
        compiler_params=_params(("parallel", "parallel"), 40), name="dil_attn",
    )(*args)


def _l1_router_kernel(o_ref, wo_ref, gmix_ref, h_ref, gpre_ref, wr_ref,
                      h1_ref, a_ref, idx_ref, gate_ref, cnt_ref, carry_ref):
    @pl.when(pl.program_id(0) == 0)
    def _():
        carry_ref[...] = jnp.zeros_like(carry_ref)

    h1 = h_ref[...] + _rms(_dot(o_ref[...], wo_ref[...]), gmix_ref[...])
    h1_ref[...] = h1
    a = _rms(h1, gpre_ref[...]).astype(BF16)
    a_ref[...] = a
    logits = _dot(a, wr_ref[...])
    tm = logits.shape[0]
    lane = _lane_iota(logits.shape)
    lanef = lane.astype(F32)
    lg = jnp.where(lane < N_EXPERTS, logits, -jnp.inf)
    m1 = jnp.max(lg, axis=-1, keepdims=True)
    i1 = jnp.min(jnp.where(lg == m1, lanef, float(LANES)), axis=-1, keepdims=True)
    lg2 = jnp.where(lanef == i1, -jnp.inf, lg)
    m2 = jnp.max(lg2, axis=-1, keepdims=True)
    i2 = jnp.min(jnp.where(lg2 == m2, lanef, float(LANES)), axis=-1, keepdims=True)
    t = jnp.exp(m2 - m1)
    g1 = 1.0 / (1.0 + t)
    g2 = t / (1.0 + t)
    hit1 = lanef == i1
    hit2 = lanef == i2
    onehot = jnp.where(hit1 | hit2, 1.0, 0.0)
    r = lax.broadcasted_iota(jnp.int32, (tm, tm), 0)
    c = lax.broadcasted_iota(jnp.int32, (tm, tm), 1)
    tri = jnp.where(c < r, 1.0, 0.0).astype(BF16)
    before = _dot(tri, onehot.astype(BF16)) + carry_ref[...]
    rank1 = jnp.sum(jnp.where(hit1, before, 0.0), axis=-1, keepdims=True)
    rank2 = jnp.sum(jnp.where(hit2, before, 0.0), axis=-1, keepdims=True)
    info = jnp.where(lane == 0, i1, jnp.where(lane == 1, i2, jnp.where(lane == 2, rank1, rank2)))
    idx_ref[...] = info.astype(jnp.int32)
    gate_ref[...] = jnp.where(lane == 0, g1, g2)
    carry_ref[...] += jnp.sum(onehot, axis=0, keepdims=True)
    cnt_ref[...] = carry_ref[...]


def _l1_router(o, wo, gmix, h, gpre, wr):
    m = h.shape[0]
    row = lambda i: (i, 0)
    const = lambda i: (0, 0)
    full = lambda x: pl.BlockSpec(x.shape, const)
    return pl.pallas_call(
        _l1_router_kernel, grid=(m // TM,),
        in_specs=[pl.BlockSpec((TM, o.shape[1]), row), full(wo), full(gmix),
                  pl.BlockSpec((TM, D_MODEL), row), full(gpre), full(wr)],
        out_specs=[pl.BlockSpec((TM, D_MODEL), row), pl.BlockSpec((TM, D_MODEL), row),
                   pl.BlockSpec((TM, LANES), row), pl.BlockSpec((TM, LANES), row),
                   pl.BlockSpec((1, LANES), const)],
        out_shape=[jax.ShapeDtypeStruct((m, D_MODEL), F32),
                   jax.ShapeDtypeStruct((m, D_MODEL), BF16),
                   jax.ShapeDtypeStruct((m, LANES), jnp.int32),
                   jax.ShapeDtypeStruct((m, LANES), F32),
                   jax.ShapeDtypeStruct((1, LANES), F32)],
        scratch_shapes=[pltpu.VMEM((1, LANES), F32)],
        compiler_params=_params(("arbitrary",), 32), name="l1_router",
    )(o, wo, gmix, h, gpre, wr)


def _expert_kernel(te_ref, tv_ref, x_ref, wg_ref, wu_ref, wd_ref, *refs):
    y_ref, acc_ref = refs[-2:]
    i = pl.program_id(0)
    j = pl.program_id(1)
    last = pl.num_programs(1) - 1
    valid = tv_ref[i] > 0

    @pl.when(valid & (j == 0))
    def _():
        acc_ref[...] = jnp.zeros_like(acc_ref)

    @pl.when(valid)
    def _():
        x = x_ref[...]
        gate = _dot(x, wg_ref[...].astype(BF16))
        up = _dot(x, wu_ref[...].astype(BF16))
        acc_ref[...] += _dot((_silu(gate) * up).astype(BF16), wd_ref[...].astype(BF16))

    @pl.when(valid & (j == last))
    def _():
        y_ref[...] = acc_ref[...].astype(BF16)

    @pl.when(jnp.logical_not(valid) & (j == last))
    def _():
        y_ref[...] = jnp.zeros_like(y_ref)


def _experts(tile_e, tile_valid, xs, wg, wu, wd, ys, first_tile, n_slots):
    n_tiles = xs.shape[0] // TM_MOE
    dff = wg.shape[2]
    te = tile_e[first_tile:first_tile + n_tiles]
    tv = tile_valid[first_tile:first_tile + n_tiles]
    in_specs = [pl.BlockSpec((TM_MOE, D_MODEL), lambda i, j, te, tv: (i, 0)),
                pl.BlockSpec((None, D_MODEL, TF), lambda i, j, te, tv: (te[i], 0, j * tv[i])),
                pl.BlockSpec((None, D_MODEL, TF), lambda i, j, te, tv: (te[i], 0, j * tv[i])),
                pl.BlockSpec((None, TF, D_MODEL), lambda i, j, te, tv: (te[i], j * tv[i], 0))]
    args = [te, tv, xs, wg, wu, wd]
    aliases = {}
    if ys is not None:
        in_specs.append(pl.BlockSpec(memory_space=pl.ANY))
        aliases = {len(args): 0}
        args.append(ys)
    grid_spec = pltpu.PrefetchScalarGridSpec(
        num_scalar_prefetch=2, grid=(n_tiles, dff // TF), in_specs=in_specs,
        out_specs=pl.BlockSpec((TM_MOE, D_MODEL), lambda i, j, te, tv: (first_tile + i, 0)),
        scratch_shapes=[pltpu.VMEM((TM_MOE, D_MODEL), F32)])
    return pl.pallas_call(
        _expert_kernel, grid_spec=grid_spec,
        out_shape=jax.ShapeDtypeStruct((n_slots, D_MODEL), BF16),
        input_output_aliases=aliases,
        compiler_params=_params(("parallel", "arbitrary"), 48), name="moe_experts",
    )(*args)


def _combine_ple_kernel(y0_ref, y1_ref, gate_ref, g_ref, h_ref, p_ref, ggn_ref, wpg_ref, wpp_ref,
                        gple_ref, out_ref):
    gate = gate_ref[...]
    f = y0_ref[...].astype(F32) * gate[:, 0:1] + y1_ref[...].astype(F32) * gate[:, 1:2]
    h2 = h_ref[...] + _rms(f, g_ref[...])
    out_ref[...] = _ple_update(h2, p_ref[...], ggn_ref[...], wpg_ref[...], wpp_ref[...], gple_ref[...])


def _combine_ple(y0, y1, gate, g, h, p, ggn, wpg, wpp, gple):
    m = h.shape[0]
    row = lambda i: (i, 0)
    const = lambda i: (0, 0)
    full = lambda x: pl.BlockSpec(x.shape, const)
    act = pl.BlockSpec((TM, D_MODEL), row)
    return pl.pallas_call(
        _combine_ple_kernel, grid=(m // TM,),
        in_specs=[act, act, pl.BlockSpec((TM, LANES), row), full(g), act,
                  pl.BlockSpec((TM, PLE_DIM), row), full(ggn), full(wpg), full(wpp), full(gple)],
        out_specs=act,
        out_shape=jax.ShapeDtypeStruct((m, D_MODEL), F32),
        compiler_params=_params(("parallel",), 40), name="combine_ple",
    )(y0, y1, gate, g, h, p, ggn, wpg, wpp, gple)


def _rows(x, idx):
    return x.at[idx].get(mode='promise_in_bounds')


def _moe_plan(idx, cnt, n):
    counts = cnt[0, :N_EXPERTS].astype(jnp.int32)
    padded = (counts + TM_MOE - 1) // TM_MOE * TM_MOE
    end_pad = jnp.cumsum(padded)
    start_pad = end_pad - padded
    experts = jnp.arange(N_EXPERTS, dtype=jnp.int32)
    start = jnp.sum(jnp.where(idx[:, 0:2, None] == experts, start_pad, 0), axis=-1)
    dest = start + idx[:, 2:4]
    n_slots = n * TOP_K + N_EXPERTS * TM_MOE
    tok = jnp.repeat(jnp.arange(n, dtype=jnp.int32), TOP_K)
    slot_tok = jnp.zeros((n_slots,), jnp.int32).at[dest.reshape(-1)].set(tok, unique_indices=True)
    tile_start = jnp.arange(n_slots // TM_MOE, dtype=jnp.int32) * TM_MOE
    tile_e = jnp.minimum(jnp.sum(tile_start[:, None] >= end_pad[None, :], axis=1),
                         N_EXPERTS - 1).astype(jnp.int32)
    tile_valid = (tile_start < end_pad[-1]).astype(jnp.int32)
    return dest, slot_tok, tile_e, tile_valid


def _rotary_tables(pos, dim, theta):
    exponent = jnp.arange(0, dim, 2, dtype=F32) / dim
    inv_freq = jnp.power(jnp.float32(theta), -exponent)
    ang = pos.astype(F32)[:, None] * inv_freq[None, :]
    return jnp.cos(ang), jnp.sin(ang)


def _tables(s):
    t = jnp.arange(s, dtype=jnp.int32)
    cr, sr = _rotary_tables(t // GRID_W, HEAD_DIM // 2, AXIAL_THETA)
    cc, sc = _rotary_tables(t % GRID_W, HEAD_DIM // 2, AXIAL_THETA)
    ca = jnp.tile(jnp.concatenate([cr, cr, cc, cc], axis=1), (1, 2))
    sa = jnp.tile(jnp.concatenate([-sr, sr, -sc, sc], axis=1), (1, 2))
    cm_, sm_ = _rotary_tables(t, MLA_ROPE_DIM, MLA_ROPE_THETA)
    one = jnp.ones((s, 1), F32)
    zero = jnp.zeros((s, 1), F32)
    cm = jnp.concatenate([jnp.tile(one, (1, 64)), cm_, cm_, jnp.tile(one, (1, 32))], axis=1)
    sm = jnp.concatenate([jnp.tile(zero, (1, 64)), -sm_, sm_, jnp.tile(zero, (1, 32))], axis=1)
    ckr = jnp.concatenate([cm_, cm_, jnp.tile(zero, (1, 96))], axis=1)
    skr = jnp.concatenate([-sm_, sm_, jnp.tile(zero, (1, 96))], axis=1)
    cp_, sp_ = _rotary_tables(t, PARTIAL_ROPE_DIM, ROPE_THETA)
    cp = jnp.tile(jnp.concatenate([cp_, cp_, jnp.tile(one, (1, 48))], axis=1), (1, 2))
    sp = jnp.tile(jnp.concatenate([-sp_, sp_, jnp.tile(zero, (1, 48))], axis=1), (1, 2))
    scale = HEAD_DIM ** -0.5 * LOG2_E
    ctab = jnp.stack([cp * scale, cp])
    stab = jnp.stack([sp * scale, sp])
    return ca, sa, cm, sm, ckr, skr, ctab, stab


_GQA_PERM = (0, 4, 1, 5, 2, 6, 3, 7)


def _l0_weights(w_in, w_uq, w_ukv, w_out):
    d = w_in.shape[0]
    q = w_in[:, :512].reshape(d, 8, 64)[:, _GQA_PERM, :].reshape(d, 512)
    win = jnp.concatenate([q, w_in[:, 512:], jnp.zeros((d, AB_IN_PAD - w_in.shape[1]), w_in.dtype)],
                          axis=1).astype(BF16)
    uq = w_uq.reshape(MLA_Q_RANK, MLA_HEADS, MLA_NOPE_DIM + MLA_ROPE_DIM)
    wuq = jnp.pad(uq, ((0, 0), (0, 0), (0, LANES - uq.shape[2]))).reshape(MLA_Q_RANK, -1).astype(BF16)
    ukv = w_ukv.reshape(MLA_KV_RANK, MLA_HEADS, MLA_NOPE_DIM + MLA_V_DIM)
    wk = jnp.pad(ukv[:, :, :MLA_NOPE_DIM], ((0, 0), (0, 0), (0, LANES - MLA_NOPE_DIM)))
    wk = wk.reshape(MLA_KV_RANK, -1).astype(BF16)
    wv = ukv[:, :, MLA_NOPE_DIM:].reshape(MLA_KV_RANK, -1).astype(BF16)
    src = jnp.arange(LANES)[:, None]
    dst = jnp.arange(MLA_HEADS * LANES)[None, :]
    wp = ((src < MLA_ROPE_DIM) & (dst % LANES == src + MLA_NOPE_DIM)).astype(BF16)
    oa = w_out[:512].reshape(8, 64, -1)[_GQA_PERM, :, :].reshape(512, -1)
    wout = jnp.concatenate([oa, w_out[512:]], axis=0).astype(BF16)
    return win, wuq, wk, wp, wv, wout


def kernel(x, p, mix_pre_g, mix_post_g, ffn_pre_g, ffn_post_g, ple_w_proj, ple_gate_norm_g, ple_w_gate, ple_post_g, ab_w_in, gqa_q_norm_g, gqa_k_norm_g, mla_q_norm_g, mla_w_uq, mla_kv_norm_g, mla_w_ukv, ab_w_out, ffn_w_gate, ffn_w_up, ffn_w_down, dil_w_qkv, dil_w_out, moe_w_router, moe_w_gate, moe_w_up, moe_w_down):
    b, s, d = x.shape
    m = b * s
    row = lambda g: g.reshape(1, -1).astype(F32)
    ca, sa, cm, sm, ckr, skr, ctab, stab = _tables(s)
    h = x.reshape(m, d)
    pf = p.reshape(p.shape[0], m, PLE_DIM)

    win, wuq, wk, wp, wv, wout = _l0_weights(ab_w_in[0], mla_w_uq[0], mla_w_ukv[0], ab_w_out[0])
    gq = jnp.tile(row(gqa_q_norm_g[0]), (1, 2))
    gk = jnp.tile(row(gqa_k_norm_g[0]), (1, 2))
    qa, ka, vat, qm, km, vmt = _l0_in(h, row(mix_pre_g[0]), win, gq, gk, ca, sa,
                                      row(mla_q_norm_g[0]), wuq, cm, sm,
                                      row(mla_kv_norm_g[0]), wk, wp, wv, ckr, skr, b, s)
    oa = _gqa_attn(qa, ka, vat, b, s)
    om = _mla_attn(qm, km, vmt, b, s)
    h = _l0_tail(oa, om, wout, row(mix_post_g[0]), h, row(ffn_pre_g[0]),
                 ffn_w_gate[0].astype(BF16), ffn_w_up[0].astype(BF16), ffn_w_down[0].astype(BF16),
                 row(ffn_post_g[0]), pf[0], row(ple_gate_norm_g[0]), ple_w_gate[0].astype(BF16),
                 ple_w_proj[0].astype(BF16), row(ple_post_g[0]))

    qkvs = _l1_qkv(h, row(mix_pre_g[1]), dil_w_qkv[0].astype(BF16), ctab, stab, b, s)
    o = _dil_attn(qkvs, b, s)
    wr = jnp.pad(moe_w_router[0].astype(BF16), ((0, 0), (0, LANES - N_EXPERTS)))
    h, a, idx, gate, cnt = _l1_router(o, dil_w_out[0].astype(BF16), row(mix_post_g[1]), h,
                                      row(ffn_pre_g[1]), wr)
    dest, slot_tok, tile_e, tile_valid = _moe_plan(idx, cnt, m)
    n_slots = slot_tok.shape[0]
    ys, first_tile = None, 0
    assert (n_slots // TM_MOE) % sum(MOE_CHUNKS) == 0
    for share in MOE_CHUNKS:
        n_tiles = n_slots // TM_MOE * share // sum(MOE_CHUNKS)
        rows = slice(first_tile * TM_MOE, (first_tile + n_tiles) * TM_MOE)
        ys = _experts(tile_e, tile_valid, _rows(a, slot_tok[rows]), moe_w_gate[0], moe_w_up[0],
                      moe_w_down[0], ys, first_tile, n_slots)
        first_tile += n_tiles
    h = _combine_ple(_rows(ys, dest[:, 0]), _rows(ys, dest[:, 1]), gate, row(ffn_post_g[1]), h, pf[1],
                     row(ple_gate_norm_g[1]), ple_w_gate[1].astype(BF16), ple_w_proj[1].astype(BF16),
                     row(ple_post_g[1]))
    return h.reshape(b, s, d)
```

```python
import functools

import jax
import jax.numpy as jnp
from jax import lax
from jax.experimental import pallas as pl
from jax.experimental.pallas import tpu as pltpu

F32 = jnp.float32
BF16 = jnp.bfloat16

D_MODEL = 1024
GRID_W = 64
HEAD_DIM = 64
NORM_EPS = 1e-6
NEG_INF = -1e30
ROPE_THETA = 500000.0
PARTIAL_ROPE_DIM = HEAD_DIM // 4
GQA_Q_HEADS = 8
GQA_KV_HEADS = 2
AXIAL_THETA = 10000.0
MLA_HEADS = 8
MLA_Q_RANK = 256
MLA_KV_RANK = 128
MLA_NOPE_DIM = 64
MLA_ROPE_DIM = 32
MLA_V_DIM = 64
MLA_ROPE_THETA = 10000.0
DIL_CONFIGS = ((128, 1), (512, 4), (2048, 16))
DIL_GROUPS = len(DIL_CONFIGS)
DIL_HEADS = 8
DIL_HALF = 64
DIL_QB = 128
LOG2_E = 1.4426950408889634
DIL_IN_WIDTH = DIL_GROUPS * 3 * DIL_HEADS * HEAD_DIM
D_FF = 3584
N_EXPERTS = 8
TOP_K = 2
PLE_DIM = 256

LANES = 128
VMEM_BYTES = 64 * 1024 * 1024
MIB = 1024 * 1024

TM = 512
TQ = 1024
KEY_CHUNK = 256
V_ONES = 16
V_ROWS = LANES + V_ONES
TF = 512
TM_FFN = 1024
TAIL_ROWS = 256
TM_MOE = 1024
AB_IN_PAD = 1280
MOE_GROUP = 4
MOE_CHUNKS = (1, 2, 3, 3)


def _params(semantics, vmem_mib):
    return pltpu.CompilerParams(dimension_semantics=semantics,
                                vmem_limit_bytes=vmem_mib * MIB)


def _dot(a, b):
    return jnp.dot(a, b, preferred_element_type=F32)


def _rms(x, g):
    return x * lax.rsqrt(jnp.mean(x * x, axis=-1, keepdims=True) + NORM_EPS) * g


def _lane_iota(shape):
    return lax.broadcasted_iota(jnp.int32, shape, len(shape) - 1)


def _swap_halves(x, k):
    w = x.shape[-1]
    fwd = pltpu.roll(x, w - k, 1)
    bwd = pltpu.roll(x, k, 1)
    return jnp.where((_lane_iota(x.shape) % (2 * k)) < k, fwd, bwd)


def _rotate(x, c, s, k):
    return x * c + _swap_halves(x, k) * s


def _head_rms(x, g):
    lo = _lane_iota(x.shape) < HEAD_DIM
    x2 = x * x
    s_all = jnp.sum(x2, axis=-1, keepdims=True)
    s_lo = jnp.sum(jnp.where(lo, x2, 0.0), axis=-1, keepdims=True)
    ms = jnp.where(lo, s_lo, s_all - s_lo) * (1.0 / HEAD_DIM)
    return x * lax.rsqrt(ms + NORM_EPS) * g


def _l0_in_kernel(h_ref, gpre_ref, win_ref, gq_ref, gk_ref, ca_ref, sa_ref,
                  gmq_ref, wuq_ref, cm_ref, sm_ref, gmkv_ref, wk_ref, wp_ref, wv_ref,
                  ckr_ref, skr_ref,
                  qa_ref, ka_ref, vat_ref, qm_ref, km_ref, vmt_ref):
    a = _rms(h_ref[...], gpre_ref[...]).astype(BF16)
    z = _dot(a, win_ref[...])
    ca, sa = ca_ref[...], sa_ref[...]
    for j in range(4):
        xq = _head_rms(z[:, j * LANES:(j + 1) * LANES], gq_ref[...])
        qa_ref[:, j * LANES:(j + 1) * LANES] = (
            _rotate(xq, ca, sa, 16) * (HEAD_DIM ** -0.5 * LOG2_E)).astype(BF16)
    ka_ref[...] = _rotate(_head_rms(z[:, 512:640], gk_ref[...]), ca, sa, 16).astype(BF16)
    ones = jnp.ones((V_ONES, z.shape[0]), BF16)
    vat_ref[0:LANES, :] = z[:, 640:768].T.astype(BF16)
    vat_ref[LANES:, :] = ones
    cq = _rms(z[:, 768:1024], gmq_ref[...]).astype(BF16)
    qm = _dot(cq, wuq_ref[...])
    cm, sm = cm_ref[...], sm_ref[...]
    scale_m = (MLA_NOPE_DIM + MLA_ROPE_DIM) ** -0.5 * LOG2_E
    for hb in range(MLA_HEADS):
        x = qm[:, hb * LANES:(hb + 1) * LANES]
        qm_ref[:, hb * LANES:(hb + 1) * LANES] = (_rotate(x, cm, sm, 16) * scale_m).astype(BF16)
    ckv = _rms(z[:, 1024:1152], gmkv_ref[...]).astype(BF16)
    kr = _rotate(z[:, 1152:1280], ckr_ref[...], skr_ref[...], 16).astype(BF16)
    km = _dot(ckv, wk_ref[...]) + _dot(kr, wp_ref[...])
    km_ref[...] = km.astype(BF16)
    vm = _dot(ckv, wv_ref[...])
    for j in range(4):
        vmt_ref[j * V_ROWS:j * V_ROWS + LANES, :] = vm[:, j * LANES:(j + 1) * LANES].T.astype(BF16)
        vmt_ref[j * V_ROWS + LANES:(j + 1) * V_ROWS, :] = ones


def _l0_in(h, gpre, win, gq, gk, ca, sa, gmq, wuq, cm, sm, gmkv, wk, wp, wv, ckr, skr, b, s):
    m = h.shape[0]
    nt = s // TM
    row = lambda i: (i, 0)
    const = lambda i: (0, 0)
    tab = lambda i: (i % nt, 0)
    full = lambda a: pl.BlockSpec(a.shape, const)
    in_specs = [pl.BlockSpec((TM, D_MODEL), row), full(gpre), full(win), full(gq), full(gk),
                pl.BlockSpec((TM, LANES), tab), pl.BlockSpec((TM, LANES), tab),
                full(gmq), full(wuq), pl.BlockSpec((TM, LANES), tab), pl.BlockSpec((TM, LANES), tab),
                full(gmkv), full(wk), full(wp), full(wv),
                pl.BlockSpec((TM, LANES), tab), pl.BlockSpec((TM, LANES), tab)]
    tr = lambda i: (i // nt, 0, i % nt)
    out_specs = [pl.BlockSpec((TM, 512), row),
                 pl.BlockSpec((TM, LANES), row),
                 pl.BlockSpec((None, V_ROWS, TM), tr),
                 pl.BlockSpec((TM, 1024), row),
                 pl.BlockSpec((TM, 1024), row),
                 pl.BlockSpec((None, 4 * V_ROWS, TM), tr)]
    out_shape = [jax.ShapeDtypeStruct((m, 512), BF16),
                 jax.ShapeDtypeStruct((m, LANES), BF16),
                 jax.ShapeDtypeStruct((b, V_ROWS, s), BF16),
                 jax.ShapeDtypeStruct((m, 1024), BF16),
                 jax.ShapeDtypeStruct((m, 1024), BF16),
                 jax.ShapeDtypeStruct((b, 4 * V_ROWS, s), BF16)]
    return pl.pallas_call(
        _l0_in_kernel, grid=(m // TM,), in_specs=in_specs, out_specs=out_specs,
        out_shape=out_shape, compiler_params=_params(("parallel",), 48), name="l0_in",
    )(h, gpre, win, gq, gk, ca, sa, gmq, wuq, cm, sm, gmkv, wk, wp, wv, ckr, skr)


def _attend_pair(score_t, vt_ref, o_ref):
    n_chunks = vt_ref.shape[1] // KEY_CHUNK
    tq = o_ref.shape[0]
    keys = [slice(c * KEY_CHUNK, (c + 1) * KEY_CHUNK) for c in range(n_chunks)]
    parts = []
    s_next = score_t(keys[0])
    for c in range(n_chunks):
        s_t = s_next
        if c + 1 < n_chunks:
            s_next = score_t(keys[c + 1])
        m = jnp.max(s_t, axis=0, keepdims=True)
        parts.append((m, _dot(vt_ref[:, keys[c]], jnp.exp2(s_t - m).astype(BF16))))
    m_all = functools.reduce(jnp.maximum, [m for m, _ in parts])
    o_t = sum(o_c * jnp.exp2(m_c - m_all) for m_c, o_c in parts)
    o_t = o_t[0:LANES, :] / o_t[LANES:LANES + 1, :]
    first = lax.broadcasted_iota(jnp.int32, (LANES, tq), 0) < HEAD_DIM
    o_ref[...] = jnp.where(first, o_t[:, 0:tq], o_t[:, tq:]).T.astype(BF16)


def _transposed(q):
    return q.astype(F32).T.astype(BF16)


def _gqa_attn_kernel(q_ref, k_ref, vt_ref, o_ref):
    qt = _transposed(q_ref[...])
    first = lax.broadcasted_iota(jnp.int32, qt.shape, 0) < HEAD_DIM
    zero = jnp.zeros_like(qt)
    w = jnp.concatenate([jnp.where(first, qt, zero), jnp.where(first, zero, qt)], axis=1)
    _attend_pair(lambda keys: _dot(k_ref[keys, :], w), vt_ref, o_ref)


def _gqa_attn(qa, ka, vat, b, s):
    m = qa.shape[0]
    nq = s // TQ
    return pl.pallas_call(
        _gqa_attn_kernel, grid=(b, nq, 4),
        in_specs=[pl.BlockSpec((TQ, LANES), lambda bi, i, j: (bi * nq + i, j)),
                  pl.BlockSpec((s, LANES), lambda bi, i, j: (bi, 0)),
                  pl.BlockSpec((None, V_ROWS, s), lambda bi, i, j: (bi, 0, 0))],
        out_specs=pl.BlockSpec((TQ, LANES), lambda bi, i, j: (bi * nq + i, j)),
        out_shape=jax.ShapeDtypeStruct((m, 512), BF16),
        compiler_params=_params(("parallel", "parallel", "parallel"), 32), name="gqa_attn",
    )(qa, ka, vat)


def _mla_attn_kernel(q_ref, k_ref, vt_ref, o_ref):
    qt0 = _transposed(q_ref[:, 0:LANES])
    qt1 = _transposed(q_ref[:, LANES:2 * LANES])

    def score_t(keys):
        return jnp.concatenate([_dot(k_ref[keys, 0:LANES], qt0),
                                _dot(k_ref[keys, LANES:2 * LANES], qt1)], axis=1)

    _attend_pair(score_t, vt_ref, o_ref)


def _mla_attn(qm, km, vmt, b, s):
    m = qm.shape[0]
    nq = s // TQ
    return pl.pallas_call(
        _mla_attn_kernel, grid=(b, nq, 4),
        in_specs=[pl.BlockSpec((TQ, 2 * LANES), lambda bi, i, j: (bi * nq + i, j)),
                  pl.BlockSpec((s, 2 * LANES), lambda bi, i, j: (bi, j)),
                  pl.BlockSpec((None, V_ROWS, s), lambda bi, i, j: (bi, j, 0))],
        out_specs=pl.BlockSpec((TQ, LANES), lambda bi, i, j: (bi * nq + i, j)),
        out_shape=jax.ShapeDtypeStruct((m, 512), BF16),
        compiler_params=_params(("parallel", "parallel", "parallel"), 32), name="mla_attn",
    )(qm, km, vmt)


def _silu(x):
    return x * (1.0 / (1.0 + jnp.exp(-x)))


def _ple_update(h, p, ggn, wg, wp, gpost):
    gate = _dot(_rms(h, ggn).astype(BF16), wg)
    gate = 1.0 / (1.0 + jnp.exp(-gate))
    e = _dot(p.astype(BF16), wp) * gate
    return h + _rms(e, gpost)


def _l0_tail_kernel(oa_ref, om_ref, wo_ref, gmix_ref, h_ref, gpre_ref, wg_ref, wu_ref, wd_ref,
                    gpost_ref, p_ref, ggn_ref, wpg_ref, wpp_ref, gple_ref, out_ref, a_ref, acc_ref):
    j = pl.program_id(1)

    chunks = [slice(r, r + TAIL_ROWS) for r in range(0, out_ref.shape[0], TAIL_ROWS)]

    @pl.when(j == 0)
    def _():
        k = oa_ref.shape[1]
        for rows in chunks:
            y = _dot(oa_ref[rows, :], wo_ref[0:k, :]) + _dot(om_ref[rows, :], wo_ref[k:, :])
            h1 = h_ref[rows, :] + _rms(y, gmix_ref[...])
            out_ref[rows, :] = h1
            a_ref[rows, :] = _rms(h1, gpre_ref[...]).astype(BF16)
        acc_ref[...] = jnp.zeros_like(acc_ref)

    a = a_ref[...]
    mid = (_silu(_dot(a, wg_ref[...])) * _dot(a, wu_ref[...])).astype(BF16)
    acc_ref[...] += _dot(mid, wd_ref[...])

    @pl.when(j == pl.num_programs(1) - 1)
    def _():
        for rows in chunks:
            h2 = out_ref[rows, :] + _rms(acc_ref[rows, :], gpost_ref[...])
            out_ref[rows, :] = _ple_update(h2, p_ref[rows, :], ggn_ref[...], wpg_ref[...],
                                           wpp_ref[...], gple_ref[...])


def _l0_tail(oa, om, wo, gmix, h, gpre, wg, wu, wd, gpost, p, ggn, wpg, wpp, gple):
    m = h.shape[0]
    dff = wg.shape[1]
    row = lambda i, j: (i, 0)
    const = lambda i, j: (0, 0)
    vec = lambda g: pl.BlockSpec(g.shape, const)
    once = lambda w: pl.BlockSpec(w.shape, const, pipeline_mode=pl.Buffered(1))
    return pl.pallas_call(
        _l0_tail_kernel, grid=(m // TM_FFN, dff // TF),
        in_specs=[pl.BlockSpec((TM_FFN, oa.shape[1]), row), pl.BlockSpec((TM_FFN, om.shape[1]), row),
                  once(wo), vec(gmix), pl.BlockSpec((TM_FFN, D_MODEL), row), vec(gpre),
                  pl.BlockSpec((D_MODEL, TF), lambda i, j: (0, j)),
                  pl.BlockSpec((D_MODEL, TF), lambda i, j: (0, j)),
                  pl.BlockSpec((TF, D_MODEL), lambda i, j: (j, 0)),
                  vec(gpost), pl.BlockSpec((None, TM_FFN, PLE_DIM), lambda i, j: (0, i, 0)), vec(ggn), once(wpg), once(wpp),
                  vec(gple)],
        out_specs=pl.BlockSpec((TM_FFN, D_MODEL), row),
        out_shape=jax.ShapeDtypeStruct((m, D_MODEL), F32),
        scratch_shapes=[pltpu.VMEM((TM_FFN, D_MODEL), BF16), pltpu.VMEM((TM_FFN, D_MODEL), F32)],
        compiler_params=_params(("parallel", "arbitrary"), 56), name="l0_tail",
    )(oa, om, wo, gmix, h, gpre, wg, wu, wd, gpost, p, ggn, wpg, wpp, gple)


def _l1_qkv_kernel(h_ref, gpre_ref, w_ref, c_ref, s_ref, out0_ref, out1_ref, out2_ref, stage_ref):
    a = _rms(h_ref[...], gpre_ref[...]).astype(BF16)
    tm = a.shape[0]
    outs = (out0_ref, out1_ref, out2_ref)
    for blk in range(DIL_IN_WIDTH // 512):
        gi, kind = divmod(blk, 3)
        dil = DIL_CONFIGS[gi][1]
        z = _dot(a, w_ref[:, blk * 512:(blk + 1) * 512])
        if kind < 2:
            c, s = c_ref[kind], s_ref[kind]
            z = jnp.concatenate(
                [_rotate(z[:, sub * LANES:(sub + 1) * LANES], c, s, PARTIAL_ROPE_DIM // 2)
                 for sub in range(4)], axis=1)
        cols = slice(kind * 512, (kind + 1) * 512)
        if dil == 1:
            outs[gi][0, :, cols] = z.astype(BF16)
            continue
        for sub in range(4):
            stage = stage_ref.at[(blk % 2) * 4 + sub]
            stage[...] = z[:, sub * LANES:(sub + 1) * LANES]
            lanes = slice(kind * 512 + sub * LANES, kind * 512 + (sub + 1) * LANES)
            for r in range(dil):
                outs[gi][r, :, lanes] = stage[pl.ds(r, tm // dil, stride=dil), :].astype(BF16)


def _l1_qkv(h, gpre, w, ctab, stab, b, s):
    m = h.shape[0]
    nt = s // TM
    const = lambda i: (0, 0)
    out_specs, out_shape = [], []
    for _, dil in DIL_CONFIGS:
        out_specs.append(pl.BlockSpec((None, dil, TM // dil, 1536), lambda i: (i // nt, 0, i % nt, 0)))
        out_shape.append(jax.ShapeDtypeStruct((b, dil, s // dil, 1536), BF16))
    return pl.pallas_call(
        _l1_qkv_kernel, grid=(m // TM,),
        in_specs=[pl.BlockSpec((TM, D_MODEL), lambda i: (i, 0)),
                  pl.BlockSpec(gpre.shape, const),
                  pl.BlockSpec(w.shape, const, pipeline_mode=pl.Buffered(1)),
                  pl.BlockSpec((2, TM, LANES), lambda i: (0, i % nt, 0)),
                  pl.BlockSpec((2, TM, LANES), lambda i: (0, i % nt, 0))],
        out_specs=out_specs, out_shape=out_shape,
        scratch_shapes=[pltpu.VMEM((8, TM, LANES), F32)],
        compiler_params=_params(("parallel",), 48), name="l1_qkv",
    )(h, gpre, w, ctab, stab)


def _dil_attn_kernel(q0_ref, k0_ref, v0_ref, q1_ref, k1_ref, v1_ref, q2_ref, k2_ref, v2_ref, o_ref,
                     ra_ref, rl_ref, rm_ref, acc_ref, sum_ref, max_ref):
    lo = _lane_iota((DIL_QB, LANES)) < HEAD_DIM
    groups = ((q0_ref, k0_ref, v0_ref), (q1_ref, k1_ref, v1_ref), (q2_ref, k2_ref, v2_ref))
    for gi, (q_ref, k_ref, v_ref) in enumerate(groups):
        dil, length = q_ref.shape[0], q_ref.shape[1]
        width = min(DIL_QB + 2 * DIL_HALF, length)
        nblk = length // DIL_QB
        first = gi == 0
        ta, tl, tm_ = (acc_ref, sum_ref, max_ref) if first else (ra_ref, rl_ref, rm_ref)
        row = lax.broadcasted_iota(jnp.int32, (2 * DIL_QB, width), 0) % DIL_QB
        col = lax.broadcasted_iota(jnp.int32, (2 * DIL_QB, width), 1)
        rel = row - col
        ones = jnp.ones((width, LANES), BF16)

        def body(it, carry, length=length, width=width, nblk=nblk, ta=ta, tl=tl, tm_=tm_, rel=rel,
                 ones=ones, q_ref=q_ref, k_ref=k_ref, v_ref=v_ref):
            r = it // nblk
            i = it % nblk
            q0 = pl.multiple_of(i * DIL_QB, DIL_QB)
            start = pl.multiple_of(jnp.clip(q0 - DIL_HALF, 0, length - width), DIL_HALF)
            qrow = pl.multiple_of(r * length + q0, DIL_HALF)
            q = q_ref[r, pl.ds(q0, DIL_QB), :]
            kw = k_ref[r, pl.ds(start, width), :]
            vw = jnp.concatenate([v_ref[r, pl.ds(start, width), :], ones], axis=1)
            zero = jnp.zeros_like(q)
            q2 = jnp.concatenate([jnp.where(lo, q, zero), jnp.where(lo, zero, q)], axis=0)
            sc = lax.dot_general(q2, kw, (((1,), (1,)), ((), ())), preferred_element_type=F32)
            sc = jnp.where(jnp.abs(rel + (q0 - start)) <= DIL_HALF, sc, NEG_INF)
            m = jnp.max(sc, axis=-1, keepdims=True)
            p = jnp.exp2(sc - m).astype(BF16)
            o2 = _dot(p, vw)
            ta[pl.ds(qrow, DIL_QB), :] = jnp.where(lo, o2[:DIL_QB, 0:LANES], o2[DIL_QB:, 0:LANES])
            tl[pl.ds(qrow, DIL_QB), :] = jnp.where(lo, o2[:DIL_QB, LANES:], o2[DIL_QB:, LANES:])
            tm_[pl.ds(qrow, DIL_QB), :] = jnp.where(lo, m[:DIL_QB], m[DIL_QB:])
            return carry

        lax.fori_loop(0, dil * nblk, body, 0, unroll=8)
        if first:
            continue
        for r in range(dil):
            rows = pl.ds(r, length, stride=dil)
            blk = slice(r * length, (r + 1) * length)
            m_old, m_blk = max_ref[rows, :], rm_ref[blk, :]
            m_new = jnp.maximum(m_old, m_blk)
            w_old, w_blk = jnp.exp2(m_old - m_new), jnp.exp2(m_blk - m_new)
            acc_ref[rows, :] = acc_ref[rows, :] * w_old + ra_ref[blk, :] * w_blk
            sum_ref[rows, :] = sum_ref[rows, :] * w_old + rl_ref[blk, :] * w_blk
            max_ref[rows, :] = m_new
    o_ref[...] = (acc_ref[...] / sum_ref[...]).astype(BF16)


def _dil_attn(qkvs, b, s):
    in_specs, args = [], []
    for x in qkvs:
        for j in range(3):
            in_specs.append(pl.BlockSpec((None,) + x.shape[1:3] + (LANES,),
                                         lambda bi, hp, j=j: (bi, 0, 0, 4 * j + hp)))
            args.append(x)
    stat = pltpu.VMEM((s, LANES), F32)
    return pl.pallas_call(
        _dil_attn_kernel, grid=(b, 4), in_specs=in_specs,
        out_specs=pl.BlockSpec((s, LANES), lambda bi, hp: (bi, hp)),
        out_shape=jax.ShapeDtypeStruct((b * s, 512), BF16),
        scratch_shapes=[stat] * 6,
        compiler_params=_params(("parallel", "parallel"), 40), name="dil_attn",
    )(*args)


def _l1_router_kernel(o_ref, wo_ref, gmix_ref, h_ref, gpre_ref, wr_ref,
                      h1_ref, a_ref, idx_ref, gate_ref, cnt_ref, carry_ref):
    @pl.when(pl.program_id(0) == 0)
    def _():
        carry_ref[...] = jnp.zeros_like(carry_ref)

    h1 = h_ref[...] + _rms(_dot(o_ref[...], wo_ref[...]), gmix_ref[...])
    h1_ref[...] = h1
    a = _rms(h1, gpre_ref[...]).astype(BF16)
    a_ref[...] = a
    logits = _dot(a, wr_ref[...])
    tm = logits.shape[0]
    lane = _lane_iota(logits.shape)
    lanef = lane.astype(F32)
    lg = jnp.where(lane < N_EXPERTS, logits, -jnp.inf)
    m1 = jnp.max(lg, axis=-1, keepdims=True)
    i1 = jnp.min(jnp.where(lg == m1, lanef, float(LANES)), axis=-1, keepdims=True)
    lg2 = jnp.where(lanef == i1, -jnp.inf, lg)
    m2 = jnp.max(lg2, axis=-1, keepdims=True)
    i2 = jnp.min(jnp.where(lg2 == m2, lanef, float(LANES)), axis=-1, keepdims=True)
    t = jnp.exp(m2 - m1)
    g1 = 1.0 / (1.0 + t)
    g2 = t / (1.0 + t)
    hit1 = lanef == i1
    hit2 = lanef == i2
    onehot = jnp.where(hit1 | hit2, 1.0, 0.0)
    r = lax.broadcasted_iota(jnp.int32, (tm, tm), 0)
    c = lax.broadcasted_iota(jnp.int32, (tm, tm), 1)
    tri = jnp.where(c < r, 1.0, 0.0).astype(BF16)
    before = _dot(tri, onehot.astype(BF16)) + carry_ref[...]
    rank1 = jnp.sum(jnp.where(hit1, before, 0.0), axis=-1, keepdims=True)
    rank2 = jnp.sum(jnp.where(hit2, before, 0.0), axis=-1, keepdims=True)
    info = jnp.where(lane == 0, i1, jnp.where(lane == 1, i2, jnp.where(lane == 2, rank1, rank2)))
    idx_ref[...] = info.astype(jnp.int32)
    gate_ref[...] = jnp.where(lane == 0, g1, g2)
    carry_ref[...] += jnp.sum(onehot, axis=0, keepdims=True)
    cnt_ref[...] = carry_ref[...]


def _l1_router(o, wo, gmix, h, gpre, wr):
    m = h.shape[0]
    row = lambda i: (i, 0)
    const = lambda i: (0, 0)
    full = lambda x: pl.BlockSpec(x.shape, const)
    return pl.pallas_call(
        _l1_router_kernel, grid=(m // TM,),
        in_specs=[pl.BlockSpec((TM, o.shape[1]), row), full(wo), full(gmix),
                  pl.BlockSpec((TM, D_MODEL), row), full(gpre), full(wr)],
        out_specs=[pl.BlockSpec((TM, D_MODEL), row), pl.BlockSpec((TM, D_MODEL), row),
                   pl.BlockSpec((TM, LANES), row), pl.BlockSpec((TM, LANES), row),
                   pl.BlockSpec((1, LANES), const)],
        out_shape=[jax.ShapeDtypeStruct((m, D_MODEL), F32),
                   jax.ShapeDtypeStruct((m, D_MODEL), BF16),
                   jax.ShapeDtypeStruct((m, LANES), jnp.int32),
                   jax.ShapeDtypeStruct((m, LANES), F32),
                   jax.ShapeDtypeStruct((1, LANES), F32)],
        scratch_shapes=[pltpu.VMEM((1, LANES), F32)],
        compiler_params=_params(("arbitrary",), 32), name="l1_router",
    )(o, wo, gmix, h, gpre, wr)


def _expert_kernel(te_ref, tv_ref, x_ref, wg_ref, wu_ref, wd_ref, *refs):
    y_ref, acc_ref = refs[-2:]
    j = pl.program_id(1)
    t = pl.program_id(2)
    last = pl.num_programs(1) - 1
    valid = tv_ref[pl.program_id(0) * MOE_GROUP + t] > 0

    @pl.when(valid & (j == 0))
    def _():
        acc_ref[t] = jnp.zeros(acc_ref.shape[1:], F32)

    @pl.when(valid)
    def _():
        x = x_ref[...]
        gate = _dot(x, wg_ref[...].astype(BF16))
        up = _dot(x, wu_ref[...].astype(BF16))
        acc_ref[t] += _dot((_silu(gate) * up).astype(BF16), wd_ref[...].astype(BF16))

    @pl.when(valid & (j == last))
    def _():
        y_ref[...] = acc_ref[t].astype(BF16)

    @pl.when(jnp.logical_not(valid) & (j == last))
    def _():
        y_ref[...] = jnp.zeros_like(y_ref)


def _experts(tile_e, tile_valid, xs, wg, wu, wd, ys, first_tile, n_slots):
    n_tiles = xs.shape[0] // TM_MOE
    n_j = wg.shape[2] // TF
    te = tile_e[first_tile:first_tile + n_tiles]
    tv = tile_valid[first_tile:first_tile + n_tiles]
    tile = lambda g, t: g * MOE_GROUP + t
    in_specs = [
        pl.BlockSpec((TM_MOE, D_MODEL), lambda g, j, t, te, tv: (tile(g, t), 0)),
        pl.BlockSpec((None, D_MODEL, TF),
                     lambda g, j, t, te, tv: (te[tile(g, t)], 0, j * tv[tile(g, t)])),
        pl.BlockSpec((None, D_MODEL, TF),
                     lambda g, j, t, te, tv: (te[tile(g, t)], 0, j * tv[tile(g, t)])),
        pl.BlockSpec((None, TF, D_MODEL),
                     lambda g, j, t, te, tv: (te[tile(g, t)], j * tv[tile(g, t)], 0))]
    args = [te, tv, xs, wg, wu, wd]
    aliases = {}
    if ys is not None:
        in_specs.append(pl.BlockSpec(memory_space=pl.ANY))
        aliases = {len(args): 0}
        args.append(ys)
    out_spec = pl.BlockSpec(
        (TM_MOE, D_MODEL),
        lambda g, j, t, te, tv: (first_tile + g * MOE_GROUP + jnp.where(j == n_j - 1, t, 0), 0))
    grid_spec = pltpu.PrefetchScalarGridSpec(
        num_scalar_prefetch=2, grid=(n_tiles // MOE_GROUP, n_j, MOE_GROUP), in_specs=in_specs,
        out_specs=out_spec,
        scratch_shapes=[pltpu.VMEM((MOE_GROUP, TM_MOE, D_MODEL), F32)])
    return pl.pallas_call(
        _expert_kernel, grid_spec=grid_spec,
        out_shape=jax.ShapeDtypeStruct((n_slots, D_MODEL), BF16),
        input_output_aliases=aliases,
        compiler_params=_params(("parallel", "arbitrary", "arbitrary"), 56), name="moe_experts",
    )(*args)


def _combine_ple_kernel(y0_ref, y1_ref, gate_ref, g_ref, h_ref, p_ref, ggn_ref, wpg_ref, wpp_ref,
                        gple_ref, out_ref):
    gate = gate_ref[...]
    f = y0_ref[...].astype(F32) * gate[:, 0:1] + y1_ref[...].astype(F32) * gate[:, 1:2]
    h2 = h_ref[...] + _rms(f, g_ref[...])
    out_ref[...] = _ple_update(h2, p_ref[...], ggn_ref[...], wpg_ref[...], wpp_ref[...], gple_ref[...])


def _combine_ple(y0, y1, gate, g, h, p, ggn, wpg, wpp, gple):
    m = h.shape[0]
    row = lambda i: (i, 0)
    const = lambda i: (0, 0)
    full = lambda x: pl.BlockSpec(x.shape, const)
    act = pl.BlockSpec((TM, D_MODEL), row)
    return pl.pallas_call(
        _combine_ple_kernel, grid=(m // TM,),
        in_specs=[act, act, pl.BlockSpec((TM, LANES), row), full(g), act,
                  pl.BlockSpec((None, TM, PLE_DIM), lambda i: (1, i, 0)), full(ggn), full(wpg), full(wpp), full(gple)],
        out_specs=act,
        out_shape=jax.ShapeDtypeStruct((m, D_MODEL), F32),
        compiler_params=_params(("parallel",), 40), name="combine_ple",
    )(y0, y1, gate, g, h, p, ggn, wpg, wpp, gple)


def _rows(x, idx):
    return x.at[idx].get(mode='promise_in_bounds')


def _moe_plan(idx, cnt, n):
    counts = cnt[0, :N_EXPERTS].astype(jnp.int32)
    padded = (counts + TM_MOE - 1) // TM_MOE * TM_MOE
    end_pad = jnp.cumsum(padded)
    start_pad = end_pad - padded
    experts = jnp.arange(N_EXPERTS, dtype=jnp.int32)
    start = jnp.sum(jnp.where(idx[:, 0:2, None] == experts, start_pad, 0), axis=-1)
    dest = start + idx[:, 2:4]
    n_slots = n * TOP_K + N_EXPERTS * TM_MOE
    tok = jnp.repeat(jnp.arange(n, dtype=jnp.int32), TOP_K)
    slot_tok = jnp.zeros((n_slots,), jnp.int32).at[dest.reshape(-1)].set(tok, unique_indices=True)
    tile_start = jnp.arange(n_slots // TM_MOE, dtype=jnp.int32) * TM_MOE
    tile_e = jnp.minimum(jnp.sum(tile_start[:, None] >= end_pad[None, :], axis=1),
                         N_EXPERTS - 1).astype(jnp.int32)
    tile_valid = (tile_start < end_pad[-1]).astype(jnp.int32)
    return dest, slot_tok, tile_e, tile_valid


def _rotary_tables(pos, dim, theta):
    exponent = jnp.arange(0, dim, 2, dtype=F32) / dim
    inv_freq = jnp.power(jnp.float32(theta), -exponent)
    ang = pos.astype(F32)[:, None] * inv_freq[None, :]
    return jnp.cos(ang), jnp.sin(ang)


def _tables(s):
    t = jnp.arange(s, dtype=jnp.int32)
    cr, sr = _rotary_tables(t // GRID_W, HEAD_DIM // 2, AXIAL_THETA)
    cc, sc = _rotary_tables(t % GRID_W, HEAD_DIM // 2, AXIAL_THETA)
    ca = jnp.tile(jnp.concatenate([cr, cr, cc, cc], axis=1), (1, 2))
    sa = jnp.tile(jnp.concatenate([-sr, sr, -sc, sc], axis=1), (1, 2))
    cm_, sm_ = _rotary_tables(t, MLA_ROPE_DIM, MLA_ROPE_THETA)
    one = jnp.ones((s, 1), F32)
    zero = jnp.zeros((s, 1), F32)
    cm = jnp.concatenate([jnp.tile(one, (1, 64)), cm_, cm_, jnp.tile(one, (1, 32))], axis=1)
    sm = jnp.concatenate([jnp.tile(zero, (1, 64)), -sm_, sm_, jnp.tile(zero, (1, 32))], axis=1)
    ckr = jnp.concatenate([cm_, cm_, jnp.tile(zero, (1, 96))], axis=1)
    skr = jnp.concatenate([-sm_, sm_, jnp.tile(zero, (1, 96))], axis=1)
    cp_, sp_ = _rotary_tables(t, PARTIAL_ROPE_DIM, ROPE_THETA)
    cp = jnp.tile(jnp.concatenate([cp_, cp_, jnp.tile(one, (1, 48))], axis=1), (1, 2))
    sp = jnp.tile(jnp.concatenate([-sp_, sp_, jnp.tile(zero, (1, 48))], axis=1), (1, 2))
    scale = HEAD_DIM ** -0.5 * LOG2_E
    ctab = jnp.stack([cp * scale, cp])
    stab = jnp.stack([sp * scale, sp])
    return ca, sa, cm, sm, ckr, skr, ctab, stab


_GQA_PERM = (0, 4, 1, 5, 2, 6, 3, 7)


def _l0_weights(w_in, w_uq, w_ukv, w_out):
    d = w_in.shape[0]
    q = w_in[:, :512].reshape(d, 8, 64)[:, _GQA_PERM, :].reshape(d, 512)
    win = jnp.concatenate([q, w_in[:, 512:], jnp.zeros((d, AB_IN_PAD - w_in.shape[1]), w_in.dtype)],
                          axis=1).astype(BF16)
    uq = w_uq.reshape(MLA_Q_RANK, MLA_HEADS, MLA_NOPE_DIM + MLA_ROPE_DIM)
    wuq = jnp.pad(uq, ((0, 0), (0, 0), (0, LANES - uq.shape[2]))).reshape(MLA_Q_RANK, -1).astype(BF16)
    ukv = w_ukv.reshape(MLA_KV_RANK, MLA_HEADS, MLA_NOPE_DIM + MLA_V_DIM)
    wk = jnp.pad(ukv[:, :, :MLA_NOPE_DIM], ((0, 0), (0, 0), (0, LANES - MLA_NOPE_DIM)))
    wk = wk.reshape(MLA_KV_RANK, -1).astype(BF16)
    wv = ukv[:, :, MLA_NOPE_DIM:].reshape(MLA_KV_RANK, -1).astype(BF16)
    src = jnp.arange(LANES)[:, None]
    dst = jnp.arange(MLA_HEADS * LANES)[None, :]
    wp = ((src < MLA_ROPE_DIM) & (dst % LANES == src + MLA_NOPE_DIM)).astype(BF16)
    oa = w_out[:512].reshape(8, 64, -1)[_GQA_PERM, :, :].reshape(512, -1)
    wout = jnp.concatenate([oa, w_out[512:]], axis=0).astype(BF16)
    return win, wuq, wk, wp, wv, wout


def kernel(x, p, mix_pre_g, mix_post_g, ffn_pre_g, ffn_post_g, ple_w_proj, ple_gate_norm_g, ple_w_gate, ple_post_g, ab_w_in, gqa_q_norm_g, gqa_k_norm_g, mla_q_norm_g, mla_w_uq, mla_kv_norm_g, mla_w_ukv, ab_w_out, ffn_w_gate, ffn_w_up, ffn_w_down, dil_w_qkv, dil_w_out, moe_w_router, moe_w_gate, moe_w_up, moe_w_down):
    b, s, d = x.shape
    m = b * s
    row = lambda g: g.reshape(1, -1).astype(F32)
    ca, sa, cm, sm, ckr, skr, ctab, stab = _tables(s)
    h = x.reshape(m, d)
    pf = p.reshape(p.shape[0], m, PLE_DIM)

    win, wuq, wk, wp, wv, wout = _l0_weights(ab_w_in[0], mla_w_uq[0], mla_w_ukv[0], ab_w_out[0])
    gq = jnp.tile(row(gqa_q_norm_g[0]), (1, 2))
    gk = jnp.tile(row(gqa_k_norm_g[0]), (1, 2))
    qa, ka, vat, qm, km, vmt = _l0_in(h, row(mix_pre_g[0]), win, gq, gk, ca, sa,
                                      row(mla_q_norm_g[0]), wuq, cm, sm,
                                      row(mla_kv_norm_g[0]), wk, wp, wv, ckr, skr, b, s)
    oa = _gqa_attn(qa, ka, vat, b, s)
    om = _mla_attn(qm, km, vmt, b, s)
    h = _l0_tail(oa, om, wout, row(mix_post_g[0]), h, row(ffn_pre_g[0]),
                 ffn_w_gate[0].astype(BF16), ffn_w_up[0].astype(BF16), ffn_w_down[0].astype(BF16),
                 row(ffn_post_g[0]), pf, row(ple_gate_norm_g[0]), ple_w_gate[0].astype(BF16),
                 ple_w_proj[0].astype(BF16), row(ple_post_g[0]))

    qkvs = _l1_qkv(h, row(mix_pre_g[1]), dil_w_qkv[0].astype(BF16), ctab, stab, b, s)
    o = _dil_attn(qkvs, b, s)
    wr = jnp.pad(moe_w_router[0].astype(BF16), ((0, 0), (0, LANES - N_EXPERTS)))
    h, a, idx, gate, cnt = _l1_router(o, dil_w_out[0].astype(BF16), row(mix_post_g[1]), h,
                                      row(ffn_pre_g[1]), wr)
    dest, slot_tok, tile_e, tile_valid = _moe_plan(idx, cnt, m)
    n_slots = slot_tok.shape[0]
    ys, first_tile = None, 0
    n_groups = n_slots // (TM_MOE * MOE_GROUP)
    sizes = [n_groups * share // sum(MOE_CHUNKS) for share in MOE_CHUNKS]
    sizes[-1] += n_groups - sum(sizes)
    for n_tiles in [size * MOE_GROUP for size in sizes if size > 0]:
        rows = slice(first_tile * TM_MOE, (first_tile + n_tiles) * TM_MOE)
        ys = _experts(tile_e, tile_valid, _rows(a, slot_tok[rows]), moe_w_gate[0], moe_w_up[0],
                      moe_w_down[0], ys, first_tile, n_slots)
        first_tile += n_tiles
    h = _combine_ple(_rows(ys, dest[:, 0]), _rows(ys, dest[:, 1]), gate, row(ffn_post_g[1]), h, pf,
                     row(ple_gate_norm_g[1]), ple_w_gate[1].astype(BF16), ple_w_proj[1].astype(BF16),
                     row(ple_post_g[1]))
    return h.reshape(b, s, d)
```

```python
import functools

import jax
import jax.numpy as jnp
from jax import lax
from jax.experimental import pallas as pl
from jax.experimental.pallas import tpu as pltpu

F32 = jnp.float32
BF16 = jnp.bfloat16

D_MODEL = 1024
GRID_W = 64
HEAD_DIM = 64
NORM_EPS = 1e-6
NEG_INF = -1e30
ROPE_THETA = 500000.0
PARTIAL_ROPE_DIM = HEAD_DIM // 4
GQA_Q_HEADS = 8
GQA_KV_HEADS = 2
AXIAL_THETA = 10000.0
MLA_HEADS = 8
MLA_Q_RANK = 256
MLA_KV_RANK = 128
MLA_NOPE_DIM = 64
MLA_ROPE_DIM = 32
MLA_V_DIM = 64
MLA_ROPE_THETA = 10000.0
DIL_CONFIGS = ((128, 1), (512, 4), (2048, 16))
DIL_GROUPS = len(DIL_CONFIGS)
DIL_HEADS = 8
DIL_HALF = 64
DIL_QB = 128
LOG2_E = 1.4426950408889634
DIL_IN_WIDTH = DIL_GROUPS * 3 * DIL_HEADS * HEAD_DIM
D_FF = 3584
N_EXPERTS = 8
TOP_K = 2
PLE_DIM = 256

LANES = 128
VMEM_BYTES = 64 * 1024 * 1024
MIB = 1024 * 1024

TM = 512
TQ = 1024
KEY_CHUNK = 256
V_ONES = 16
V_ROWS = LANES + V_ONES
TF = 512
TM_FFN = 1024
TAIL_ROWS = 256
TM_MOE = 1024
AB_IN_PAD = 1280
MOE_GROUP = 4
MOE_CHUNKS = (1, 2, 3, 3)


def _params(semantics, vmem_mib):
    return pltpu.CompilerParams(dimension_semantics=semantics,
                                vmem_limit_bytes=vmem_mib * MIB)


def _dot(a, b):
    return jnp.dot(a, b, preferred_element_type=F32)


def _rms(x, g):
    return x * lax.rsqrt(jnp.mean(x * x, axis=-1, keepdims=True) + NORM_EPS) * g


def _lane_iota(shape):
    return lax.broadcasted_iota(jnp.int32, shape, len(shape) - 1)


def _swap_halves(x, k):
    w = x.shape[-1]
    fwd = pltpu.roll(x, w - k, 1)
    bwd = pltpu.roll(x, k, 1)
    return jnp.where((_lane_iota(x.shape) % (2 * k)) < k, fwd, bwd)


def _rotate(x, c, s, k):
    return x * c + _swap_halves(x, k) * s


def _head_rms(x, g):
    lo = _lane_iota(x.shape) < HEAD_DIM
    x2 = x * x
    s_all = jnp.sum(x2, axis=-1, keepdims=True)
    s_lo = jnp.sum(jnp.where(lo, x2, 0.0), axis=-1, keepdims=True)
    ms = jnp.where(lo, s_lo, s_all - s_lo) * (1.0 / HEAD_DIM)
    return x * lax.rsqrt(ms + NORM_EPS) * g


def _l0_in_kernel(h_ref, gpre_ref, win_ref, gq_ref, gk_ref, ca_ref, sa_ref,
                  gmq_ref, wuq_ref, cm_ref, sm_ref, gmkv_ref, wk_ref, wp_ref, wv_ref,
                  ckr_ref, skr_ref,
                  qa_ref, ka_ref, vat_ref, qm_ref, km_ref, vmt_ref):
    a = _rms(h_ref[...], gpre_ref[...]).astype(BF16)
    z = _dot(a, win_ref[...])
    ca, sa = ca_ref[...], sa_ref[...]
    for j in range(4):
        xq = _head_rms(z[:, j * LANES:(j + 1) * LANES], gq_ref[...])
        qa_ref[:, j * LANES:(j + 1) * LANES] = (
            _rotate(xq, ca, sa, 16) * (HEAD_DIM ** -0.5 * LOG2_E)).astype(BF16)
    ka_ref[...] = _rotate(_head_rms(z[:, 512:640], gk_ref[...]), ca, sa, 16).astype(BF16)
    ones = jnp.ones((V_ONES, z.shape[0]), BF16)
    vat_ref[0:LANES, :] = z[:, 640:768].T.astype(BF16)
    vat_ref[LANES:, :] = ones
    cq = _rms(z[:, 768:1024], gmq_ref[...]).astype(BF16)
    qm = _dot(cq, wuq_ref[...])
    cm, sm = cm_ref[...], sm_ref[...]
    scale_m = (MLA_NOPE_DIM + MLA_ROPE_DIM) ** -0.5 * LOG2_E
    for hb in range(MLA_HEADS):
        x = qm[:, hb * LANES:(hb + 1) * LANES]
        qm_ref[:, hb * LANES:(hb + 1) * LANES] = (_rotate(x, cm, sm, 16) * scale_m).astype(BF16)
    ckv = _rms(z[:, 1024:1152], gmkv_ref[...]).astype(BF16)
    kr = _rotate(z[:, 1152:1280], ckr_ref[...], skr_ref[...], 16).astype(BF16)
    km = _dot(ckv, wk_ref[...]) + _dot(kr, wp_ref[...])
    km_ref[...] = km.astype(BF16)
    vm = _dot(ckv, wv_ref[...])
    for j in range(4):
        vmt_ref[j * V_ROWS:j * V_ROWS + LANES, :] = vm[:, j * LANES:(j + 1) * LANES].T.astype(BF16)
        vmt_ref[j * V_ROWS + LANES:(j + 1) * V_ROWS, :] = ones


def _l0_in(h, gpre, win, gq, gk, ca, sa, gmq, wuq, cm, sm, gmkv, wk, wp, wv, ckr, skr, b, s):
    m = h.shape[0]
    nt = s // TM
    row = lambda i: (i, 0)
    const = lambda i: (0, 0)
    tab = lambda i: (i % nt, 0)
    full = lambda a: pl.BlockSpec(a.shape, const)
    in_specs = [pl.BlockSpec((TM, D_MODEL), row), full(gpre), full(win), full(gq), full(gk),
                pl.BlockSpec((TM, LANES), tab), pl.BlockSpec((TM, LANES), tab),
                full(gmq), full(wuq), pl.BlockSpec((TM, LANES), tab), pl.BlockSpec((TM, LANES), tab),
                full(gmkv), full(wk), full(wp), full(wv),
                pl.BlockSpec((TM, LANES), tab), pl.BlockSpec((TM, LANES), tab)]
    tr = lambda i: (i // nt, 0, i % nt)
    out_specs = [pl.BlockSpec((TM, 512), row),
                 pl.BlockSpec((TM, LANES), row),
                 pl.BlockSpec((None, V_ROWS, TM), tr),
                 pl.BlockSpec((TM, 1024), row),
                 pl.BlockSpec((TM, 1024), row),
                 pl.BlockSpec((None, 4 * V_ROWS, TM), tr)]
    out_shape = [jax.ShapeDtypeStruct((m, 512), BF16),
                 jax.ShapeDtypeStruct((m, LANES), BF16),
                 jax.ShapeDtypeStruct((b, V_ROWS, s), BF16),
                 jax.ShapeDtypeStruct((m, 1024), BF16),
                 jax.ShapeDtypeStruct((m, 1024), BF16),
                 jax.ShapeDtypeStruct((b, 4 * V_ROWS, s), BF16)]
    return pl.pallas_call(
        _l0_in_kernel, grid=(m // TM,), in_specs=in_specs, out_specs=out_specs,
        out_shape=out_shape, compiler_params=_params(("parallel",), 48), name="l0_in",
    )(h, gpre, win, gq, gk, ca, sa, gmq, wuq, cm, sm, gmkv, wk, wp, wv, ckr, skr)


def _attend_pair(score_t, vt_ref, o_ref):
    n_chunks = vt_ref.shape[1] // KEY_CHUNK
    tq = o_ref.shape[0]
    keys = [slice(c * KEY_CHUNK, (c + 1) * KEY_CHUNK) for c in range(n_chunks)]
    parts = []
    s_next = score_t(keys[0])
    for c in range(n_chunks):
        s_t = s_next
        if c + 1 < n_chunks:
            s_next = score_t(keys[c + 1])
        m = jnp.max(s_t, axis=0, keepdims=True)
        parts.append((m, _dot(vt_ref[:, keys[c]], jnp.exp2(s_t - m).astype(BF16))))
    m_all = functools.reduce(jnp.maximum, [m for m, _ in parts])
    o_t = sum(o_c * jnp.exp2(m_c - m_all) for m_c, o_c in parts)
    o_t = o_t[0:LANES, :] / o_t[LANES:LANES + 1, :]
    first = lax.broadcasted_iota(jnp.int32, (LANES, tq), 0) < HEAD_DIM
    o_ref[...] = jnp.where(first, o_t[:, 0:tq], o_t[:, tq:]).T.astype(BF16)


def _transposed(q):
    return q.astype(F32).T.astype(BF16)


def _gqa_attn_kernel(q_ref, k_ref, vt_ref, o_ref):
    qt = _transposed(q_ref[...])
    first = lax.broadcasted_iota(jnp.int32, qt.shape, 0) < HEAD_DIM
    zero = jnp.zeros_like(qt)
    w = jnp.concatenate([jnp.where(first, qt, zero), jnp.where(first, zero, qt)], axis=1)
    _attend_pair(lambda keys: _dot(k_ref[keys, :], w), vt_ref, o_ref)


def _gqa_attn(qa, ka, vat, b, s):
    m = qa.shape[0]
    nq = s // TQ
    return pl.pallas_call(
        _gqa_attn_kernel, grid=(b, nq, 4),
        in_specs=[pl.BlockSpec((TQ, LANES), lambda bi, i, j: (bi * nq + i, j)),
                  pl.BlockSpec((s, LANES), lambda bi, i, j: (bi, 0)),
                  pl.BlockSpec((None, V_ROWS, s), lambda bi, i, j: (bi, 0, 0))],
        out_specs=pl.BlockSpec((TQ, LANES), lambda bi, i, j: (bi * nq + i, j)),
        out_shape=jax.ShapeDtypeStruct((m, 512), BF16),
        compiler_params=_params(("parallel", "parallel", "parallel"), 32), name="gqa_attn",
    )(qa, ka, vat)


def _mla_attn_kernel(q_ref, k_ref, vt_ref, o_ref):
    qt0 = _transposed(q_ref[:, 0:LANES])
    qt1 = _transposed(q_ref[:, LANES:2 * LANES])

    def score_t(keys):
        return jnp.concatenate([_dot(k_ref[keys, 0:LANES], qt0),
                                _dot(k_ref[keys, LANES:2 * LANES], qt1)], axis=1)

    _attend_pair(score_t, vt_ref, o_ref)


def _mla_attn(qm, km, vmt, b, s):
    m = qm.shape[0]
    nq = s // TQ
    return pl.pallas_call(
        _mla_attn_kernel, grid=(b, nq, 4),
        in_specs=[pl.BlockSpec((TQ, 2 * LANES), lambda bi, i, j: (bi * nq + i, j)),
                  pl.BlockSpec((s, 2 * LANES), lambda bi, i, j: (bi, j)),
                  pl.BlockSpec((None, V_ROWS, s), lambda bi, i, j: (bi, j, 0))],
        out_specs=pl.BlockSpec((TQ, LANES), lambda bi, i, j: (bi * nq + i, j)),
        out_shape=jax.ShapeDtypeStruct((m, 512), BF16),
        compiler_params=_params(("parallel", "parallel", "parallel"), 32), name="mla_attn",
    )(qm, km, vmt)


def _silu(x):
    return x * (1.0 / (1.0 + jnp.exp(-x)))


def _ple_update(h, p, ggn, wg, wp, gpost):
    gate = _dot(_rms(h, ggn).astype(BF16), wg)
    gate = 1.0 / (1.0 + jnp.exp(-gate))
    e = _dot(p.astype(BF16), wp) * gate
    return h + _rms(e, gpost)


def _l0_tail_kernel(oa_ref, om_ref, wo_ref, gmix_ref, h_ref, gpre_ref, wg_ref, wu_ref, wd_ref,
                    gpost_ref, p_ref, ggn_ref, wpg_ref, wpp_ref, gple_ref, out_ref, a_ref, acc_ref):
    j = pl.program_id(1)

    chunks = [slice(r, r + TAIL_ROWS) for r in range(0, out_ref.shape[0], TAIL_ROWS)]

    @pl.when(j == 0)
    def _():
        k = oa_ref.shape[1]
        for rows in chunks:
            y = _dot(oa_ref[rows, :], wo_ref[0:k, :]) + _dot(om_ref[rows, :], wo_ref[k:, :])
            h1 = h_ref[rows, :] + _rms(y, gmix_ref[...])
            out_ref[rows, :] = h1
            a_ref[rows, :] = _rms(h1, gpre_ref[...]).astype(BF16)
        acc_ref[...] = jnp.zeros_like(acc_ref)

    a = a_ref[...]
    mid = (_silu(_dot(a, wg_ref[...])) * _dot(a, wu_ref[...])).astype(BF16)
    acc_ref[...] += _dot(mid, wd_ref[...])

    @pl.when(j == pl.num_programs(1) - 1)
    def _():
        for rows in chunks:
            h2 = out_ref[rows, :] + _rms(acc_ref[rows, :], gpost_ref[...])
            out_ref[rows, :] = _ple_update(h2, p_ref[rows, :], ggn_ref[...], wpg_ref[...],
                                           wpp_ref[...], gple_ref[...])


def _l0_tail(oa, om, wo, gmix, h, gpre, wg, wu, wd, gpost, p, ggn, wpg, wpp, gple):
    m = h.shape[0]
    dff = wg.shape[1]
    row = lambda i, j: (i, 0)
    const = lambda i, j: (0, 0)
    vec = lambda g: pl.BlockSpec(g.shape, const)
    once = lambda w: pl.BlockSpec(w.shape, const, pipeline_mode=pl.Buffered(1))
    return pl.pallas_call(
        _l0_tail_kernel, grid=(m // TM_FFN, dff // TF),
        in_specs=[pl.BlockSpec((TM_FFN, oa.shape[1]), row), pl.BlockSpec((TM_FFN, om.shape[1]), row),
                  once(wo), vec(gmix), pl.BlockSpec((TM_FFN, D_MODEL), row), vec(gpre),
                  pl.BlockSpec((D_MODEL, TF), lambda i, j: (0, j)),
                  pl.BlockSpec((D_MODEL, TF), lambda i, j: (0, j)),
                  pl.BlockSpec((TF, D_MODEL), lambda i, j: (j, 0)),
                  vec(gpost), pl.BlockSpec((None, TM_FFN, PLE_DIM), lambda i, j: (0, i, 0)), vec(ggn), once(wpg), once(wpp),
                  vec(gple)],
        out_specs=pl.BlockSpec((TM_FFN, D_MODEL), row),
        out_shape=jax.ShapeDtypeStruct((m, D_MODEL), F32),
        scratch_shapes=[pltpu.VMEM((TM_FFN, D_MODEL), BF16), pltpu.VMEM((TM_FFN, D_MODEL), F32)],
        compiler_params=_params(("parallel", "arbitrary"), 56), name="l0_tail",
    )(oa, om, wo, gmix, h, gpre, wg, wu, wd, gpost, p, ggn, wpg, wpp, gple)


def _l1_qkv_kernel(h_ref, gpre_ref, w_ref, c_ref, s_ref, out0_ref, out1_ref, out2_ref, stage_ref):
    a = _rms(h_ref[...], gpre_ref[...]).astype(BF16)
    tm = a.shape[0]
    outs = (out0_ref, out1_ref, out2_ref)
    for blk in range(DIL_IN_WIDTH // 512):
        gi, kind = divmod(blk, 3)
        dil = DIL_CONFIGS[gi][1]
        z = _dot(a, w_ref[:, blk * 512:(blk + 1) * 512])
        if kind < 2:
            c, s = c_ref[kind], s_ref[kind]
            z = jnp.concatenate(
                [_rotate(z[:, sub * LANES:(sub + 1) * LANES], c, s, PARTIAL_ROPE_DIM // 2)
                 for sub in range(4)], axis=1)
        cols = slice(kind * 512, (kind + 1) * 512)
        if dil == 1:
            outs[gi][0, :, cols] = z.astype(BF16)
            continue
        for sub in range(4):
            stage = stage_ref.at[(blk % 2) * 4 + sub]
            stage[...] = z[:, sub * LANES:(sub + 1) * LANES]
            lanes = slice(kind * 512 + sub * LANES, kind * 512 + (sub + 1) * LANES)
            for r in range(dil):
                outs[gi][r, :, lanes] = stage[pl.ds(r, tm // dil, stride=dil), :].astype(BF16)


def _l1_qkv(h, gpre, w, ctab, stab, b, s):
    m = h.shape[0]
    nt = s // TM
    const = lambda i: (0, 0)
    out_specs, out_shape = [], []
    for _, dil in DIL_CONFIGS:
        out_specs.append(pl.BlockSpec((None, dil, TM // dil, 1536), lambda i: (i // nt, 0, i % nt, 0)))
        out_shape.append(jax.ShapeDtypeStruct((b, dil, s // dil, 1536), BF16))
    return pl.pallas_call(
        _l1_qkv_kernel, grid=(m // TM,),
        in_specs=[pl.BlockSpec((TM, D_MODEL), lambda i: (i, 0)),
                  pl.BlockSpec(gpre.shape, const),
                  pl.BlockSpec(w.shape, const, pipeline_mode=pl.Buffered(1)),
                  pl.BlockSpec((2, TM, LANES), lambda i: (0, i % nt, 0)),
                  pl.BlockSpec((2, TM, LANES), lambda i: (0, i % nt, 0))],
        out_specs=out_specs, out_shape=out_shape,
        scratch_shapes=[pltpu.VMEM((8, TM, LANES), F32)],
        compiler_params=_params(("parallel",), 48), name="l1_qkv",
    )(h, gpre, w, ctab, stab)


def _dil_attn_kernel(q0_ref, k0_ref, v0_ref, q1_ref, k1_ref, v1_ref, q2_ref, k2_ref, v2_ref, o_ref,
                     ra_ref, rl_ref, rm_ref, acc_ref, sum_ref, max_ref, bias_ref):
    lo = _lane_iota((DIL_QB, LANES)) < HEAD_DIM
    groups = ((q0_ref, k0_ref, v0_ref), (q1_ref, k1_ref, v1_ref), (q2_ref, k2_ref, v2_ref))
    for gi, (q_ref, k_ref, v_ref) in enumerate(groups):
        dil, length = q_ref.shape[0], q_ref.shape[1]
        width = min(DIL_QB + 2 * DIL_HALF, length)
        nblk = length // DIL_QB
        first = gi == 0
        ta, tl, tm_ = (acc_ref, sum_ref, max_ref) if first else (ra_ref, rl_ref, rm_ref)
        row = lax.broadcasted_iota(jnp.int32, (2 * DIL_QB, width), 0) % DIL_QB
        col = lax.broadcasted_iota(jnp.int32, (2 * DIL_QB, width), 1)
        rel = row - col
        ones = jnp.ones((width, LANES), BF16)
        for case in range(1 if nblk == 1 else 3):
            bias_ref[case, :, 0:width] = jnp.where(jnp.abs(rel + case * DIL_HALF) <= DIL_HALF,
                                                   0.0, NEG_INF)

        def body(it, carry, length=length, width=width, nblk=nblk, ta=ta, tl=tl, tm_=tm_,
                 ones=ones, q_ref=q_ref, k_ref=k_ref, v_ref=v_ref):
            r = it // nblk
            i = it % nblk
            q0 = pl.multiple_of(i * DIL_QB, DIL_QB)
            start = pl.multiple_of(jnp.clip(q0 - DIL_HALF, 0, length - width), DIL_HALF)
            qrow = pl.multiple_of(r * length + q0, DIL_HALF)
            q = q_ref[r, pl.ds(q0, DIL_QB), :]
            kw = k_ref[r, pl.ds(start, width), :]
            vw = jnp.concatenate([v_ref[r, pl.ds(start, width), :], ones], axis=1)
            zero = jnp.zeros_like(q)
            q2 = jnp.concatenate([jnp.where(lo, q, zero), jnp.where(lo, zero, q)], axis=0)
            sc = lax.dot_general(q2, kw, (((1,), (1,)), ((), ())), preferred_element_type=F32)
            sc = sc + bias_ref[lax.shift_right_logical(q0 - start, 6), :, 0:width]
            m = jnp.max(sc, axis=-1, keepdims=True)
            p = jnp.exp2(sc - m).astype(BF16)
            o2 = _dot(p, vw)
            ta[pl.ds(qrow, DIL_QB), :] = jnp.where(lo, o2[:DIL_QB, 0:LANES], o2[DIL_QB:, 0:LANES])
            tl[pl.ds(qrow, DIL_QB), :] = jnp.where(lo, o2[:DIL_QB, LANES:], o2[DIL_QB:, LANES:])
            tm_[pl.ds(qrow, DIL_QB), :] = jnp.where(lo, m[:DIL_QB], m[DIL_QB:])
            return carry

        lax.fori_loop(0, dil * nblk, body, 0, unroll=8)
        if first:
            continue
        for r in range(dil):
            rows = pl.ds(r, length, stride=dil)
            blk = slice(r * length, (r + 1) * length)
            m_old, m_blk = max_ref[rows, :], rm_ref[blk, :]
            m_new = jnp.maximum(m_old, m_blk)
            w_old, w_blk = jnp.exp2(m_old - m_new), jnp.exp2(m_blk - m_new)
            acc_ref[rows, :] = acc_ref[rows, :] * w_old + ra_ref[blk, :] * w_blk
            sum_ref[rows, :] = sum_ref[rows, :] * w_old + rl_ref[blk, :] * w_blk
            max_ref[rows, :] = m_new
    o_ref[...] = (acc_ref[...] / sum_ref[...]).astype(BF16)


def _dil_attn(qkvs, b, s):
    in_specs, args = [], []
    for x in qkvs:
        for j in range(3):
            in_specs.append(pl.BlockSpec((None,) + x.shape[1:3] + (LANES,),
                                         lambda bi, hp, j=j: (bi, 0, 0, 4 * j + hp)))
            args.append(x)
    stat = pltpu.VMEM((s, LANES), F32)
    return pl.pallas_call(
        _dil_attn_kernel, grid=(b, 4), in_specs=in_specs,
        out_specs=pl.BlockSpec((s, LANES), lambda bi, hp: (bi, hp)),
        out_shape=jax.ShapeDtypeStruct((b * s, 512), BF16),
        scratch_shapes=[stat] * 6 + [pltpu.VMEM((3, 2 * DIL_QB, DIL_QB + 2 * DIL_HALF), F32)],
        compiler_params=_params(("parallel", "parallel"), 40), name="dil_attn",
    )(*args)


def _l1_router_kernel(o_ref, wo_ref, gmix_ref, h_ref, gpre_ref, wr_ref,
                      h1_ref, a_ref, idx_ref, gate_ref, cnt_ref, carry_ref):
    @pl.when(pl.program_id(0) == 0)
    def _():
        carry_ref[...] = jnp.zeros_like(carry_ref)

    h1 = h_ref[...] + _rms(_dot(o_ref[...], wo_ref[...]), gmix_ref[...])
    h1_ref[...] = h1
    a = _rms(h1, gpre_ref[...]).astype(BF16)
    a_ref[...] = a
    logits = _dot(a, wr_ref[...])
    tm = logits.shape[0]
    lane = _lane_iota(logits.shape)
    lanef = lane.astype(F32)
    lg = jnp.where(lane < N_EXPERTS, logits, -jnp.inf)
    m1 = jnp.max(lg, axis=-1, keepdims=True)
    i1 = jnp.min(jnp.where(lg == m1, lanef, float(LANES)), axis=-1, keepdims=True)
    lg2 = jnp.where(lanef == i1, -jnp.inf, lg)
    m2 = jnp.max(lg2, axis=-1, keepdims=True)
    i2 = jnp.min(jnp.where(lg2 == m2, lanef, float(LANES)), axis=-1, keepdims=True)
    t = jnp.exp(m2 - m1)
    g1 = 1.0 / (1.0 + t)
    g2 = t / (1.0 + t)
    hit1 = lanef == i1
    hit2 = lanef == i2
    onehot = jnp.where(hit1 | hit2, 1.0, 0.0)
    r = lax.broadcasted_iota(jnp.int32, (tm, tm), 0)
    c = lax.broadcasted_iota(jnp.int32, (tm, tm), 1)
    tri = jnp.where(c < r, 1.0, 0.0).astype(BF16)
    before = _dot(tri, onehot.astype(BF16)) + carry_ref[...]
    rank1 = jnp.sum(jnp.where(hit1, before, 0.0), axis=-1, keepdims=True)
    rank2 = jnp.sum(jnp.where(hit2, before, 0.0), axis=-1, keepdims=True)
    info = jnp.where(lane == 0, i1, jnp.where(lane == 1, i2, jnp.where(lane == 2, rank1, rank2)))
    idx_ref[...] = info.astype(jnp.int32)
    gate_ref[...] = jnp.where(lane == 0, g1, g2)
    carry_ref[...] += jnp.sum(onehot, axis=0, keepdims=True)
    cnt_ref[...] = carry_ref[...]


def _l1_router(o, wo, gmix, h, gpre, wr):
    m = h.shape[0]
    row = lambda i: (i, 0)
    const = lambda i: (0, 0)
    full = lambda x: pl.BlockSpec(x.shape, const)
    return pl.pallas_call(
        _l1_router_kernel, grid=(m // TM,),
        in_specs=[pl.BlockSpec((TM, o.shape[1]), row), full(wo), full(gmix),
                  pl.BlockSpec((TM, D_MODEL), row), full(gpre), full(wr)],
        out_specs=[pl.BlockSpec((TM, D_MODEL), row), pl.BlockSpec((TM, D_MODEL), row),
                   pl.BlockSpec((TM, LANES), row), pl.BlockSpec((TM, LANES), row),
                   pl.BlockSpec((1, LANES), const)],
        out_shape=[jax.ShapeDtypeStruct((m, D_MODEL), F32),
                   jax.ShapeDtypeStruct((m, D_MODEL), BF16),
                   jax.ShapeDtypeStruct((m, LANES), jnp.int32),
                   jax.ShapeDtypeStruct((m, LANES), F32),
                   jax.ShapeDtypeStruct((1, LANES), F32)],
        scratch_shapes=[pltpu.VMEM((1, LANES), F32)],
        compiler_params=_params(("arbitrary",), 32), name="l1_router",
    )(o, wo, gmix, h, gpre, wr)


def _expert_kernel(te_ref, tv_ref, x_ref, wg_ref, wu_ref, wd_ref, *refs):
    y_ref, acc_ref = refs[-2:]
    j = pl.program_id(1)
    t = pl.program_id(2)
    last = pl.num_programs(1) - 1
    valid = tv_ref[pl.program_id(0) * MOE_GROUP + t] > 0

    @pl.when(valid & (j == 0))
    def _():
        acc_ref[t] = jnp.zeros(acc_ref.shape[1:], F32)

    @pl.when(valid)
    def _():
        x = x_ref[...]
        gate = _dot(x, wg_ref[...].astype(BF16))
        up = _dot(x, wu_ref[...].astype(BF16))
        acc_ref[t] += _dot((_silu(gate) * up).astype(BF16), wd_ref[...].astype(BF16))

    @pl.when(valid & (j == last))
    def _():
        y_ref[...] = acc_ref[t].astype(BF16)

    @pl.when(jnp.logical_not(valid) & (j == last))
    def _():
        y_ref[...] = jnp.zeros_like(y_ref)


def _experts(tile_e, tile_valid, xs, wg, wu, wd, ys, first_tile, n_slots):
    n_tiles = xs.shape[0] // TM_MOE
    n_j = wg.shape[2] // TF
    te = tile_e[first_tile:first_tile + n_tiles]
    tv = tile_valid[first_tile:first_tile + n_tiles]
    tile = lambda g, t: g * MOE_GROUP + t
    in_specs = [
        pl.BlockSpec((TM_MOE, D_MODEL), lambda g, j, t, te, tv: (tile(g, t), 0)),
        pl.BlockSpec((None, D_MODEL, TF),
                     lambda g, j, t, te, tv: (te[tile(g, t)], 0, j * tv[tile(g, t)])),
        pl.BlockSpec((None, D_MODEL, TF),
                     lambda g, j, t, te, tv: (te[tile(g, t)], 0, j * tv[tile(g, t)])),
        pl.BlockSpec((None, TF, D_MODEL),
                     lambda g, j, t, te, tv: (te[tile(g, t)], j * tv[tile(g, t)], 0))]
    args = [te, tv, xs, wg, wu, wd]
    aliases = {}
    if ys is not None:
        in_specs.append(pl.BlockSpec(memory_space=pl.ANY))
        aliases = {len(args): 0}
        args.append(ys)
    out_spec = pl.BlockSpec(
        (TM_MOE, D_MODEL),
        lambda g, j, t, te, tv: (first_tile + g * MOE_GROUP + jnp.where(j == n_j - 1, t, 0), 0))
    grid_spec = pltpu.PrefetchScalarGridSpec(
        num_scalar_prefetch=2, grid=(n_tiles // MOE_GROUP, n_j, MOE_GROUP), in_specs=in_specs,
        out_specs=out_spec,
        scratch_shapes=[pltpu.VMEM((MOE_GROUP, TM_MOE, D_MODEL), F32)])
    return pl.pallas_call(
        _expert_kernel, grid_spec=grid_spec,
        out_shape=jax.ShapeDtypeStruct((n_slots, D_MODEL), BF16),
        input_output_aliases=aliases,
        compiler_params=_params(("parallel", "arbitrary", "arbitrary"), 56), name="moe_experts",
    )(*args)


def _combine_ple_kernel(y0_ref, y1_ref, gate_ref, g_ref, h_ref, p_ref, ggn_ref, wpg_ref, wpp_ref,
                        gple_ref, out_ref):
    gate = gate_ref[...]
    f = y0_ref[...].astype(F32) * gate[:, 0:1] + y1_ref[...].astype(F32) * gate[:, 1:2]
    h2 = h_ref[...] + _rms(f, g_ref[...])
    out_ref[...] = _ple_update(h2, p_ref[...], ggn_ref[...], wpg_ref[...], wpp_ref[...], gple_ref[...])


def _combine_ple(y01, gate, g, h, p, ggn, wpg, wpp, gple):
    m = h.shape[0]
    nt = m // TM
    row = lambda i: (i, 0)
    const = lambda i: (0, 0)
    full = lambda x: pl.BlockSpec(x.shape, const)
    act = pl.BlockSpec((TM, D_MODEL), row)
    return pl.pallas_call(
        _combine_ple_kernel, grid=(nt,),
        in_specs=[act, pl.BlockSpec((TM, D_MODEL), lambda i: (nt + i, 0)),
                  pl.BlockSpec((TM, LANES), row), full(g), act,
                  pl.BlockSpec((None, TM, PLE_DIM), lambda i: (1, i, 0)), full(ggn), full(wpg),
                  full(wpp), full(gple)],
        out_specs=act,
        out_shape=jax.ShapeDtypeStruct((m, D_MODEL), F32),
        compiler_params=_params(("parallel",), 40), name="combine_ple",
    )(y01, y01, gate, g, h, p, ggn, wpg, wpp, gple)


def _rows(x, idx):
    return x.at[idx].get(mode='promise_in_bounds')


def _moe_plan(idx, cnt, n):
    counts = cnt[0, :N_EXPERTS].astype(jnp.int32)
    padded = (counts + TM_MOE - 1) // TM_MOE * TM_MOE
    end_pad = jnp.cumsum(padded)
    start_pad = end_pad - padded
    experts = jnp.arange(N_EXPERTS, dtype=jnp.int32)
    first = lambda e: jnp.sum(jnp.where(e[:, None] == experts, start_pad, 0), axis=-1)
    dest = (first(idx[:, 0]) + idx[:, 2], first(idx[:, 1]) + idx[:, 3])
    n_slots = n * TOP_K + N_EXPERTS * TM_MOE
    tok = jnp.arange(n, dtype=jnp.int32)
    slot_tok = jnp.zeros((n_slots,), jnp.int32).at[jnp.concatenate(dest)].set(
        jnp.concatenate([tok, tok]), unique_indices=True)
    tile_start = jnp.arange(n_slots // TM_MOE, dtype=jnp.int32) * TM_MOE
    tile_e = jnp.minimum(jnp.sum(tile_start[:, None] >= end_pad[None, :], axis=1),
                         N_EXPERTS - 1).astype(jnp.int32)
    tile_valid = (tile_start < end_pad[-1]).astype(jnp.int32)
    return dest, slot_tok, tile_e, tile_valid


def _rotary_tables(pos, dim, theta):
    exponent = jnp.arange(0, dim, 2, dtype=F32) / dim
    inv_freq = jnp.power(jnp.float32(theta), -exponent)
    ang = pos.astype(F32)[:, None] * inv_freq[None, :]
    return jnp.cos(ang), jnp.sin(ang)


def _tables(s):
    t = jnp.arange(s, dtype=jnp.int32)
    cr, sr = _rotary_tables(t // GRID_W, HEAD_DIM // 2, AXIAL_THETA)
    cc, sc = _rotary_tables(t % GRID_W, HEAD_DIM // 2, AXIAL_THETA)
    ca = jnp.tile(jnp.concatenate([cr, cr, cc, cc], axis=1), (1, 2))
    sa = jnp.tile(jnp.concatenate([-sr, sr, -sc, sc], axis=1), (1, 2))
    cm_, sm_ = _rotary_tables(t, MLA_ROPE_DIM, MLA_ROPE_THETA)
    one = jnp.ones((s, 1), F32)
    zero = jnp.zeros((s, 1), F32)
    cm = jnp.concatenate([jnp.tile(one, (1, 64)), cm_, cm_, jnp.tile(one, (1, 32))], axis=1)
    sm = jnp.concatenate([jnp.tile(zero, (1, 64)), -sm_, sm_, jnp.tile(zero, (1, 32))], axis=1)
    ckr = jnp.concatenate([cm_, cm_, jnp.tile(zero, (1, 96))], axis=1)
    skr = jnp.concatenate([-sm_, sm_, jnp.tile(zero, (1, 96))], axis=1)
    cp_, sp_ = _rotary_tables(t, PARTIAL_ROPE_DIM, ROPE_THETA)
    cp = jnp.tile(jnp.concatenate([cp_, cp_, jnp.tile(one, (1, 48))], axis=1), (1, 2))
    sp = jnp.tile(jnp.concatenate([-sp_, sp_, jnp.tile(zero, (1, 48))], axis=1), (1, 2))
    scale = HEAD_DIM ** -0.5 * LOG2_E
    ctab = jnp.stack([cp * scale, cp])
    stab = jnp.stack([sp * scale, sp])
    return ca, sa, cm, sm, ckr, skr, ctab, stab


_GQA_PERM = (0, 4, 1, 5, 2, 6, 3, 7)


def _l0_weights(w_in, w_uq, w_ukv, w_out):
    d = w_in.shape[0]
    q = w_in[:, :512].reshape(d, 8, 64)[:, _GQA_PERM, :].reshape(d, 512)
    win = jnp.concatenate([q, w_in[:, 512:], jnp.zeros((d, AB_IN_PAD - w_in.shape[1]), w_in.dtype)],
                          axis=1).astype(BF16)
    uq = w_uq.reshape(MLA_Q_RANK, MLA_HEADS, MLA_NOPE_DIM + MLA_ROPE_DIM)
    wuq = jnp.pad(uq, ((0, 0), (0, 0), (0, LANES - uq.shape[2]))).reshape(MLA_Q_RANK, -1).astype(BF16)
    ukv = w_ukv.reshape(MLA_KV_RANK, MLA_HEADS, MLA_NOPE_DIM + MLA_V_DIM)
    wk = jnp.pad(ukv[:, :, :MLA_NOPE_DIM], ((0, 0), (0, 0), (0, LANES - MLA_NOPE_DIM)))
    wk = wk.reshape(MLA_KV_RANK, -1).astype(BF16)
    wv = ukv[:, :, MLA_NOPE_DIM:].reshape(MLA_KV_RANK, -1).astype(BF16)
    src = jnp.arange(LANES)[:, None]
    dst = jnp.arange(MLA_HEADS * LANES)[None, :]
    wp = ((src < MLA_ROPE_DIM) & (dst % LANES == src + MLA_NOPE_DIM)).astype(BF16)
    oa = w_out[:512].reshape(8, 64, -1)[_GQA_PERM, :, :].reshape(512, -1)
    wout = jnp.concatenate([oa, w_out[512:]], axis=0).astype(BF16)
    return win, wuq, wk, wp, wv, wout


def kernel(x, p, mix_pre_g, mix_post_g, ffn_pre_g, ffn_post_g, ple_w_proj, ple_gate_norm_g, ple_w_gate, ple_post_g, ab_w_in, gqa_q_norm_g, gqa_k_norm_g, mla_q_norm_g, mla_w_uq, mla_kv_norm_g, mla_w_ukv, ab_w_out, ffn_w_gate, ffn_w_up, ffn_w_down, dil_w_qkv, dil_w_out, moe_w_router, moe_w_gate, moe_w_up, moe_w_down):
    b, s, d = x.shape
    m = b * s
    row = lambda g: g.reshape(1, -1).astype(F32)
    ca, sa, cm, sm, ckr, skr, ctab, stab = _tables(s)
    h = x.reshape(m, d)
    pf = p.reshape(p.shape[0], m, PLE_DIM)

    win, wuq, wk, wp, wv, wout = _l0_weights(ab_w_in[0], mla_w_uq[0], mla_w_ukv[0], ab_w_out[0])
    gq = jnp.tile(row(gqa_q_norm_g[0]), (1, 2))
    gk = jnp.tile(row(gqa_k_norm_g[0]), (1, 2))
    qa, ka, vat, qm, km, vmt = _l0_in(h, row(mix_pre_g[0]), win, gq, gk, ca, sa,
                                      row(mla_q_norm_g[0]), wuq, cm, sm,
                                      row(mla_kv_norm_g[0]), wk, wp, wv, ckr, skr, b, s)
    oa = _gqa_attn(qa, ka, vat, b, s)
    om = _mla_attn(qm, km, vmt, b, s)
    h = _l0_tail(oa, om, wout, row(mix_post_g[0]), h, row(ffn_pre_g[0]),
                 ffn_w_gate[0].astype(BF16), ffn_w_up[0].astype(BF16), ffn_w_down[0].astype(BF16),
                 row(ffn_post_g[0]), pf, row(ple_gate_norm_g[0]), ple_w_gate[0].astype(BF16),
                 ple_w_proj[0].astype(BF16), row(ple_post_g[0]))

    qkvs = _l1_qkv(h, row(mix_pre_g[1]), dil_w_qkv[0].astype(BF16), ctab, stab, b, s)
    o = _dil_attn(qkvs, b, s)
    wr = jnp.pad(moe_w_router[0].astype(BF16), ((0, 0), (0, LANES - N_EXPERTS)))
    h, a, idx, gate, cnt = _l1_router(o, dil_w_out[0].astype(BF16), row(mix_post_g[1]), h,
                                      row(ffn_pre_g[1]), wr)
    dest, slot_tok, tile_e, tile_valid = _moe_plan(idx, cnt, m)
    n_slots = slot_tok.shape[0]
    ys, first_tile = None, 0
    n_groups = n_slots // (TM_MOE * MOE_GROUP)
    sizes = [n_groups * share // sum(MOE_CHUNKS) for share in MOE_CHUNKS]
    sizes[-1] += n_groups - sum(sizes)
    for n_tiles in [size * MOE_GROUP for size in sizes if size > 0]:
        rows = slice(first_tile * TM_MOE, (first_tile + n_tiles) * TM_MOE)
        ys = _experts(tile_e, tile_valid, _rows(a, slot_tok[rows]), moe_w_gate[0], moe_w_up[0],
                      moe_w_down[0], ys, first_tile, n_slots)
        first_tile += n_tiles
    h = _combine_ple(_rows(ys, jnp.concatenate(dest)), gate, row(ffn_post_g[1]), h, pf,
                     row(ple_gate_norm_g[1]), ple_w_gate[1].astype(BF16), ple_w_proj[1].astype(BF16),
                     row(ple_post_g[1]))
    return h.reshape(b, s, d)
```

```python
import functools

import jax
import jax.numpy as jnp
from jax import lax
from jax.experimental import pallas as pl
from jax.experimental.pallas import tpu as pltpu

F32 = jnp.float32
BF16 = jnp.bfloat16

D_MODEL = 1024
GRID_W = 64
HEAD_DIM = 64
NORM_EPS = 1e-6
NEG_INF = -1e30
ROPE_THETA = 500000.0
PARTIAL_ROPE_DIM = HEAD_DIM // 4
GQA_Q_HEADS = 8
GQA_KV_HEADS = 2
AXIAL_THETA = 10000.0
MLA_HEADS = 8
MLA_Q_RANK = 256
MLA_KV_RANK = 128
MLA_NOPE_DIM = 64
MLA_ROPE_DIM = 32
MLA_V_DIM = 64
MLA_ROPE_THETA = 10000.0
DIL_CONFIGS = ((128, 1), (512, 4), (2048, 16))
DIL_GROUPS = len(DIL_CONFIGS)
DIL_HEADS = 8
DIL_HALF = 64
DIL_QB = 128
LOG2_E = 1.4426950408889634
DIL_IN_WIDTH = DIL_GROUPS * 3 * DIL_HEADS * HEAD_DIM
D_FF = 3584
N_EXPERTS = 8
TOP_K = 2
PLE_DIM = 256

LANES = 128
VMEM_BYTES = 64 * 1024 * 1024
MIB = 1024 * 1024

TM = 512
TQ = 1024
KEY_CHUNK = 256
V_ONES = 16
V_ROWS = LANES + V_ONES
TF = 512
TM_FFN = 1024
TAIL_ROWS = 256
TM_MOE = 1024
AB_IN_PAD = 1280
MOE_GROUP = 4
MOE_CHUNKS = (1, 2, 3, 3)


def _params(semantics, vmem_mib):
    return pltpu.CompilerParams(dimension_semantics=semantics,
                                vmem_limit_bytes=vmem_mib * MIB)


def _dot(a, b):
    return jnp.dot(a, b, preferred_element_type=F32)


def _rms(x, g):
    return x * lax.rsqrt(jnp.mean(x * x, axis=-1, keepdims=True) + NORM_EPS) * g


def _lane_iota(shape):
    return lax.broadcasted_iota(jnp.int32, shape, len(shape) - 1)


def _swap_halves(x, k):
    w = x.shape[-1]
    fwd = pltpu.roll(x, w - k, 1)
    bwd = pltpu.roll(x, k, 1)
    return jnp.where((_lane_iota(x.shape) % (2 * k)) < k, fwd, bwd)


def _rotate(x, c, s, k):
    return x * c + _swap_halves(x, k) * s


def _head_rms(x, g):
    lo = _lane_iota(x.shape) < HEAD_DIM
    x2 = x * x
    s_all = jnp.sum(x2, axis=-1, keepdims=True)
    s_lo = jnp.sum(jnp.where(lo, x2, 0.0), axis=-1, keepdims=True)
    ms = jnp.where(lo, s_lo, s_all - s_lo) * (1.0 / HEAD_DIM)
    return x * lax.rsqrt(ms + NORM_EPS) * g


def _l0_in_kernel(h_ref, gpre_ref, win_ref, gq_ref, gk_ref, ca_ref, sa_ref,
                  gmq_ref, wuq_ref, cm_ref, sm_ref, gmkv_ref, wk_ref, wp_ref, wv_ref,
                  ckr_ref, skr_ref,
                  qa_ref, ka_ref, vat_ref, qm_ref, km_ref, vmt_ref):
    a = _rms(h_ref[...], gpre_ref[...]).astype(BF16)
    z = _dot(a, win_ref[...])
    ca, sa = ca_ref[...], sa_ref[...]
    for j in range(4):
        xq = _head_rms(z[:, j * LANES:(j + 1) * LANES], gq_ref[...])
        qa_ref[:, j * LANES:(j + 1) * LANES] = (
            _rotate(xq, ca, sa, 16) * (HEAD_DIM ** -0.5 * LOG2_E)).astype(BF16)
    ka_ref[...] = _rotate(_head_rms(z[:, 512:640], gk_ref[...]), ca, sa, 16).astype(BF16)
    ones = jnp.ones((V_ONES, z.shape[0]), BF16)
    vat_ref[0:LANES, :] = z[:, 640:768].T.astype(BF16)
    vat_ref[LANES:, :] = ones
    cq = _rms(z[:, 768:1024], gmq_ref[...]).astype(BF16)
    qm = _dot(cq, wuq_ref[...])
    cm, sm = cm_ref[...], sm_ref[...]
    scale_m = (MLA_NOPE_DIM + MLA_ROPE_DIM) ** -0.5 * LOG2_E
    for hb in range(MLA_HEADS):
        x = qm[:, hb * LANES:(hb + 1) * LANES]
        qm_ref[:, hb * LANES:(hb + 1) * LANES] = (_rotate(x, cm, sm, 16) * scale_m).astype(BF16)
    ckv = _rms(z[:, 1024:1152], gmkv_ref[...]).astype(BF16)
    kr = _rotate(z[:, 1152:1280], ckr_ref[...], skr_ref[...], 16).astype(BF16)
    km = _dot(ckv, wk_ref[...]) + _dot(kr, wp_ref[...])
    km_ref[...] = km.astype(BF16)
    vm = _dot(ckv, wv_ref[...])
    for j in range(4):
        vmt_ref[j * V_ROWS:j * V_ROWS + LANES, :] = vm[:, j * LANES:(j + 1) * LANES].T.astype(BF16)
        vmt_ref[j * V_ROWS + LANES:(j + 1) * V_ROWS, :] = ones


def _l0_in(h, gpre, win, gq, gk, ca, sa, gmq, wuq, cm, sm, gmkv, wk, wp, wv, ckr, skr, b, s):
    m = h.shape[0]
    nt = s // TM
    row = lambda i: (i, 0)
    const = lambda i: (0, 0)
    tab = lambda i: (i % nt, 0)
    full = lambda a: pl.BlockSpec(a.shape, const)
    in_specs = [pl.BlockSpec((TM, D_MODEL), row), full(gpre), full(win), full(gq), full(gk),
                pl.BlockSpec((TM, LANES), tab), pl.BlockSpec((TM, LANES), tab),
                full(gmq), full(wuq), pl.BlockSpec((TM, LANES), tab), pl.BlockSpec((TM, LANES), tab),
                full(gmkv), full(wk), full(wp), full(wv),
                pl.BlockSpec((TM, LANES), tab), pl.BlockSpec((TM, LANES), tab)]
    tr = lambda i: (i // nt, 0, i % nt)
    out_specs = [pl.BlockSpec((TM, 512), row),
                 pl.BlockSpec((TM, LANES), row),
                 pl.BlockSpec((None, V_ROWS, TM), tr),
                 pl.BlockSpec((TM, 1024), row),
                 pl.BlockSpec((TM, 1024), row),
                 pl.BlockSpec((None, 4 * V_ROWS, TM), tr)]
    out_shape = [jax.ShapeDtypeStruct((m, 512), BF16),
                 jax.ShapeDtypeStruct((m, LANES), BF16),
                 jax.ShapeDtypeStruct((b, V_ROWS, s), BF16),
                 jax.ShapeDtypeStruct((m, 1024), BF16),
                 jax.ShapeDtypeStruct((m, 1024), BF16),
                 jax.ShapeDtypeStruct((b, 4 * V_ROWS, s), BF16)]
    return pl.pallas_call(
        _l0_in_kernel, grid=(m // TM,), in_specs=in_specs, out_specs=out_specs,
        out_shape=out_shape, compiler_params=_params(("parallel",), 48), name="l0_in",
    )(h, gpre, win, gq, gk, ca, sa, gmq, wuq, cm, sm, gmkv, wk, wp, wv, ckr, skr)


def _attend_pair(score_t, vt_ref, o_ref):
    n_chunks = vt_ref.shape[1] // KEY_CHUNK
    tq = o_ref.shape[0]
    keys = [slice(c * KEY_CHUNK, (c + 1) * KEY_CHUNK) for c in range(n_chunks)]
    parts = []
    s_next = score_t(keys[0])
    for c in range(n_chunks):
        s_t = s_next
        if c + 1 < n_chunks:
            s_next = score_t(keys[c + 1])
        m = jnp.max(s_t, axis=0, keepdims=True)
        parts.append((m, _dot(vt_ref[:, keys[c]], jnp.exp2(s_t - m).astype(BF16))))
    m_all = functools.reduce(jnp.maximum, [m for m, _ in parts])
    o_t = sum(o_c * jnp.exp2(m_c - m_all) for m_c, o_c in parts)
    o_t = o_t[0:LANES, :] / o_t[LANES:LANES + 1, :]
    first = lax.broadcasted_iota(jnp.int32, (LANES, tq), 0) < HEAD_DIM
    o_ref[...] = jnp.where(first, o_t[:, 0:tq], o_t[:, tq:]).T.astype(BF16)


def _transposed(q):
    return q.astype(F32).T.astype(BF16)


def _gqa_attn_kernel(q_ref, k_ref, vt_ref, o_ref):
    qt = _transposed(q_ref[...])
    first = lax.broadcasted_iota(jnp.int32, qt.shape, 0) < HEAD_DIM
    zero = jnp.zeros_like(qt)
    w = jnp.concatenate([jnp.where(first, qt, zero), jnp.where(first, zero, qt)], axis=1)
    _attend_pair(lambda keys: _dot(k_ref[keys, :], w), vt_ref, o_ref)


def _gqa_attn(qa, ka, vat, b, s):
    m = qa.shape[0]
    nq = s // TQ
    return pl.pallas_call(
        _gqa_attn_kernel, grid=(b, nq, 4),
        in_specs=[pl.BlockSpec((TQ, LANES), lambda bi, i, j: (bi * nq + i, j)),
                  pl.BlockSpec((s, LANES), lambda bi, i, j: (bi, 0)),
                  pl.BlockSpec((None, V_ROWS, s), lambda bi, i, j: (bi, 0, 0))],
        out_specs=pl.BlockSpec((TQ, LANES), lambda bi, i, j: (bi * nq + i, j)),
        out_shape=jax.ShapeDtypeStruct((m, 512), BF16),
        compiler_params=_params(("parallel", "parallel", "parallel"), 32), name="gqa_attn",
    )(qa, ka, vat)


def _mla_attn_kernel(q_ref, k_ref, vt_ref, o_ref):
    qt0 = _transposed(q_ref[:, 0:LANES])
    qt1 = _transposed(q_ref[:, LANES:2 * LANES])

    def score_t(keys):
        return jnp.concatenate([_dot(k_ref[keys, 0:LANES], qt0),
                                _dot(k_ref[keys, LANES:2 * LANES], qt1)], axis=1)

    _attend_pair(score_t, vt_ref, o_ref)


def _mla_attn(qm, km, vmt, b, s):
    m = qm.shape[0]
    nq = s // TQ
    return pl.pallas_call(
        _mla_attn_kernel, grid=(b, nq, 4),
        in_specs=[pl.BlockSpec((TQ, 2 * LANES), lambda bi, i, j: (bi * nq + i, j)),
                  pl.BlockSpec((s, 2 * LANES), lambda bi, i, j: (bi, j)),
                  pl.BlockSpec((None, V_ROWS, s), lambda bi, i, j: (bi, j, 0))],
        out_specs=pl.BlockSpec((TQ, LANES), lambda bi, i, j: (bi * nq + i, j)),
        out_shape=jax.ShapeDtypeStruct((m, 512), BF16),
        compiler_params=_params(("parallel", "parallel", "parallel"), 32), name="mla_attn",
    )(qm, km, vmt)


def _silu(x):
    return x * (1.0 / (1.0 + jnp.exp(-x)))


def _ple_update(h, p, ggn, wg, wp, gpost):
    gate = _dot(_rms(h, ggn).astype(BF16), wg)
    gate = 1.0 / (1.0 + jnp.exp(-gate))
    e = _dot(p.astype(BF16), wp) * gate
    return h + _rms(e, gpost)


def _l0_tail_kernel(oa_ref, om_ref, wo_ref, gmix_ref, h_ref, gpre_ref, wg_ref, wu_ref, wd_ref,
                    gpost_ref, p_ref, ggn_ref, wpg_ref, wpp_ref, gple_ref, out_ref, a_ref, acc_ref):
    j = pl.program_id(1)

    chunks = [slice(r, r + TAIL_ROWS) for r in range(0, out_ref.shape[0], TAIL_ROWS)]

    @pl.when(j == 0)
    def _():
        k = oa_ref.shape[1]
        for rows in chunks:
            y = _dot(oa_ref[rows, :], wo_ref[0:k, :]) + _dot(om_ref[rows, :], wo_ref[k:, :])
            h1 = h_ref[rows, :] + _rms(y, gmix_ref[...])
            out_ref[rows, :] = h1
            a_ref[rows, :] = _rms(h1, gpre_ref[...]).astype(BF16)
        acc_ref[...] = jnp.zeros_like(acc_ref)

    a = a_ref[...]
    mid = (_silu(_dot(a, wg_ref[...])) * _dot(a, wu_ref[...])).astype(BF16)
    acc_ref[...] += _dot(mid, wd_ref[...])

    @pl.when(j == pl.num_programs(1) - 1)
    def _():
        for rows in chunks:
            h2 = out_ref[rows, :] + _rms(acc_ref[rows, :], gpost_ref[...])
            out_ref[rows, :] = _ple_update(h2, p_ref[rows, :], ggn_ref[...], wpg_ref[...],
                                           wpp_ref[...], gple_ref[...])


def _l0_tail(oa, om, wo, gmix, h, gpre, wg, wu, wd, gpost, p, ggn, wpg, wpp, gple):
    m = h.shape[0]
    dff = wg.shape[1]
    row = lambda i, j: (i, 0)
    const = lambda i, j: (0, 0)
    vec = lambda g: pl.BlockSpec(g.shape, const)
    once = lambda w: pl.BlockSpec(w.shape, const, pipeline_mode=pl.Buffered(1))
    return pl.pallas_call(
        _l0_tail_kernel, grid=(m // TM_FFN, dff // TF),
        in_specs=[pl.BlockSpec((TM_FFN, oa.shape[1]), row), pl.BlockSpec((TM_FFN, om.shape[1]), row),
                  once(wo), vec(gmix), pl.BlockSpec((TM_FFN, D_MODEL), row), vec(gpre),
                  pl.BlockSpec((D_MODEL, TF), lambda i, j: (0, j)),
                  pl.BlockSpec((D_MODEL, TF), lambda i, j: (0, j)),
                  pl.BlockSpec((TF, D_MODEL), lambda i, j: (j, 0)),
                  vec(gpost), pl.BlockSpec((None, TM_FFN, PLE_DIM), lambda i, j: (0, i, 0)), vec(ggn), once(wpg), once(wpp),
                  vec(gple)],
        out_specs=pl.BlockSpec((TM_FFN, D_MODEL), row),
        out_shape=jax.ShapeDtypeStruct((m, D_MODEL), F32),
        scratch_shapes=[pltpu.VMEM((TM_FFN, D_MODEL), BF16), pltpu.VMEM((TM_FFN, D_MODEL), F32)],
        compiler_params=_params(("parallel", "arbitrary"), 56), name="l0_tail",
    )(oa, om, wo, gmix, h, gpre, wg, wu, wd, gpost, p, ggn, wpg, wpp, gple)


def _l1_qkv_kernel(h_ref, gpre_ref, w_ref, c_ref, s_ref, out0_ref, out1_ref, out2_ref, stage_ref):
    a = _rms(h_ref[...], gpre_ref[...]).astype(BF16)
    tm = a.shape[0]
    outs = (out0_ref, out1_ref, out2_ref)
    for blk in range(DIL_IN_WIDTH // 512):
        gi, kind = divmod(blk, 3)
        dil = DIL_CONFIGS[gi][1]
        z = _dot(a, w_ref[:, blk * 512:(blk + 1) * 512])
        if kind < 2:
            c, s = c_ref[kind], s_ref[kind]
            z = jnp.concatenate(
                [_rotate(z[:, sub * LANES:(sub + 1) * LANES], c, s, PARTIAL_ROPE_DIM // 2)
                 for sub in range(4)], axis=1)
        cols = slice(kind * 512, (kind + 1) * 512)
        if dil == 1:
            outs[gi][0, :, cols] = z.astype(BF16)
            continue
        for sub in range(4):
            stage = stage_ref.at[(blk % 2) * 4 + sub]
            stage[...] = z[:, sub * LANES:(sub + 1) * LANES]
            lanes = slice(kind * 512 + sub * LANES, kind * 512 + (sub + 1) * LANES)
            for r in range(dil):
                outs[gi][r, :, lanes] = stage[pl.ds(r, tm // dil, stride=dil), :].astype(BF16)


def _l1_qkv(h, gpre, w, ctab, stab, b, s):
    m = h.shape[0]
    nt = s // TM
    const = lambda i: (0, 0)
    out_specs, out_shape = [], []
    for _, dil in DIL_CONFIGS:
        out_specs.append(pl.BlockSpec((None, dil, TM // dil, 1536), lambda i: (i // nt, 0, i % nt, 0)))
        out_shape.append(jax.ShapeDtypeStruct((b, dil, s // dil, 1536), BF16))
    return pl.pallas_call(
        _l1_qkv_kernel, grid=(m // TM,),
        in_specs=[pl.BlockSpec((TM, D_MODEL), lambda i: (i, 0)),
                  pl.BlockSpec(gpre.shape, const),
                  pl.BlockSpec(w.shape, const, pipeline_mode=pl.Buffered(1)),
                  pl.BlockSpec((2, TM, LANES), lambda i: (0, i % nt, 0)),
                  pl.BlockSpec((2, TM, LANES), lambda i: (0, i % nt, 0))],
        out_specs=out_specs, out_shape=out_shape,
        scratch_shapes=[pltpu.VMEM((8, TM, LANES), F32)],
        compiler_params=_params(("parallel",), 48), name="l1_qkv",
    )(h, gpre, w, ctab, stab)


def _dil_attn_kernel(q0_ref, k0_ref, v0_ref, q1_ref, k1_ref, v1_ref, q2_ref, k2_ref, v2_ref, o_ref,
                     ra_ref, rl_ref, rm_ref, acc_ref, sum_ref, max_ref, bias_ref):
    lo = _lane_iota((DIL_QB, LANES)) < HEAD_DIM
    groups = ((q0_ref, k0_ref, v0_ref), (q1_ref, k1_ref, v1_ref), (q2_ref, k2_ref, v2_ref))
    for gi, (q_ref, k_ref, v_ref) in enumerate(groups):
        dil, length = q_ref.shape[0], q_ref.shape[1]
        width = min(DIL_QB + 2 * DIL_HALF, length)
        nblk = length // DIL_QB
        first = gi == 0
        ta, tl, tm_ = (acc_ref, sum_ref, max_ref) if first else (ra_ref, rl_ref, rm_ref)
        row = lax.broadcasted_iota(jnp.int32, (2 * DIL_QB, width), 0) % DIL_QB
        col = lax.broadcasted_iota(jnp.int32, (2 * DIL_QB, width), 1)
        rel = row - col
        ones = jnp.ones((width, LANES), BF16)
        for case in range(1 if nblk == 1 else 3):
            bias_ref[case, :, 0:width] = jnp.where(jnp.abs(rel + case * DIL_HALF) <= DIL_HALF,
                                                   0.0, NEG_INF)

        def body(it, carry, length=length, width=width, nblk=nblk, ta=ta, tl=tl, tm_=tm_,
                 ones=ones, q_ref=q_ref, k_ref=k_ref, v_ref=v_ref):
            r = it // nblk
            i = it % nblk
            q0 = pl.multiple_of(i * DIL_QB, DIL_QB)
            start = pl.multiple_of(jnp.clip(q0 - DIL_HALF, 0, length - width), DIL_HALF)
            qrow = pl.multiple_of(r * length + q0, DIL_HALF)
            q = q_ref[r, pl.ds(q0, DIL_QB), :]
            kw = k_ref[r, pl.ds(start, width), :]
            vw = jnp.concatenate([v_ref[r, pl.ds(start, width), :], ones], axis=1)
            zero = jnp.zeros_like(q)
            q2 = jnp.concatenate([jnp.where(lo, q, zero), jnp.where(lo, zero, q)], axis=0)
            sc = lax.dot_general(q2, kw, (((1,), (1,)), ((), ())), preferred_element_type=F32)
            sc = sc + bias_ref[lax.shift_right_logical(q0 - start, 6), :, 0:width]
            m = jnp.max(sc, axis=-1, keepdims=True)
            p = jnp.exp2(sc - m).astype(BF16)
            o2 = _dot(p, vw)
            ta[pl.ds(qrow, DIL_QB), :] = jnp.where(lo, o2[:DIL_QB, 0:LANES], o2[DIL_QB:, 0:LANES])
            tl[pl.ds(qrow, DIL_QB), :] = jnp.where(lo, o2[:DIL_QB, LANES:], o2[DIL_QB:, LANES:])
            tm_[pl.ds(qrow, DIL_QB), :] = jnp.where(lo, m[:DIL_QB], m[DIL_QB:])
            return carry

        lax.fori_loop(0, dil * nblk, body, 0, unroll=8)
        if first:
            continue
        for r in range(dil):
            rows = pl.ds(r, length, stride=dil)
            blk = slice(r * length, (r + 1) * length)
            m_old, m_blk = max_ref[rows, :], rm_ref[blk, :]
            m_new = jnp.maximum(m_old, m_blk)
            w_old, w_blk = jnp.exp2(m_old - m_new), jnp.exp2(m_blk - m_new)
            acc_ref[rows, :] = acc_ref[rows, :] * w_old + ra_ref[blk, :] * w_blk
            sum_ref[rows, :] = sum_ref[rows, :] * w_old + rl_ref[blk, :] * w_blk
            max_ref[rows, :] = m_new
    o_ref[...] = (acc_ref[...] / sum_ref[...]).astype(BF16)


def _dil_attn(qkvs, b, s):
    in_specs, args = [], []
    for x in qkvs:
        for j in range(3):
            in_specs.append(pl.BlockSpec((None,) + x.shape[1:3] + (LANES,),
                                         lambda bi, hp, j=j: (bi, 0, 0, 4 * j + hp)))
            args.append(x)
    stat = pltpu.VMEM((s, LANES), F32)
    return pl.pallas_call(
        _dil_attn_kernel, grid=(b, 4), in_specs=in_specs,
        out_specs=pl.BlockSpec((s, LANES), lambda bi, hp: (bi, hp)),
        out_shape=jax.ShapeDtypeStruct((b * s, 512), BF16),
        scratch_shapes=[stat] * 6 + [pltpu.VMEM((3, 2 * DIL_QB, DIL_QB + 2 * DIL_HALF), F32)],
        compiler_params=_params(("parallel", "parallel"), 40), name="dil_attn",
    )(*args)


def _l1_router_kernel(o_ref, wo_ref, gmix_ref, h_ref, gpre_ref, wr_ref,
                      h1_ref, a_ref, idx_ref, gate_ref, cnt_ref, carry_ref):
    @pl.when(pl.program_id(0) == 0)
    def _():
        carry_ref[...] = jnp.zeros_like(carry_ref)

    h1 = h_ref[...] + _rms(_dot(o_ref[...], wo_ref[...]), gmix_ref[...])
    h1_ref[...] = h1
    a = _rms(h1, gpre_ref[...]).astype(BF16)
    a_ref[...] = a
    logits = _dot(a, wr_ref[...])
    tm = logits.shape[0]
    lane = _lane_iota(logits.shape)
    lanef = lane.astype(F32)
    lg = jnp.where(lane < N_EXPERTS, logits, -jnp.inf)
    m1 = jnp.max(lg, axis=-1, keepdims=True)
    i1 = jnp.min(jnp.where(lg == m1, lanef, float(LANES)), axis=-1, keepdims=True)
    lg2 = jnp.where(lanef == i1, -jnp.inf, lg)
    m2 = jnp.max(lg2, axis=-1, keepdims=True)
    i2 = jnp.min(jnp.where(lg2 == m2, lanef, float(LANES)), axis=-1, keepdims=True)
    t = jnp.exp(m2 - m1)
    g1 = 1.0 / (1.0 + t)
    g2 = t / (1.0 + t)
    hit1 = lanef == i1
    hit2 = lanef == i2
    onehot = jnp.where(hit1 | hit2, 1.0, 0.0)
    r = lax.broadcasted_iota(jnp.int32, (tm, tm), 0)
    c = lax.broadcasted_iota(jnp.int32, (tm, tm), 1)
    tri = jnp.where(c < r, 1.0, 0.0).astype(BF16)
    before = _dot(tri, onehot.astype(BF16)) + carry_ref[...]
    rank1 = jnp.sum(jnp.where(hit1, before, 0.0), axis=-1, keepdims=True)
    rank2 = jnp.sum(jnp.where(hit2, before, 0.0), axis=-1, keepdims=True)
    info = jnp.where(lane == 0, i1, jnp.where(lane == 1, i2, jnp.where(lane == 2, rank1, rank2)))
    idx_ref[...] = info.astype(jnp.int32)
    gate_ref[...] = jnp.where(lane == 0, g1, g2)
    carry_ref[...] += jnp.sum(onehot, axis=0, keepdims=True)
    cnt_ref[...] = carry_ref[...]


def _l1_router(o, wo, gmix, h, gpre, wr):
    m = h.shape[0]
    row = lambda i: (i, 0)
    const = lambda i: (0, 0)
    full = lambda x: pl.BlockSpec(x.shape, const)
    return pl.pallas_call(
        _l1_router_kernel, grid=(m // TM,),
        in_specs=[pl.BlockSpec((TM, o.shape[1]), row), full(wo), full(gmix),
                  pl.BlockSpec((TM, D_MODEL), row), full(gpre), full(wr)],
        out_specs=[pl.BlockSpec((TM, D_MODEL), row), pl.BlockSpec((TM, D_MODEL), row),
                   pl.BlockSpec((TM, LANES), row), pl.BlockSpec((TM, LANES), row),
                   pl.BlockSpec((1, LANES), const)],
        out_shape=[jax.ShapeDtypeStruct((m, D_MODEL), F32),
                   jax.ShapeDtypeStruct((m, D_MODEL), BF16),
                   jax.ShapeDtypeStruct((m, LANES), jnp.int32),
                   jax.ShapeDtypeStruct((m, LANES), F32),
                   jax.ShapeDtypeStruct((1, LANES), F32)],
        scratch_shapes=[pltpu.VMEM((1, LANES), F32)],
        compiler_params=_params(("arbitrary",), 32), name="l1_router",
    )(o, wo, gmix, h, gpre, wr)


def _expert_kernel(te_ref, tv_ref, x_ref, wg_ref, wu_ref, wd_ref, *refs):
    y_ref, acc_ref = refs[-2:]
    j = pl.program_id(1)
    t = pl.program_id(2)
    last = pl.num_programs(1) - 1
    valid = tv_ref[pl.program_id(0) * MOE_GROUP + t] > 0

    @pl.when(valid & (j == 0))
    def _():
        acc_ref[t] = jnp.zeros(acc_ref.shape[1:], F32)

    @pl.when(valid)
    def _():
        x = x_ref[...]
        gate = _dot(x, wg_ref[...].astype(BF16))
        up = _dot(x, wu_ref[...].astype(BF16))
        acc_ref[t] += _dot((_silu(gate) * up).astype(BF16), wd_ref[...].astype(BF16))

    @pl.when(valid & (j == last))
    def _():
        y_ref[...] = acc_ref[t].astype(BF16)

    @pl.when(jnp.logical_not(valid) & (j == last))
    def _():
        y_ref[...] = jnp.zeros_like(y_ref)


def _experts(tile_e, tile_valid, xs, wg, wu, wd, ys, first_tile, n_slots):
    n_tiles = xs.shape[0] // TM_MOE
    n_j = wg.shape[2] // TF
    te = tile_e[first_tile:first_tile + n_tiles]
    tv = tile_valid[first_tile:first_tile + n_tiles]
    tile = lambda g, t: g * MOE_GROUP + t
    in_specs = [
        pl.BlockSpec((TM_MOE, D_MODEL), lambda g, j, t, te, tv: (tile(g, t), 0)),
        pl.BlockSpec((None, D_MODEL, TF),
                     lambda g, j, t, te, tv: (te[tile(g, t)], 0, j * tv[tile(g, t)])),
        pl.BlockSpec((None, D_MODEL, TF),
                     lambda g, j, t, te, tv: (te[tile(g, t)], 0, j * tv[tile(g, t)])),
        pl.BlockSpec((None, TF, D_MODEL),
                     lambda g, j, t, te, tv: (te[tile(g, t)], j * tv[tile(g, t)], 0))]
    args = [te, tv, xs, wg, wu, wd]
    aliases = {}
    if ys is not None:
        in_specs.append(pl.BlockSpec(memory_space=pl.ANY))
        aliases = {len(args): 0}
        args.append(ys)
    out_spec = pl.BlockSpec(
        (TM_MOE, D_MODEL),
        lambda g, j, t, te, tv: (first_tile + g * MOE_GROUP + jnp.where(j == n_j - 1, t, 0), 0))
    grid_spec = pltpu.PrefetchScalarGridSpec(
        num_scalar_prefetch=2, grid=(n_tiles // MOE_GROUP, n_j, MOE_GROUP), in_specs=in_specs,
        out_specs=out_spec,
        scratch_shapes=[pltpu.VMEM((MOE_GROUP, TM_MOE, D_MODEL), F32)])
    return pl.pallas_call(
        _expert_kernel, grid_spec=grid_spec,
        out_shape=jax.ShapeDtypeStruct((n_slots, D_MODEL), BF16),
        input_output_aliases=aliases,
        compiler_params=_params(("parallel", "arbitrary", "arbitrary"), 56), name="moe_experts",
    )(*args)


def _combine_ple_kernel(y0_ref, y1_ref, gate_ref, g_ref, h_ref, p_ref, ggn_ref, wpg_ref, wpp_ref,
                        gple_ref, out_ref):
    gate = gate_ref[...]
    f = y0_ref[...].astype(F32) * gate[:, 0:1] + y1_ref[...].astype(F32) * gate[:, 1:2]
    h2 = h_ref[...] + _rms(f, g_ref[...])
    out_ref[...] = _ple_update(h2, p_ref[...], ggn_ref[...], wpg_ref[...], wpp_ref[...], gple_ref[...])


def _combine_ple(y01, gate, g, h, p, ggn, wpg, wpp, gple):
    m = h.shape[0]
    nt = m // TM
    row = lambda i: (i, 0)
    const = lambda i: (0, 0)
    full = lambda x: pl.BlockSpec(x.shape, const)
    act = pl.BlockSpec((TM, D_MODEL), row)
    return pl.pallas_call(
        _combine_ple_kernel, grid=(nt,),
        in_specs=[act, pl.BlockSpec((TM, D_MODEL), lambda i: (nt + i, 0)),
                  pl.BlockSpec((TM, LANES), row), full(g), act,
                  pl.BlockSpec((None, TM, PLE_DIM), lambda i: (1, i, 0)), full(ggn), full(wpg),
                  full(wpp), full(gple)],
        out_specs=act,
        out_shape=jax.ShapeDtypeStruct((m, D_MODEL), F32),
        compiler_params=_params(("parallel",), 40), name="combine_ple",
    )(y01, y01, gate, g, h, p, ggn, wpg, wpp, gple)


def _rows(x, idx):
    return x.at[idx].get(mode='promise_in_bounds')


def _moe_plan(idx, cnt, n):
    counts = cnt[0, :N_EXPERTS].astype(jnp.int32)
    padded = (counts + TM_MOE - 1) // TM_MOE * TM_MOE
    end_pad = jnp.cumsum(padded)
    start_pad = end_pad - padded
    experts = jnp.arange(N_EXPERTS, dtype=jnp.int32)
    first = lambda e: jnp.sum(jnp.where(e[:, None] == experts, start_pad, 0), axis=-1)
    dest = (first(idx[:, 0]) + idx[:, 2], first(idx[:, 1]) + idx[:, 3])
    n_slots = n * TOP_K + N_EXPERTS * TM_MOE
    pair = 2 * jnp.arange(n, dtype=jnp.int32)
    keys = jnp.concatenate([idx[:, 0] * (2 * n) + pair, idx[:, 1] * (2 * n) + pair + 1])
    by_expert = jnp.concatenate([(jnp.sort(keys) % (2 * n)) // 2, jnp.zeros((n,), jnp.int32)])
    start_run = jnp.cumsum(counts) - counts
    slot_tok = jnp.zeros((n_slots + n,), jnp.int32)
    for e in range(N_EXPERTS):
        run = lax.dynamic_slice(by_expert, (start_run[e],), (n,))
        slot_tok = lax.dynamic_update_slice(slot_tok, run, (start_pad[e],))
    slot_tok = slot_tok[:n_slots]
    tile_start = jnp.arange(n_slots // TM_MOE, dtype=jnp.int32) * TM_MOE
    tile_e = jnp.minimum(jnp.sum(tile_start[:, None] >= end_pad[None, :], axis=1),
                         N_EXPERTS - 1).astype(jnp.int32)
    tile_valid = (tile_start < end_pad[-1]).astype(jnp.int32)
    return dest, slot_tok, tile_e, tile_valid


def _rotary_tables(pos, dim, theta):
    exponent = jnp.arange(0, dim, 2, dtype=F32) / dim
    inv_freq = jnp.power(jnp.float32(theta), -exponent)
    ang = pos.astype(F32)[:, None] * inv_freq[None, :]
    return jnp.cos(ang), jnp.sin(ang)


def _tables(s):
    t = jnp.arange(s, dtype=jnp.int32)
    cr, sr = _rotary_tables(t // GRID_W, HEAD_DIM // 2, AXIAL_THETA)
    cc, sc = _rotary_tables(t % GRID_W, HEAD_DIM // 2, AXIAL_THETA)
    ca = jnp.tile(jnp.concatenate([cr, cr, cc, cc], axis=1), (1, 2))
    sa = jnp.tile(jnp.concatenate([-sr, sr, -sc, sc], axis=1), (1, 2))
    cm_, sm_ = _rotary_tables(t, MLA_ROPE_DIM, MLA_ROPE_THETA)
    one = jnp.ones((s, 1), F32)
    zero = jnp.zeros((s, 1), F32)
    cm = jnp.concatenate([jnp.tile(one, (1, 64)), cm_, cm_, jnp.tile(one, (1, 32))], axis=1)
    sm = jnp.concatenate([jnp.tile(zero, (1, 64)), -sm_, sm_, jnp.tile(zero, (1, 32))], axis=1)
    ckr = jnp.concatenate([cm_, cm_, jnp.tile(zero, (1, 96))], axis=1)
    skr = jnp.concatenate([-sm_, sm_, jnp.tile(zero, (1, 96))], axis=1)
    cp_, sp_ = _rotary_tables(t, PARTIAL_ROPE_DIM, ROPE_THETA)
    cp = jnp.tile(jnp.concatenate([cp_, cp_, jnp.tile(one, (1, 48))], axis=1), (1, 2))
    sp = jnp.tile(jnp.concatenate([-sp_, sp_, jnp.tile(zero, (1, 48))], axis=1), (1, 2))
    scale = HEAD_DIM ** -0.5 * LOG2_E
    ctab = jnp.stack([cp * scale, cp])
    stab = jnp.stack([sp * scale, sp])
    return ca, sa, cm, sm, ckr, skr, ctab, stab


_GQA_PERM = (0, 4, 1, 5, 2, 6, 3, 7)


def _l0_weights(w_in, w_uq, w_ukv, w_out):
    d = w_in.shape[0]
    q = w_in[:, :512].reshape(d, 8, 64)[:, _GQA_PERM, :].reshape(d, 512)
    win = jnp.concatenate([q, w_in[:, 512:], jnp.zeros((d, AB_IN_PAD - w_in.shape[1]), w_in.dtype)],
                          axis=1).astype(BF16)
    uq = w_uq.reshape(MLA_Q_RANK, MLA_HEADS, MLA_NOPE_DIM + MLA_ROPE_DIM)
    wuq = jnp.pad(uq, ((0, 0), (0, 0), (0, LANES - uq.shape[2]))).reshape(MLA_Q_RANK, -1).astype(BF16)
    ukv = w_ukv.reshape(MLA_KV_RANK, MLA_HEADS, MLA_NOPE_DIM + MLA_V_DIM)
    wk = jnp.pad(ukv[:, :, :MLA_NOPE_DIM], ((0, 0), (0, 0), (0, LANES - MLA_NOPE_DIM)))
    wk = wk.reshape(MLA_KV_RANK, -1).astype(BF16)
    wv = ukv[:, :, MLA_NOPE_DIM:].reshape(MLA_KV_RANK, -1).astype(BF16)
    src = jnp.arange(LANES)[:, None]
    dst = jnp.arange(MLA_HEADS * LANES)[None, :]
    wp = ((src < MLA_ROPE_DIM) & (dst % LANES == src + MLA_NOPE_DIM)).astype(BF16)
    oa = w_out[:512].reshape(8, 64, -1)[_GQA_PERM, :, :].reshape(512, -1)
    wout = jnp.concatenate([oa, w_out[512:]], axis=0).astype(BF16)
    return win, wuq, wk, wp, wv, wout


def kernel(x, p, mix_pre_g, mix_post_g, ffn_pre_g, ffn_post_g, ple_w_proj, ple_gate_norm_g, ple_w_gate, ple_post_g, ab_w_in, gqa_q_norm_g, gqa_k_norm_g, mla_q_norm_g, mla_w_uq, mla_kv_norm_g, mla_w_ukv, ab_w_out, ffn_w_gate, ffn_w_up, ffn_w_down, dil_w_qkv, dil_w_out, moe_w_router, moe_w_gate, moe_w_up, moe_w_down):
    b, s, d = x.shape
    m = b * s
    row = lambda g: g.reshape(1, -1).astype(F32)
    ca, sa, cm, sm, ckr, skr, ctab, stab = _tables(s)
    h = x.reshape(m, d)
    pf = p.reshape(p.shape[0], m, PLE_DIM)

    win, wuq, wk, wp, wv, wout = _l0_weights(ab_w_in[0], mla_w_uq[0], mla_w_ukv[0], ab_w_out[0])
    gq = jnp.tile(row(gqa_q_norm_g[0]), (1, 2))
    gk = jnp.tile(row(gqa_k_norm_g[0]), (1, 2))
    qa, ka, vat, qm, km, vmt = _l0_in(h, row(mix_pre_g[0]), win, gq, gk, ca, sa,
                                      row(mla_q_norm_g[0]), wuq, cm, sm,
                                      row(mla_kv_norm_g[0]), wk, wp, wv, ckr, skr, b, s)
    oa = _gqa_attn(qa, ka, vat, b, s)
    om = _mla_attn(qm, km, vmt, b, s)
    h = _l0_tail(oa, om, wout, row(mix_post_g[0]), h, row(ffn_pre_g[0]),
                 ffn_w_gate[0].astype(BF16), ffn_w_up[0].astype(BF16), ffn_w_down[0].astype(BF16),
                 row(ffn_post_g[0]), pf, row(ple_gate_norm_g[0]), ple_w_gate[0].astype(BF16),
                 ple_w_proj[0].astype(BF16), row(ple_post_g[0]))

    qkvs = _l1_qkv(h, row(mix_pre_g[1]), dil_w_qkv[0].astype(BF16), ctab, stab, b, s)
    o = _dil_attn(qkvs, b, s)
    wr = jnp.pad(moe_w_router[0].astype(BF16), ((0, 0), (0, LANES - N_EXPERTS)))
    h, a, idx, gate, cnt = _l1_router(o, dil_w_out[0].astype(BF16), row(mix_post_g[1]), h,
                                      row(ffn_pre_g[1]), wr)
    dest, slot_tok, tile_e, tile_valid = _moe_plan(idx, cnt, m)
    n_slots = slot_tok.shape[0]
    ys, first_tile = None, 0
    n_groups = n_slots // (TM_MOE * MOE_GROUP)
    sizes = [n_groups * share // sum(MOE_CHUNKS) for share in MOE_CHUNKS]
    sizes[-1] += n_groups - sum(sizes)
    for n_tiles in [size * MOE_GROUP for size in sizes if size > 0]:
        rows = slice(first_tile * TM_MOE, (first_tile + n_tiles) * TM_MOE)
        ys = _experts(tile_e, tile_valid, _rows(a, slot_tok[rows]), moe_w_gate[0], moe_w_up[0],
                      moe_w_down[0], ys, first_tile, n_slots)
        first_tile += n_tiles
    h = _combine_ple(_rows(ys, jnp.concatenate(dest)), gate, row(ffn_post_g[1]), h, pf,
                     row(ple_gate_norm_g[1]), ple_w_gate[1].astype(BF16), ple_w_proj[1].astype(BF16),
                     row(ple_post_g[1]))
    return h.reshape(b, s, d)
```

```python
import functools

import jax
import jax.numpy as jnp
from jax import lax
from jax.experimental import pallas as pl
from jax.experimental.pallas import tpu as pltpu

F32 = jnp.float32
BF16 = jnp.bfloat16

D_MODEL = 1024
GRID_W = 64
HEAD_DIM = 64
NORM_EPS = 1e-6
NEG_INF = -1e30
ROPE_THETA = 500000.0
PARTIAL_ROPE_DIM = HEAD_DIM // 4
GQA_Q_HEADS = 8
GQA_KV_HEADS = 2
AXIAL_THETA = 10000.0
MLA_HEADS = 8
MLA_Q_RANK = 256
MLA_KV_RANK = 128
MLA_NOPE_DIM = 64
MLA_ROPE_DIM = 32
MLA_V_DIM = 64
MLA_ROPE_THETA = 10000.0
DIL_CONFIGS = ((128, 1), (512, 4), (2048, 16))
DIL_GROUPS = len(DIL_CONFIGS)
DIL_HEADS = 8
DIL_HALF = 64
DIL_QB = 128
LOG2_E = 1.4426950408889634
DIL_IN_WIDTH = DIL_GROUPS * 3 * DIL_HEADS * HEAD_DIM
D_FF = 3584
N_EXPERTS = 8
TOP_K = 2
PLE_DIM = 256

LANES = 128
VMEM_BYTES = 64 * 1024 * 1024
MIB = 1024 * 1024

TM = 512
TQ = 1024
KEY_CHUNK = 256
V_ONES = 16
V_ROWS = LANES + V_ONES
TF = 512
TM_FFN = 1024
TAIL_ROWS = 256
TM_MOE = 1024
AB_IN_PAD = 1280
MOE_GROUP = 4
COMBINE_PARTS = 2
MOE_CHUNKS = (1, 2, 3, 3)


def _params(semantics, vmem_mib):
    return pltpu.CompilerParams(dimension_semantics=semantics,
                                vmem_limit_bytes=vmem_mib * MIB)


def _dot(a, b):
    return jnp.dot(a, b, preferred_element_type=F32)


def _rms(x, g):
    return x * lax.rsqrt(jnp.mean(x * x, axis=-1, keepdims=True) + NORM_EPS) * g


def _lane_iota(shape):
    return lax.broadcasted_iota(jnp.int32, shape, len(shape) - 1)


def _swap_halves(x, k):
    w = x.shape[-1]
    fwd = pltpu.roll(x, w - k, 1)
    bwd = pltpu.roll(x, k, 1)
    return jnp.where((_lane_iota(x.shape) % (2 * k)) < k, fwd, bwd)


def _rotate(x, c, s, k):
    return x * c + _swap_halves(x, k) * s


def _head_rms(x, g):
    lo = _lane_iota(x.shape) < HEAD_DIM
    x2 = x * x
    s_all = jnp.sum(x2, axis=-1, keepdims=True)
    s_lo = jnp.sum(jnp.where(lo, x2, 0.0), axis=-1, keepdims=True)
    ms = jnp.where(lo, s_lo, s_all - s_lo) * (1.0 / HEAD_DIM)
    return x * lax.rsqrt(ms + NORM_EPS) * g


def _l0_in_kernel(h_ref, gpre_ref, win_ref, gq_ref, gk_ref, ca_ref, sa_ref,
                  gmq_ref, wuq_ref, cm_ref, sm_ref, gmkv_ref, wk_ref, wp_ref, wv_ref,
                  ckr_ref, skr_ref,
                  qa_ref, ka_ref, vat_ref, qm_ref, km_ref, vmt_ref):
    a = _rms(h_ref[...], gpre_ref[...]).astype(BF16)
    z = _dot(a, win_ref[...])
    ca, sa = ca_ref[...], sa_ref[...]
    for j in range(4):
        xq = _head_rms(z[:, j * LANES:(j + 1) * LANES], gq_ref[...])
        qa_ref[:, j * LANES:(j + 1) * LANES] = (
            _rotate(xq, ca, sa, 16) * (HEAD_DIM ** -0.5 * LOG2_E)).astype(BF16)
    ka_ref[...] = _rotate(_head_rms(z[:, 512:640], gk_ref[...]), ca, sa, 16).astype(BF16)
    ones = jnp.ones((V_ONES, z.shape[0]), BF16)
    vat_ref[0:LANES, :] = z[:, 640:768].T.astype(BF16)
    vat_ref[LANES:, :] = ones
    cq = _rms(z[:, 768:1024], gmq_ref[...]).astype(BF16)
    qm = _dot(cq, wuq_ref[...])
    cm, sm = cm_ref[...], sm_ref[...]
    scale_m = (MLA_NOPE_DIM + MLA_ROPE_DIM) ** -0.5 * LOG2_E
    for hb in range(MLA_HEADS):
        x = qm[:, hb * LANES:(hb + 1) * LANES]
        qm_ref[:, hb * LANES:(hb + 1) * LANES] = (_rotate(x, cm, sm, 16) * scale_m).astype(BF16)
    ckv = _rms(z[:, 1024:1152], gmkv_ref[...]).astype(BF16)
    kr = _rotate(z[:, 1152:1280], ckr_ref[...], skr_ref[...], 16).astype(BF16)
    km = _dot(ckv, wk_ref[...]) + _dot(kr, wp_ref[...])
    km_ref[...] = km.astype(BF16)
    vm = _dot(ckv, wv_ref[...])
    for j in range(4):
        vmt_ref[j * V_ROWS:j * V_ROWS + LANES, :] = vm[:, j * LANES:(j + 1) * LANES].T.astype(BF16)
        vmt_ref[j * V_ROWS + LANES:(j + 1) * V_ROWS, :] = ones


def _l0_in(h, gpre, win, gq, gk, ca, sa, gmq, wuq, cm, sm, gmkv, wk, wp, wv, ckr, skr, b, s):
    m = h.shape[0]
    nt = s // TM
    row = lambda i: (i, 0)
    const = lambda i: (0, 0)
    tab = lambda i: (i % nt, 0)
    full = lambda a: pl.BlockSpec(a.shape, const)
    in_specs = [pl.BlockSpec((TM, D_MODEL), row), full(gpre), full(win), full(gq), full(gk),
                pl.BlockSpec((TM, LANES), tab), pl.BlockSpec((TM, LANES), tab),
                full(gmq), full(wuq), pl.BlockSpec((TM, LANES), tab), pl.BlockSpec((TM, LANES), tab),
                full(gmkv), full(wk), full(wp), full(wv),
                pl.BlockSpec((TM, LANES), tab), pl.BlockSpec((TM, LANES), tab)]
    tr = lambda i: (i // nt, 0, i % nt)
    out_specs = [pl.BlockSpec((TM, 512), row),
                 pl.BlockSpec((TM, LANES), row),
                 pl.BlockSpec((None, V_ROWS, TM), tr),
                 pl.BlockSpec((TM, 1024), row),
                 pl.BlockSpec((TM, 1024), row),
                 pl.BlockSpec((None, 4 * V_ROWS, TM), tr)]
    out_shape = [jax.ShapeDtypeStruct((m, 512), BF16),
                 jax.ShapeDtypeStruct((m, LANES), BF16),
                 jax.ShapeDtypeStruct((b, V_ROWS, s), BF16),
                 jax.ShapeDtypeStruct((m, 1024), BF16),
                 jax.ShapeDtypeStruct((m, 1024), BF16),
                 jax.ShapeDtypeStruct((b, 4 * V_ROWS, s), BF16)]
    return pl.pallas_call(
        _l0_in_kernel, grid=(m // TM,), in_specs=in_specs, out_specs=out_specs,
        out_shape=out_shape, compiler_params=_params(("parallel",), 48), name="l0_in",
    )(h, gpre, win, gq, gk, ca, sa, gmq, wuq, cm, sm, gmkv, wk, wp, wv, ckr, skr)


def _attend_pair(score_t, vt_ref, o_ref):
    n_chunks = vt_ref.shape[1] // KEY_CHUNK
    tq = o_ref.shape[0]
    keys = [slice(c * KEY_CHUNK, (c + 1) * KEY_CHUNK) for c in range(n_chunks)]
    parts = []
    s_next = score_t(keys[0])
    for c in range(n_chunks):
        s_t = s_next
        if c + 1 < n_chunks:
            s_next = score_t(keys[c + 1])
        m = jnp.max(s_t, axis=0, keepdims=True)
        parts.append((m, _dot(vt_ref[:, keys[c]], jnp.exp2(s_t - m).astype(BF16))))
    m_all = functools.reduce(jnp.maximum, [m for m, _ in parts])
    o_t = sum(o_c * jnp.exp2(m_c - m_all) for m_c, o_c in parts)
    o_t = o_t[0:LANES, :] / o_t[LANES:LANES + 1, :]
    first = lax.broadcasted_iota(jnp.int32, (LANES, tq), 0) < HEAD_DIM
    o_ref[...] = jnp.where(first, o_t[:, 0:tq], o_t[:, tq:]).T.astype(BF16)


def _transposed(q):
    return q.astype(F32).T.astype(BF16)


def _gqa_attn_kernel(q_ref, k_ref, vt_ref, o_ref):
    qt = _transposed(q_ref[...])
    first = lax.broadcasted_iota(jnp.int32, qt.shape, 0) < HEAD_DIM
    zero = jnp.zeros_like(qt)
    w = jnp.concatenate([jnp.where(first, qt, zero), jnp.where(first, zero, qt)], axis=1)
    _attend_pair(lambda keys: _dot(k_ref[keys, :], w), vt_ref, o_ref)


def _gqa_attn(qa, ka, vat, b, s):
    m = qa.shape[0]
    nq = s // TQ
    return pl.pallas_call(
        _gqa_attn_kernel, grid=(b, nq, 4),
        in_specs=[pl.BlockSpec((TQ, LANES), lambda bi, i, j: (bi * nq + i, j)),
                  pl.BlockSpec((s, LANES), lambda bi, i, j: (bi, 0)),
                  pl.BlockSpec((None, V_ROWS, s), lambda bi, i, j: (bi, 0, 0))],
        out_specs=pl.BlockSpec((TQ, LANES), lambda bi, i, j: (bi * nq + i, j)),
        out_shape=jax.ShapeDtypeStruct((m, 512), BF16),
        compiler_params=_params(("parallel", "parallel", "parallel"), 32), name="gqa_attn",
    )(qa, ka, vat)


def _mla_attn_kernel(q_ref, k_ref, vt_ref, o_ref):
    qt0 = _transposed(q_ref[:, 0:LANES])
    qt1 = _transposed(q_ref[:, LANES:2 * LANES])

    def score_t(keys):
        return jnp.concatenate([_dot(k_ref[keys, 0:LANES], qt0),
                                _dot(k_ref[keys, LANES:2 * LANES], qt1)], axis=1)

    _attend_pair(score_t, vt_ref, o_ref)


def _mla_attn(qm, km, vmt, b, s):
    m = qm.shape[0]
    nq = s // TQ
    return pl.pallas_call(
        _mla_attn_kernel, grid=(b, nq, 4),
        in_specs=[pl.BlockSpec((TQ, 2 * LANES), lambda bi, i, j: (bi * nq + i, j)),
                  pl.BlockSpec((s, 2 * LANES), lambda bi, i, j: (bi, j)),
                  pl.BlockSpec((None, V_ROWS, s), lambda bi, i, j: (bi, j, 0))],
        out_specs=pl.BlockSpec((TQ, LANES), lambda bi, i, j: (bi * nq + i, j)),
        out_shape=jax.ShapeDtypeStruct((m, 512), BF16),
        compiler_params=_params(("parallel", "parallel", "parallel"), 32), name="mla_attn",
    )(qm, km, vmt)


def _silu(x):
    return x * (1.0 / (1.0 + jnp.exp(-x)))


def _ple_update(h, p, ggn, wg, wp, gpost):
    gate = _dot(_rms(h, ggn).astype(BF16), wg)
    gate = 1.0 / (1.0 + jnp.exp(-gate))
    e = _dot(p.astype(BF16), wp) * gate
    return h + _rms(e, gpost)


def _l0_tail_kernel(oa_ref, om_ref, wo_ref, gmix_ref, h_ref, gpre_ref, wg_ref, wu_ref, wd_ref,
                    gpost_ref, p_ref, ggn_ref, wpg_ref, wpp_ref, gple_ref, out_ref, a_ref, acc_ref):
    j = pl.program_id(1)

    chunks = [slice(r, r + TAIL_ROWS) for r in range(0, out_ref.shape[0], TAIL_ROWS)]

    @pl.when(j == 0)
    def _():
        k = oa_ref.shape[1]
        for rows in chunks:
            y = _dot(oa_ref[rows, :], wo_ref[0:k, :]) + _dot(om_ref[rows, :], wo_ref[k:, :])
            h1 = h_ref[rows, :] + _rms(y, gmix_ref[...])
            out_ref[rows, :] = h1
            a_ref[rows, :] = _rms(h1, gpre_ref[...]).astype(BF16)
        acc_ref[...] = jnp.zeros_like(acc_ref)

    a = a_ref[...]
    mid = (_silu(_dot(a, wg_ref[...])) * _dot(a, wu_ref[...])).astype(BF16)
    acc_ref[...] += _dot(mid, wd_ref[...])

    @pl.when(j == pl.num_programs(1) - 1)
    def _():
        for rows in chunks:
            h2 = out_ref[rows, :] + _rms(acc_ref[rows, :], gpost_ref[...])
            out_ref[rows, :] = _ple_update(h2, p_ref[rows, :], ggn_ref[...], wpg_ref[...],
                                           wpp_ref[...], gple_ref[...])


def _l0_tail(oa, om, wo, gmix, h, gpre, wg, wu, wd, gpost, p, ggn, wpg, wpp, gple):
    m = h.shape[0]
    dff = wg.shape[1]
    row = lambda i, j: (i, 0)
    const = lambda i, j: (0, 0)
    vec = lambda g: pl.BlockSpec(g.shape, const)
    once = lambda w: pl.BlockSpec(w.shape, const, pipeline_mode=pl.Buffered(1))
    return pl.pallas_call(
        _l0_tail_kernel, grid=(m // TM_FFN, dff // TF),
        in_specs=[pl.BlockSpec((TM_FFN, oa.shape[1]), row), pl.BlockSpec((TM_FFN, om.shape[1]), row),
                  once(wo), vec(gmix), pl.BlockSpec((TM_FFN, D_MODEL), row), vec(gpre),
                  pl.BlockSpec((D_MODEL, TF), lambda i, j: (0, j)),
                  pl.BlockSpec((D_MODEL, TF), lambda i, j: (0, j)),
                  pl.BlockSpec((TF, D_MODEL), lambda i, j: (j, 0)),
                  vec(gpost), pl.BlockSpec((None, TM_FFN, PLE_DIM), lambda i, j: (0, i, 0)), vec(ggn), once(wpg), once(wpp),
                  vec(gple)],
        out_specs=pl.BlockSpec((TM_FFN, D_MODEL), row),
        out_shape=jax.ShapeDtypeStruct((m, D_MODEL), F32),
        scratch_shapes=[pltpu.VMEM((TM_FFN, D_MODEL), BF16), pltpu.VMEM((TM_FFN, D_MODEL), F32)],
        compiler_params=_params(("parallel", "arbitrary"), 56), name="l0_tail",
    )(oa, om, wo, gmix, h, gpre, wg, wu, wd, gpost, p, ggn, wpg, wpp, gple)


def _l1_qkv_kernel(h_ref, gpre_ref, w_ref, c_ref, s_ref, out0_ref, out1_ref, out2_ref, stage_ref):
    a = _rms(h_ref[...], gpre_ref[...]).astype(BF16)
    tm = a.shape[0]
    outs = (out0_ref, out1_ref, out2_ref)
    for blk in range(DIL_IN_WIDTH // 512):
        gi, kind = divmod(blk, 3)
        dil = DIL_CONFIGS[gi][1]
        z = _dot(a, w_ref[:, blk * 512:(blk + 1) * 512])
        if kind < 2:
            c, s = c_ref[kind], s_ref[kind]
            z = jnp.concatenate(
                [_rotate(z[:, sub * LANES:(sub + 1) * LANES], c, s, PARTIAL_ROPE_DIM // 2)
                 for sub in range(4)], axis=1)
        cols = slice(kind * 512, (kind + 1) * 512)
        if dil == 1:
            outs[gi][0, :, cols] = z.astype(BF16)
            continue
        for sub in range(4):
            stage = stage_ref.at[(blk % 2) * 4 + sub]
            stage[...] = z[:, sub * LANES:(sub + 1) * LANES]
            lanes = slice(kind * 512 + sub * LANES, kind * 512 + (sub + 1) * LANES)
            for r in range(dil):
                outs[gi][r, :, lanes] = stage[pl.ds(r, tm // dil, stride=dil), :].astype(BF16)


def _l1_qkv(h, gpre, w, ctab, stab, b, s):
    m = h.shape[0]
    nt = s // TM
    const = lambda i: (0, 0)
    out_specs, out_shape = [], []
    for _, dil in DIL_CONFIGS:
        out_specs.append(pl.BlockSpec((None, dil, TM // dil, 1536), lambda i: (i // nt, 0, i % nt, 0)))
        out_shape.append(jax.ShapeDtypeStruct((b, dil, s // dil, 1536), BF16))
    return pl.pallas_call(
        _l1_qkv_kernel, grid=(m // TM,),
        in_specs=[pl.BlockSpec((TM, D_MODEL), lambda i: (i, 0)),
                  pl.BlockSpec(gpre.shape, const),
                  pl.BlockSpec(w.shape, const, pipeline_mode=pl.Buffered(1)),
                  pl.BlockSpec((2, TM, LANES), lambda i: (0, i % nt, 0)),
                  pl.BlockSpec((2, TM, LANES), lambda i: (0, i % nt, 0))],
        out_specs=out_specs, out_shape=out_shape,
        scratch_shapes=[pltpu.VMEM((8, TM, LANES), F32)],
        compiler_params=_params(("parallel",), 48), name="l1_qkv",
    )(h, gpre, w, ctab, stab)


def _dil_attn_kernel(q0_ref, k0_ref, v0_ref, q1_ref, k1_ref, v1_ref, q2_ref, k2_ref, v2_ref, o_ref,
                     ra_ref, rl_ref, rm_ref, acc_ref, sum_ref, max_ref, bias_ref):
    lo = _lane_iota((DIL_QB, LANES)) < HEAD_DIM
    groups = ((q0_ref, k0_ref, v0_ref), (q1_ref, k1_ref, v1_ref), (q2_ref, k2_ref, v2_ref))
    for gi, (q_ref, k_ref, v_ref) in enumerate(groups):
        dil, length = q_ref.shape[0], q_ref.shape[1]
        width = min(DIL_QB + 2 * DIL_HALF, length)
        nblk = length // DIL_QB
        first = gi == 0
        ta, tl, tm_ = (acc_ref, sum_ref, max_ref) if first else (ra_ref, rl_ref, rm_ref)
        row = lax.broadcasted_iota(jnp.int32, (2 * DIL_QB, width), 0) % DIL_QB
        col = lax.broadcasted_iota(jnp.int32, (2 * DIL_QB, width), 1)
        rel = row - col
        ones = jnp.ones((width, LANES), BF16)
        for case in range(1 if nblk == 1 else 3):
            bias_ref[case, :, 0:width] = jnp.where(jnp.abs(rel + case * DIL_HALF) <= DIL_HALF,
                                                   0.0, NEG_INF)

        def body(it, carry, length=length, width=width, nblk=nblk, ta=ta, tl=tl, tm_=tm_,
                 ones=ones, q_ref=q_ref, k_ref=k_ref, v_ref=v_ref):
            r = it // nblk
            i = it % nblk
            q0 = pl.multiple_of(i * DIL_QB, DIL_QB)
            start = pl.multiple_of(jnp.clip(q0 - DIL_HALF, 0, length - width), DIL_HALF)
            qrow = pl.multiple_of(r * length + q0, DIL_HALF)
            q = q_ref[r, pl.ds(q0, DIL_QB), :]
            kw = k_ref[r, pl.ds(start, width), :]
            vw = jnp.concatenate([v_ref[r, pl.ds(start, width), :], ones], axis=1)
            zero = jnp.zeros_like(q)
            q2 = jnp.concatenate([jnp.where(lo, q, zero), jnp.where(lo, zero, q)], axis=0)
            sc = lax.dot_general(q2, kw, (((1,), (1,)), ((), ())), preferred_element_type=F32)
            sc = sc + bias_ref[lax.shift_right_logical(q0 - start, 6), :, 0:width]
            m = jnp.max(sc, axis=-1, keepdims=True)
            p = jnp.exp2(sc - m).astype(BF16)
            o2 = _dot(p, vw)
            ta[pl.ds(qrow, DIL_QB), :] = jnp.where(lo, o2[:DIL_QB, 0:LANES], o2[DIL_QB:, 0:LANES])
            tl[pl.ds(qrow, DIL_QB), :] = jnp.where(lo, o2[:DIL_QB, LANES:], o2[DIL_QB:, LANES:])
            tm_[pl.ds(qrow, DIL_QB), :] = jnp.where(lo, m[:DIL_QB], m[DIL_QB:])
            return carry

        lax.fori_loop(0, dil * nblk, body, 0, unroll=8)
        if first:
            continue
        for r in range(dil):
            rows = pl.ds(r, length, stride=dil)
            blk = slice(r * length, (r + 1) * length)
            m_old, m_blk = max_ref[rows, :], rm_ref[blk, :]
            m_new = jnp.maximum(m_old, m_blk)
            w_old, w_blk = jnp.exp2(m_old - m_new), jnp.exp2(m_blk - m_new)
            acc_ref[rows, :] = acc_ref[rows, :] * w_old + ra_ref[blk, :] * w_blk
            sum_ref[rows, :] = sum_ref[rows, :] * w_old + rl_ref[blk, :] * w_blk
            max_ref[rows, :] = m_new
    o_ref[...] = (acc_ref[...] / sum_ref[...]).astype(BF16)


def _dil_attn(qkvs, b, s):
    in_specs, args = [], []
    for x in qkvs:
        for j in range(3):
            in_specs.append(pl.BlockSpec((None,) + x.shape[1:3] + (LANES,),
                                         lambda bi, hp, j=j: (bi, 0, 0, 4 * j + hp)))
            args.append(x)
    stat = pltpu.VMEM((s, LANES), F32)
    return pl.pallas_call(
        _dil_attn_kernel, grid=(b, 4), in_specs=in_specs,
        out_specs=pl.BlockSpec((s, LANES), lambda bi, hp: (bi, hp)),
        out_shape=jax.ShapeDtypeStruct((b * s, 512), BF16),
        scratch_shapes=[stat] * 6 + [pltpu.VMEM((3, 2 * DIL_QB, DIL_QB + 2 * DIL_HALF), F32)],
        compiler_params=_params(("parallel", "parallel"), 40), name="dil_attn",
    )(*args)


def _l1_router_kernel(o_ref, wo_ref, gmix_ref, h_ref, gpre_ref, wr_ref,
                      h1_ref, a_ref, idx_ref, gate_ref, cnt_ref, carry_ref):
    @pl.when(pl.program_id(0) == 0)
    def _():
        carry_ref[...] = jnp.zeros_like(carry_ref)

    h1 = h_ref[...] + _rms(_dot(o_ref[...], wo_ref[...]), gmix_ref[...])
    h1_ref[...] = h1
    a = _rms(h1, gpre_ref[...]).astype(BF16)
    a_ref[...] = a
    logits = _dot(a, wr_ref[...])
    tm = logits.shape[0]
    lane = _lane_iota(logits.shape)
    lanef = lane.astype(F32)
    lg = jnp.where(lane < N_EXPERTS, logits, -jnp.inf)
    m1 = jnp.max(lg, axis=-1, keepdims=True)
    i1 = jnp.min(jnp.where(lg == m1, lanef, float(LANES)), axis=-1, keepdims=True)
    lg2 = jnp.where(lanef == i1, -jnp.inf, lg)
    m2 = jnp.max(lg2, axis=-1, keepdims=True)
    i2 = jnp.min(jnp.where(lg2 == m2, lanef, float(LANES)), axis=-1, keepdims=True)
    t = jnp.exp(m2 - m1)
    g1 = 1.0 / (1.0 + t)
    g2 = t / (1.0 + t)
    hit1 = lanef == i1
    hit2 = lanef == i2
    onehot = jnp.where(hit1 | hit2, 1.0, 0.0)
    r = lax.broadcasted_iota(jnp.int32, (tm, tm), 0)
    c = lax.broadcasted_iota(jnp.int32, (tm, tm), 1)
    tri = jnp.where(c < r, 1.0, 0.0).astype(BF16)
    before = _dot(tri, onehot.astype(BF16)) + carry_ref[...]
    rank1 = jnp.sum(jnp.where(hit1, before, 0.0), axis=-1, keepdims=True)
    rank2 = jnp.sum(jnp.where(hit2, before, 0.0), axis=-1, keepdims=True)
    info = jnp.where(lane == 0, i1, jnp.where(lane == 1, i2, jnp.where(lane == 2, rank1, rank2)))
    idx_ref[...] = info.astype(jnp.int32)
    gate_ref[...] = jnp.where(lane == 0, g1, g2)
    carry_ref[...] += jnp.sum(onehot, axis=0, keepdims=True)
    cnt_ref[...] = carry_ref[...]


def _l1_router(o, wo, gmix, h, gpre, wr):
    m = h.shape[0]
    row = lambda i: (i, 0)
    const = lambda i: (0, 0)
    full = lambda x: pl.BlockSpec(x.shape, const)
    return pl.pallas_call(
        _l1_router_kernel, grid=(m // TM,),
        in_specs=[pl.BlockSpec((TM, o.shape[1]), row), full(wo), full(gmix),
                  pl.BlockSpec((TM, D_MODEL), row), full(gpre), full(wr)],
        out_specs=[pl.BlockSpec((TM, D_MODEL), row), pl.BlockSpec((TM, D_MODEL), row),
                   pl.BlockSpec((TM, LANES), row), pl.BlockSpec((TM, LANES), row),
                   pl.BlockSpec((1, LANES), const)],
        out_shape=[jax.ShapeDtypeStruct((m, D_MODEL), F32),
                   jax.ShapeDtypeStruct((m, D_MODEL), BF16),
                   jax.ShapeDtypeStruct((m, LANES), jnp.int32),
                   jax.ShapeDtypeStruct((m, LANES), F32),
                   jax.ShapeDtypeStruct((1, LANES), F32)],
        scratch_shapes=[pltpu.VMEM((1, LANES), F32)],
        compiler_params=_params(("arbitrary",), 32), name="l1_router",
    )(o, wo, gmix, h, gpre, wr)


def _expert_kernel(te_ref, tv_ref, x_ref, wg_ref, wu_ref, wd_ref, *refs):
    y_ref, acc_ref = refs[-2:]
    j = pl.program_id(1)
    t = pl.program_id(2)
    last = pl.num_programs(1) - 1
    valid = tv_ref[pl.program_id(0) * MOE_GROUP + t] > 0

    @pl.when(valid)
    def _():
        x = x_ref[...]
        gate = _dot(x, wg_ref[...].astype(BF16))
        up = _dot(x, wu_ref[...].astype(BF16))
        part = _dot((_silu(gate) * up).astype(BF16), wd_ref[...].astype(BF16))

        @pl.when(j == 0)
        def _():
            acc_ref[t] = part

        @pl.when(j > 0)
        def _():
            acc_ref[t] += part

    @pl.when(valid & (j == last))
    def _():
        y_ref[...] = acc_ref[t].astype(BF16)

    @pl.when(jnp.logical_not(valid) & (j == last))
    def _():
        y_ref[...] = jnp.zeros_like(y_ref)


def _experts(tile_e, tile_valid, xs, wg, wu, wd, ys, first_tile, n_slots):
    n_tiles = xs.shape[0] // TM_MOE
    n_j = wg.shape[2] // TF
    te = tile_e[first_tile:first_tile + n_tiles]
    tv = tile_valid[first_tile:first_tile + n_tiles]
    tile = lambda g, t: g * MOE_GROUP + t
    in_specs = [
        pl.BlockSpec((TM_MOE, D_MODEL), lambda g, j, t, te, tv: (tile(g, t), 0)),
        pl.BlockSpec((None, D_MODEL, TF),
                     lambda g, j, t, te, tv: (te[tile(g, t)], 0, j * tv[tile(g, t)])),
        pl.BlockSpec((None, D_MODEL, TF),
                     lambda g, j, t, te, tv: (te[tile(g, t)], 0, j * tv[tile(g, t)])),
        pl.BlockSpec((None, TF, D_MODEL),
                     lambda g, j, t, te, tv: (te[tile(g, t)], j * tv[tile(g, t)], 0))]
    args = [te, tv, xs, wg, wu, wd]
    aliases = {}
    if ys is not None:
        in_specs.append(pl.BlockSpec(memory_space=pl.ANY))
        aliases = {len(args): 0}
        args.append(ys)
    out_spec = pl.BlockSpec(
        (TM_MOE, D_MODEL),
        lambda g, j, t, te, tv: (first_tile + g * MOE_GROUP + jnp.where(j == n_j - 1, t, 0), 0))
    grid_spec = pltpu.PrefetchScalarGridSpec(
        num_scalar_prefetch=2, grid=(n_tiles // MOE_GROUP, n_j, MOE_GROUP), in_specs=in_specs,
        out_specs=out_spec,
        scratch_shapes=[pltpu.VMEM((MOE_GROUP, TM_MOE, D_MODEL), F32)])
    return pl.pallas_call(
        _expert_kernel, grid_spec=grid_spec,
        out_shape=jax.ShapeDtypeStruct((n_slots, D_MODEL), BF16),
        input_output_aliases=aliases,
        compiler_params=_params(("parallel", "arbitrary", "arbitrary"), 56), name="moe_experts",
    )(*args)


def _combine_ple_kernel(y0_ref, y1_ref, gate_ref, g_ref, h_ref, p_ref, ggn_ref, wpg_ref, wpp_ref,
                        gple_ref, *refs):
    out_ref = refs[-1]
    gate = gate_ref[...]
    f = y0_ref[...].astype(F32) * gate[:, 0:1] + y1_ref[...].astype(F32) * gate[:, 1:2]
    h2 = h_ref[...] + _rms(f, g_ref[...])
    out_ref[...] = _ple_update(h2, p_ref[...], ggn_ref[...], wpg_ref[...], wpp_ref[...], gple_ref[...])


def _combine_ple(y01, gate, g, h, p, ggn, wpg, wpp, gple, out, first_tile):
    m = h.shape[0]
    nt = y01.shape[0] // (2 * TM)
    tok = lambda i: (first_tile + i, 0)
    const = lambda i: (0, 0)
    full = lambda x: pl.BlockSpec(x.shape, const)
    in_specs = [pl.BlockSpec((TM, D_MODEL), lambda i: (i, 0)),
                pl.BlockSpec((TM, D_MODEL), lambda i: (nt + i, 0)),
                pl.BlockSpec((TM, LANES), tok), full(g), pl.BlockSpec((TM, D_MODEL), tok),
                pl.BlockSpec((None, TM, PLE_DIM), lambda i: (1, first_tile + i, 0)), full(ggn),
                full(wpg), full(wpp), full(gple)]
    args = [y01, y01, gate, g, h, p, ggn, wpg, wpp, gple]
    aliases = {}
    if out is not None:
        in_specs.append(pl.BlockSpec(memory_space=pl.ANY))
        aliases = {len(args): 0}
        args.append(out)
    return pl.pallas_call(
        _combine_ple_kernel, grid=(nt,), in_specs=in_specs,
        out_specs=pl.BlockSpec((TM, D_MODEL), tok),
        out_shape=jax.ShapeDtypeStruct((m, D_MODEL), F32),
        input_output_aliases=aliases,
        compiler_params=_params(("parallel",), 40), name="combine_ple",
    )(*args)


def _rows(x, idx):
    return x.at[idx].get(mode='promise_in_bounds')


def _moe_plan(idx, cnt, n):
    counts = cnt[0, :N_EXPERTS].astype(jnp.int32)
    padded = (counts + TM_MOE - 1) // TM_MOE * TM_MOE
    end_pad = jnp.cumsum(padded)
    start_pad = end_pad - padded
    experts = jnp.arange(N_EXPERTS, dtype=jnp.int32)
    first = lambda e: jnp.sum(jnp.where(e[:, None] == experts, start_pad, 0), axis=-1)
    dest = (first(idx[:, 0]) + idx[:, 2], first(idx[:, 1]) + idx[:, 3])
    n_slots = n * TOP_K + N_EXPERTS * TM_MOE
    pair = 2 * jnp.arange(n, dtype=jnp.int32)
    keys = jnp.concatenate([idx[:, 0] * (2 * n) + pair, idx[:, 1] * (2 * n) + pair + 1])
    by_expert = jnp.concatenate([(jnp.sort(keys) % (2 * n)) // 2, jnp.zeros((n,), jnp.int32)])
    start_run = jnp.cumsum(counts) - counts
    slot_tok = jnp.zeros((n_slots + n,), jnp.int32)
    for e in range(N_EXPERTS):
        run = lax.dynamic_slice(by_expert, (start_run[e],), (n,))
        slot_tok = lax.dynamic_update_slice(slot_tok, run, (start_pad[e],))
    slot_tok = slot_tok[:n_slots]
    tile_start = jnp.arange(n_slots // TM_MOE, dtype=jnp.int32) * TM_MOE
    tile_e = jnp.minimum(jnp.sum(tile_start[:, None] >= end_pad[None, :], axis=1),
                         N_EXPERTS - 1).astype(jnp.int32)
    tile_valid = (tile_start < end_pad[-1]).astype(jnp.int32)
    return dest, slot_tok, tile_e, tile_valid


def _rotary_tables(pos, dim, theta):
    exponent = jnp.arange(0, dim, 2, dtype=F32) / dim
    inv_freq = jnp.power(jnp.float32(theta), -exponent)
    ang = pos.astype(F32)[:, None] * inv_freq[None, :]
    return jnp.cos(ang), jnp.sin(ang)


def _tables(s):
    t = jnp.arange(s, dtype=jnp.int32)
    cr, sr = _rotary_tables(t // GRID_W, HEAD_DIM // 2, AXIAL_THETA)
    cc, sc = _rotary_tables(t % GRID_W, HEAD_DIM // 2, AXIAL_THETA)
    ca = jnp.tile(jnp.concatenate([cr, cr, cc, cc], axis=1), (1, 2))
    sa = jnp.tile(jnp.concatenate([-sr, sr, -sc, sc], axis=1), (1, 2))
    cm_, sm_ = _rotary_tables(t, MLA_ROPE_DIM, MLA_ROPE_THETA)
    one = jnp.ones((s, 1), F32)
    zero = jnp.zeros((s, 1), F32)
    cm = jnp.concatenate([jnp.tile(one, (1, 64)), cm_, cm_, jnp.tile(one, (1, 32))], axis=1)
    sm = jnp.concatenate([jnp.tile(zero, (1, 64)), -sm_, sm_, jnp.tile(zero, (1, 32))], axis=1)
    ckr = jnp.concatenate([cm_, cm_, jnp.tile(zero, (1, 96))], axis=1)
    skr = jnp.concatenate([-sm_, sm_, jnp.tile(zero, (1, 96))], axis=1)
    cp_, sp_ = _rotary_tables(t, PARTIAL_ROPE_DIM, ROPE_THETA)
    cp = jnp.tile(jnp.concatenate([cp_, cp_, jnp.tile(one, (1, 48))], axis=1), (1, 2))
    sp = jnp.tile(jnp.concatenate([-sp_, sp_, jnp.tile(zero, (1, 48))], axis=1), (1, 2))
    scale = HEAD_DIM ** -0.5 * LOG2_E
    ctab = jnp.stack([cp * scale, cp])
    stab = jnp.stack([sp * scale, sp])
    return ca, sa, cm, sm, ckr, skr, ctab, stab


_GQA_PERM = (0, 4, 1, 5, 2, 6, 3, 7)


def _l0_weights(w_in, w_uq, w_ukv, w_out):
    d = w_in.shape[0]
    q = w_in[:, :512].reshape(d, 8, 64)[:, _GQA_PERM, :].reshape(d, 512)
    win = jnp.concatenate([q, w_in[:, 512:], jnp.zeros((d, AB_IN_PAD - w_in.shape[1]), w_in.dtype)],
                          axis=1).astype(BF16)
    uq = w_uq.reshape(MLA_Q_RANK, MLA_HEADS, MLA_NOPE_DIM + MLA_ROPE_DIM)
    wuq = jnp.pad(uq, ((0, 0), (0, 0), (0, LANES - uq.shape[2]))).reshape(MLA_Q_RANK, -1).astype(BF16)
    ukv = w_ukv.reshape(MLA_KV_RANK, MLA_HEADS, MLA_NOPE_DIM + MLA_V_DIM)
    wk = jnp.pad(ukv[:, :, :MLA_NOPE_DIM], ((0, 0), (0, 0), (0, LANES - MLA_NOPE_DIM)))
    wk = wk.reshape(MLA_KV_RANK, -1).astype(BF16)
    wv = ukv[:, :, MLA_NOPE_DIM:].reshape(MLA_KV_RANK, -1).astype(BF16)
    src = jnp.arange(LANES)[:, None]
    dst = jnp.arange(MLA_HEADS * LANES)[None, :]
    wp = ((src < MLA_ROPE_DIM) & (dst % LANES == src + MLA_NOPE_DIM)).astype(BF16)
    oa = w_out[:512].reshape(8, 64, -1)[_GQA_PERM, :, :].reshape(512, -1)
    wout = jnp.concatenate([oa, w_out[512:]], axis=0).astype(BF16)
    return win, wuq, wk, wp, wv, wout


def kernel(x, p, mix_pre_g, mix_post_g, ffn_pre_g, ffn_post_g, ple_w_proj, ple_gate_norm_g, ple_w_gate, ple_post_g, ab_w_in, gqa_q_norm_g, gqa_k_norm_g, mla_q_norm_g, mla_w_uq, mla_kv_norm_g, mla_w_ukv, ab_w_out, ffn_w_gate, ffn_w_up, ffn_w_down, dil_w_qkv, dil_w_out, moe_w_router, moe_w_gate, moe_w_up, moe_w_down):
    b, s, d = x.shape
    m = b * s
    row = lambda g: g.reshape(1, -1).astype(F32)
    ca, sa, cm, sm, ckr, skr, ctab, stab = _tables(s)
    h = x.reshape(m, d)
    pf = p.reshape(p.shape[0], m, PLE_DIM)

    win, wuq, wk, wp, wv, wout = _l0_weights(ab_w_in[0], mla_w_uq[0], mla_w_ukv[0], ab_w_out[0])
    gq = jnp.tile(row(gqa_q_norm_g[0]), (1, 2))
    gk = jnp.tile(row(gqa_k_norm_g[0]), (1, 2))
    qa, ka, vat, qm, km, vmt = _l0_in(h, row(mix_pre_g[0]), win, gq, gk, ca, sa,
                                      row(mla_q_norm_g[0]), wuq, cm, sm,
                                      row(mla_kv_norm_g[0]), wk, wp, wv, ckr, skr, b, s)
    oa = _gqa_attn(qa, ka, vat, b, s)
    om = _mla_attn(qm, km, vmt, b, s)
    h = _l0_tail(oa, om, wout, row(mix_post_g[0]), h, row(ffn_pre_g[0]),
                 ffn_w_gate[0].astype(BF16), ffn_w_up[0].astype(BF16), ffn_w_down[0].astype(BF16),
                 row(ffn_post_g[0]), pf, row(ple_gate_norm_g[0]), ple_w_gate[0].astype(BF16),
                 ple_w_proj[0].astype(BF16), row(ple_post_g[0]))

    qkvs = _l1_qkv(h, row(mix_pre_g[1]), dil_w_qkv[0].astype(BF16), ctab, stab, b, s)
    o = _dil_attn(qkvs, b, s)
    wr = jnp.pad(moe_w_router[0].astype(BF16), ((0, 0), (0, LANES - N_EXPERTS)))
    h, a, idx, gate, cnt = _l1_router(o, dil_w_out[0].astype(BF16), row(mix_post_g[1]), h,
                                      row(ffn_pre_g[1]), wr)
    dest, slot_tok, tile_e, tile_valid = _moe_plan(idx, cnt, m)
    n_slots = slot_tok.shape[0]
    ys, first_tile = None, 0
    n_groups = n_slots // (TM_MOE * MOE_GROUP)
    sizes = [n_groups * share // sum(MOE_CHUNKS) for share in MOE_CHUNKS]
    sizes[-1] += n_groups - sum(sizes)
    for n_tiles in [size * MOE_GROUP for size in sizes if size > 0]:
        rows = slice(first_tile * TM_MOE, (first_tile + n_tiles) * TM_MOE)
        ys = _experts(tile_e, tile_valid, _rows(a, slot_tok[rows]), moe_w_gate[0], moe_w_up[0],
                      moe_w_down[0], ys, first_tile, n_slots)
        first_tile += n_tiles
    out, per = None, m // COMBINE_PARTS
    wpg, wpp = ple_w_gate[1].astype(BF16), ple_w_proj[1].astype(BF16)
    for c in range(COMBINE_PARTS):
        part = slice(c * per, (c + 1) * per)
        y01 = _rows(ys, jnp.concatenate([dest[0][part], dest[1][part]]))
        out = _combine_ple(y01, gate, row(ffn_post_g[1]), h, pf, row(ple_gate_norm_g[1]), wpg, wpp,
                           row(ple_post_g[1]), out, c * per // TM)
    return out.reshape(b, s, d)
```

```python
import functools

import jax
import jax.numpy as jnp
from jax import lax
from jax.experimental import pallas as pl
from jax.experimental.pallas import tpu as pltpu

F32 = jnp.float32
BF16 = jnp.bfloat16

D_MODEL = 1024
GRID_W = 64
HEAD_DIM = 64
NORM_EPS = 1e-6
NEG_INF = -1e30
ROPE_THETA = 500000.0
PARTIAL_ROPE_DIM = HEAD_DIM // 4
GQA_Q_HEADS = 8
GQA_KV_HEADS = 2
AXIAL_THETA = 10000.0
MLA_HEADS = 8
MLA_Q_RANK = 256
MLA_KV_RANK = 128
MLA_NOPE_DIM = 64
MLA_ROPE_DIM = 32
MLA_V_DIM = 64
MLA_ROPE_THETA = 10000.0
DIL_CONFIGS = ((128, 1), (512, 4), (2048, 16))
DIL_GROUPS = len(DIL_CONFIGS)
DIL_HEADS = 8
DIL_HALF = 64
DIL_QB = 128
LOG2_E = 1.4426950408889634
DIL_IN_WIDTH = DIL_GROUPS * 3 * DIL_HEADS * HEAD_DIM
D_FF = 3584
N_EXPERTS = 8
TOP_K = 2
PLE_DIM = 256

LANES = 128
VMEM_BYTES = 64 * 1024 * 1024
MIB = 1024 * 1024

TM = 512
TQ = 1024
KEY_CHUNK = 256
V_ONES = 16
V_ROWS = LANES + V_ONES
TF = 512
TM_FFN = 1024
TAIL_ROWS = 256
TM_MOE = 1024
AB_IN_PAD = 1280
MOE_GROUP = 4
MOE_CHUNKS = (1, 2, 3, 3)


def _params(semantics, vmem_mib):
    return pltpu.CompilerParams(dimension_semantics=semantics,
                                vmem_limit_bytes=vmem_mib * MIB)


def _dot(a, b):
    return jnp.dot(a, b, preferred_element_type=F32)


def _rms(x, g):
    return x * lax.rsqrt(jnp.mean(x * x, axis=-1, keepdims=True) + NORM_EPS) * g


def _lane_iota(shape):
    return lax.broadcasted_iota(jnp.int32, shape, len(shape) - 1)


def _swap_halves(x, k):
    w = x.shape[-1]
    fwd = pltpu.roll(x, w - k, 1)
    bwd = pltpu.roll(x, k, 1)
    return jnp.where((_lane_iota(x.shape) % (2 * k)) < k, fwd, bwd)


def _rotate(x, c, s, k):
    return x * c + _swap_halves(x, k) * s


def _head_rms(x, g):
    lo = _lane_iota(x.shape) < HEAD_DIM
    x2 = x * x
    s_all = jnp.sum(x2, axis=-1, keepdims=True)
    s_lo = jnp.sum(jnp.where(lo, x2, 0.0), axis=-1, keepdims=True)
    ms = jnp.where(lo, s_lo, s_all - s_lo) * (1.0 / HEAD_DIM)
    return x * lax.rsqrt(ms + NORM_EPS) * g


def _l0_in_kernel(h_ref, gpre_ref, win_ref, gq_ref, gk_ref, ca_ref, sa_ref,
                  gmq_ref, wuq_ref, cm_ref, sm_ref, gmkv_ref, wk_ref, wp_ref, wv_ref,
                  ckr_ref, skr_ref,
                  qa_ref, ka_ref, vat_ref, qm_ref, km_ref, vmt_ref):
    a = _rms(h_ref[...], gpre_ref[...]).astype(BF16)
    z = _dot(a, win_ref[...])
    ca, sa = ca_ref[...], sa_ref[...]
    for j in range(4):
        xq = _head_rms(z[:, j * LANES:(j + 1) * LANES], gq_ref[...])
        qa_ref[:, j * LANES:(j + 1) * LANES] = (
            _rotate(xq, ca, sa, 16) * (HEAD_DIM ** -0.5 * LOG2_E)).astype(BF16)
    ka_ref[...] = _rotate(_head_rms(z[:, 512:640], gk_ref[...]), ca, sa, 16).astype(BF16)
    ones = jnp.ones((V_ONES, z.shape[0]), BF16)
    vat_ref[0:LANES, :] = z[:, 640:768].T.astype(BF16)
    vat_ref[LANES:, :] = ones
    cq = _rms(z[:, 768:1024], gmq_ref[...]).astype(BF16)
    qm = _dot(cq, wuq_ref[...])
    cm, sm = cm_ref[...], sm_ref[...]
    scale_m = (MLA_NOPE_DIM + MLA_ROPE_DIM) ** -0.5 * LOG2_E
    for hb in range(MLA_HEADS):
        x = qm[:, hb * LANES:(hb + 1) * LANES]
        qm_ref[:, hb * LANES:(hb + 1) * LANES] = (_rotate(x, cm, sm, 16) * scale_m).astype(BF16)
    ckv = _rms(z[:, 1024:1152], gmkv_ref[...]).astype(BF16)
    kr = _rotate(z[:, 1152:1280], ckr_ref[...], skr_ref[...], 16).astype(BF16)
    km = _dot(ckv, wk_ref[...]) + _dot(kr, wp_ref[...])
    km_ref[...] = km.astype(BF16)
    vm = _dot(ckv, wv_ref[...])
    for j in range(4):
        vmt_ref[j * V_ROWS:j * V_ROWS + LANES, :] = vm[:, j * LANES:(j + 1) * LANES].T.astype(BF16)
        vmt_ref[j * V_ROWS + LANES:(j + 1) * V_ROWS, :] = ones


def _l0_in(h, gpre, win, gq, gk, ca, sa, gmq, wuq, cm, sm, gmkv, wk, wp, wv, ckr, skr, b, s):
    m = h.shape[0]
    nt = s // TM
    row = lambda i: (i, 0)
    const = lambda i: (0, 0)
    tab = lambda i: (i % nt, 0)
    full = lambda a: pl.BlockSpec(a.shape, const)
    in_specs = [pl.BlockSpec((TM, D_MODEL), row), full(gpre), full(win), full(gq), full(gk),
                pl.BlockSpec((TM, LANES), tab), pl.BlockSpec((TM, LANES), tab),
                full(gmq), full(wuq), pl.BlockSpec((TM, LANES), tab), pl.BlockSpec((TM, LANES), tab),
                full(gmkv), full(wk), full(wp), full(wv),
                pl.BlockSpec((TM, LANES), tab), pl.BlockSpec((TM, LANES), tab)]
    tr = lambda i: (i // nt, 0, i % nt)
    out_specs = [pl.BlockSpec((TM, 512), row),
                 pl.BlockSpec((TM, LANES), row),
                 pl.BlockSpec((None, V_ROWS, TM), tr),
                 pl.BlockSpec((TM, 1024), row),
                 pl.BlockSpec((TM, 1024), row),
                 pl.BlockSpec((None, 4 * V_ROWS, TM), tr)]
    out_shape = [jax.ShapeDtypeStruct((m, 512), BF16),
                 jax.ShapeDtypeStruct((m, LANES), BF16),
                 jax.ShapeDtypeStruct((b, V_ROWS, s), BF16),
                 jax.ShapeDtypeStruct((m, 1024), BF16),
                 jax.ShapeDtypeStruct((m, 1024), BF16),
                 jax.ShapeDtypeStruct((b, 4 * V_ROWS, s), BF16)]
    return pl.pallas_call(
        _l0_in_kernel, grid=(m // TM,), in_specs=in_specs, out_specs=out_specs,
        out_shape=out_shape, compiler_params=_params(("parallel",), 48), name="l0_in",
    )(h, gpre, win, gq, gk, ca, sa, gmq, wuq, cm, sm, gmkv, wk, wp, wv, ckr, skr)


def _attend_pair(score_t, vt_ref, o_ref):
    n_chunks = vt_ref.shape[1] // KEY_CHUNK
    tq = o_ref.shape[0]
    keys = [slice(c * KEY_CHUNK, (c + 1) * KEY_CHUNK) for c in range(n_chunks)]
    parts = []
    s_next = score_t(keys[0])
    pending = None
    for c in range(n_chunks):
        s_t = s_next
        if c + 1 < n_chunks:
            s_next = score_t(keys[c + 1])
        m = jnp.max(s_t, axis=0, keepdims=True)
        p_t = jnp.exp2(s_t - m).astype(BF16)
        if pending is not None:
            parts.append((pending[0], _dot(vt_ref[:, keys[c - 1]], pending[1])))
        pending = (m, p_t)
    parts.append((pending[0], _dot(vt_ref[:, keys[n_chunks - 1]], pending[1])))
    m_all = functools.reduce(jnp.maximum, [m for m, _ in parts])
    o_t = sum(o_c * jnp.exp2(m_c - m_all) for m_c, o_c in parts)
    o_t = o_t[0:LANES, :] / o_t[LANES:LANES + 1, :]
    first = lax.broadcasted_iota(jnp.int32, (LANES, tq), 0) < HEAD_DIM
    o_ref[...] = jnp.where(first, o_t[:, 0:tq], o_t[:, tq:]).T.astype(BF16)


def _transposed(q):
    return q.astype(F32).T.astype(BF16)


def _gqa_attn_kernel(q_ref, k_ref, vt_ref, o_ref):
    qt = _transposed(q_ref[...])
    first = lax.broadcasted_iota(jnp.int32, qt.shape, 0) < HEAD_DIM
    zero = jnp.zeros_like(qt)
    w = jnp.concatenate([jnp.where(first, qt, zero), jnp.where(first, zero, qt)], axis=1)
    _attend_pair(lambda keys: _dot(k_ref[keys, :], w), vt_ref, o_ref)


def _gqa_attn(qa, ka, vat, b, s):
    m = qa.shape[0]
    nq = s // TQ
    return pl.pallas_call(
        _gqa_attn_kernel, grid=(b, nq, 4),
        in_specs=[pl.BlockSpec((TQ, LANES), lambda bi, i, j: (bi * nq + i, j)),
                  pl.BlockSpec((s, LANES), lambda bi, i, j: (bi, 0)),
                  pl.BlockSpec((None, V_ROWS, s), lambda bi, i, j: (bi, 0, 0))],
        out_specs=pl.BlockSpec((TQ, LANES), lambda bi, i, j: (bi * nq + i, j)),
        out_shape=jax.ShapeDtypeStruct((m, 512), BF16),
        compiler_params=_params(("parallel", "parallel", "parallel"), 32), name="gqa_attn",
    )(qa, ka, vat)


def _mla_attn_kernel(q_ref, k_ref, vt_ref, o_ref):
    qt0 = _transposed(q_ref[:, 0:LANES])
    qt1 = _transposed(q_ref[:, LANES:2 * LANES])

    def score_t(keys):
        return jnp.concatenate([_dot(k_ref[keys, 0:LANES], qt0),
                                _dot(k_ref[keys, LANES:2 * LANES], qt1)], axis=1)

    _attend_pair(score_t, vt_ref, o_ref)


def _mla_attn(qm, km, vmt, b, s):
    m = qm.shape[0]
    nq = s // TQ
    return pl.pallas_call(
        _mla_attn_kernel, grid=(b, nq, 4),
        in_specs=[pl.BlockSpec((TQ, 2 * LANES), lambda bi, i, j: (bi * nq + i, j)),
                  pl.BlockSpec((s, 2 * LANES), lambda bi, i, j: (bi, j)),
                  pl.BlockSpec((None, V_ROWS, s), lambda bi, i, j: (bi, j, 0))],
        out_specs=pl.BlockSpec((TQ, LANES), lambda bi, i, j: (bi * nq + i, j)),
        out_shape=jax.ShapeDtypeStruct((m, 512), BF16),
        compiler_params=_params(("parallel", "parallel", "parallel"), 32), name="mla_attn",
    )(qm, km, vmt)


def _silu(x):
    return x * (1.0 / (1.0 + jnp.exp(-x)))


def _ple_update(h, p, ggn, wg, wp, gpost):
    gate = _dot(_rms(h, ggn).astype(BF16), wg)
    gate = 1.0 / (1.0 + jnp.exp(-gate))
    e = _dot(p.astype(BF16), wp) * gate
    return h + _rms(e, gpost)


def _l0_tail_kernel(oa_ref, om_ref, wo_ref, gmix_ref, h_ref, gpre_ref, wg_ref, wu_ref, wd_ref,
                    gpost_ref, p_ref, ggn_ref, wpg_ref, wpp_ref, gple_ref, out_ref, a_ref, acc_ref):
    j = pl.program_id(1)

    chunks = [slice(r, r + TAIL_ROWS) for r in range(0, out_ref.shape[0], TAIL_ROWS)]

    @pl.when(j == 0)
    def _():
        k = oa_ref.shape[1]
        for rows in chunks:
            y = _dot(oa_ref[rows, :], wo_ref[0:k, :]) + _dot(om_ref[rows, :], wo_ref[k:, :])
            h1 = h_ref[rows, :] + _rms(y, gmix_ref[...])
            out_ref[rows, :] = h1
            a_ref[rows, :] = _rms(h1, gpre_ref[...]).astype(BF16)
        acc_ref[...] = jnp.zeros_like(acc_ref)

    a = a_ref[...]
    mid = (_silu(_dot(a, wg_ref[...])) * _dot(a, wu_ref[...])).astype(BF16)
    acc_ref[...] += _dot(mid, wd_ref[...])

    @pl.when(j == pl.num_programs(1) - 1)
    def _():
        for rows in chunks:
            h2 = out_ref[rows, :] + _rms(acc_ref[rows, :], gpost_ref[...])
            out_ref[rows, :] = _ple_update(h2, p_ref[rows, :], ggn_ref[...], wpg_ref[...],
                                           wpp_ref[...], gple_ref[...])


def _l0_tail(oa, om, wo, gmix, h, gpre, wg, wu, wd, gpost, p, ggn, wpg, wpp, gple):
    m = h.shape[0]
    dff = wg.shape[1]
    row = lambda i, j: (i, 0)
    const = lambda i, j: (0, 0)
    vec = lambda g: pl.BlockSpec(g.shape, const)
    once = lambda w: pl.BlockSpec(w.shape, const, pipeline_mode=pl.Buffered(1))
    return pl.pallas_call(
        _l0_tail_kernel, grid=(m // TM_FFN, dff // TF),
        in_specs=[pl.BlockSpec((TM_FFN, oa.shape[1]), row), pl.BlockSpec((TM_FFN, om.shape[1]), row),
                  once(wo), vec(gmix), pl.BlockSpec((TM_FFN, D_MODEL), row), vec(gpre),
                  pl.BlockSpec((D_MODEL, TF), lambda i, j: (0, j)),
                  pl.BlockSpec((D_MODEL, TF), lambda i, j: (0, j)),
                  pl.BlockSpec((TF, D_MODEL), lambda i, j: (j, 0)),
                  vec(gpost), pl.BlockSpec((None, TM_FFN, PLE_DIM), lambda i, j: (0, i, 0)), vec(ggn), once(wpg), once(wpp),
                  vec(gple)],
        out_specs=pl.BlockSpec((TM_FFN, D_MODEL), row),
        out_shape=jax.ShapeDtypeStruct((m, D_MODEL), F32),
        scratch_shapes=[pltpu.VMEM((TM_FFN, D_MODEL), BF16), pltpu.VMEM((TM_FFN, D_MODEL), F32)],
        compiler_params=_params(("parallel", "arbitrary"), 56), name="l0_tail",
    )(oa, om, wo, gmix, h, gpre, wg, wu, wd, gpost, p, ggn, wpg, wpp, gple)


def _l1_qkv_kernel(h_ref, gpre_ref, w_ref, c_ref, s_ref, out0_ref, out1_ref, out2_ref, stage_ref):
    a = _rms(h_ref[...], gpre_ref[...]).astype(BF16)
    tm = a.shape[0]
    outs = (out0_ref, out1_ref, out2_ref)
    for blk in range(DIL_IN_WIDTH // 512):
        gi, kind = divmod(blk, 3)
        dil = DIL_CONFIGS[gi][1]
        z = _dot(a, w_ref[:, blk * 512:(blk + 1) * 512])
        if kind < 2:
            c, s = c_ref[kind], s_ref[kind]
            z = jnp.concatenate(
                [_rotate(z[:, sub * LANES:(sub + 1) * LANES], c, s, PARTIAL_ROPE_DIM // 2)
                 for sub in range(4)], axis=1)
        cols = slice(kind * 512, (kind + 1) * 512)
        if dil == 1:
            outs[gi][0, :, cols] = z.astype(BF16)
            continue
        for sub in range(4):
            stage = stage_ref.at[(blk % 2) * 4 + sub]
            stage[...] = z[:, sub * LANES:(sub + 1) * LANES]
            lanes = slice(kind * 512 + sub * LANES, kind * 512 + (sub + 1) * LANES)
            for r in range(dil):
                outs[gi][r, :, lanes] = stage[pl.ds(r, tm // dil, stride=dil), :].astype(BF16)


def _l1_qkv(h, gpre, w, ctab, stab, b, s):
    m = h.shape[0]
    nt = s // TM
    const = lambda i: (0, 0)
    out_specs, out_shape = [], []
    for _, dil in DIL_CONFIGS:
        out_specs.append(pl.BlockSpec((None, dil, TM // dil, 1536), lambda i: (i // nt, 0, i % nt, 0)))
        out_shape.append(jax.ShapeDtypeStruct((b, dil, s // dil, 1536), BF16))
    return pl.pallas_call(
        _l1_qkv_kernel, grid=(m // TM,),
        in_specs=[pl.BlockSpec((TM, D_MODEL), lambda i: (i, 0)),
                  pl.BlockSpec(gpre.shape, const),
                  pl.BlockSpec(w.shape, const, pipeline_mode=pl.Buffered(1)),
                  pl.BlockSpec((2, TM, LANES), lambda i: (0, i % nt, 0)),
                  pl.BlockSpec((2, TM, LANES), lambda i: (0, i % nt, 0))],
        out_specs=out_specs, out_shape=out_shape,
        scratch_shapes=[pltpu.VMEM((8, TM, LANES), F32)],
        compiler_params=_params(("parallel",), 48), name="l1_qkv",
    )(h, gpre, w, ctab, stab)


def _dil_attn_kernel(q0_ref, k0_ref, v0_ref, q1_ref, k1_ref, v1_ref, q2_ref, k2_ref, v2_ref, o_ref,
                     ra_ref, rl_ref, rm_ref, acc_ref, sum_ref, max_ref, bias_ref):
    seq = o_ref.shape[0]
    lo = _lane_iota((DIL_QB, LANES)) < HEAD_DIM
    groups = ((q0_ref, k0_ref, v0_ref), (q1_ref, k1_ref, v1_ref), (q2_ref, k2_ref, v2_ref))
    sets = ((acc_ref, sum_ref, max_ref), (ra_ref, rl_ref, rm_ref))
    prev_dil = None
    for gi, (q_ref, k_ref, v_ref) in enumerate(groups):
        dil, length = q_ref.shape[0], q_ref.shape[1]
        width = min(DIL_QB + 2 * DIL_HALF, length)
        nblk = length // DIL_QB
        ta, tl, tm_ = sets[gi % 2]
        row = lax.broadcasted_iota(jnp.int32, (2 * DIL_QB, width), 0) % DIL_QB
        col = lax.broadcasted_iota(jnp.int32, (2 * DIL_QB, width), 1)
        rel = row - col
        ones = jnp.ones((width, LANES), BF16)
        for case in range(1 if nblk == 1 else 3):
            bias_ref[case, :, 0:width] = jnp.where(jnp.abs(rel + case * DIL_HALF) <= DIL_HALF,
                                                   0.0, NEG_INF)

        def body(it, carry, length=length, width=width, nblk=nblk, ta=ta, tl=tl, tm_=tm_,
                 ones=ones, q_ref=q_ref, k_ref=k_ref, v_ref=v_ref):
            r = it // nblk
            i = it % nblk
            q0 = pl.multiple_of(i * DIL_QB, DIL_QB)
            start = pl.multiple_of(jnp.clip(q0 - DIL_HALF, 0, length - width), DIL_HALF)
            qrow = pl.multiple_of(r * length + q0, DIL_HALF)
            q = q_ref[r, pl.ds(q0, DIL_QB), :]
            kw = k_ref[r, pl.ds(start, width), :]
            vw = jnp.concatenate([v_ref[r, pl.ds(start, width), :], ones], axis=1)
            zero = jnp.zeros_like(q)
            q2 = jnp.concatenate([jnp.where(lo, q, zero), jnp.where(lo, zero, q)], axis=0)
            sc = lax.dot_general(q2, kw, (((1,), (1,)), ((), ())), preferred_element_type=F32)
            sc = sc + bias_ref[lax.shift_right_logical(q0 - start, 6), :, 0:width]
            m = jnp.max(sc, axis=-1, keepdims=True)
            p = jnp.exp2(sc - m).astype(BF16)
            o2 = _dot(p, vw)
            ta[pl.ds(qrow, DIL_QB), :] = jnp.where(lo, o2[:DIL_QB, 0:LANES], o2[DIL_QB:, 0:LANES])
            tl[pl.ds(qrow, DIL_QB), :] = jnp.where(lo, o2[:DIL_QB, LANES:], o2[DIL_QB:, LANES:])
            tm_[pl.ds(qrow, DIL_QB), :] = jnp.where(lo, m[:DIL_QB], m[DIL_QB:])
            return carry

        lax.fori_loop(0, dil * nblk, body, 0, unroll=8)
        if gi > 0:
            pa, ps, pm = sets[(gi - 1) % 2]
            step, prev_len = dil // prev_dil, seq // prev_dil
            for r in range(dil):
                rows = pl.ds((r % prev_dil) * prev_len + r // prev_dil, length, stride=step)
                blk = slice(r * length, (r + 1) * length)
                m_old, m_blk = pm[rows, :], tm_[blk, :]
                m_new = jnp.maximum(m_old, m_blk)
                w_old, w_blk = jnp.exp2(m_old - m_new), jnp.exp2(m_blk - m_new)
                ta[blk, :] = pa[rows, :] * w_old + ta[blk, :] * w_blk
                tl[blk, :] = ps[rows, :] * w_old + tl[blk, :] * w_blk
                tm_[blk, :] = m_new
        prev_dil = dil
    fa, fs, _ = sets[(len(groups) - 1) % 2]
    stage = sets[len(groups) % 2][2]
    length = seq // prev_dil
    for r in range(prev_dil):
        blk = slice(r * length, (r + 1) * length)
        stage[pl.ds(r, length, stride=prev_dil), :] = fa[blk, :] / fs[blk, :]
    o_ref[...] = stage[...].astype(BF16)


def _dil_attn(qkvs, b, s):
    in_specs, args = [], []
    for x in qkvs:
        for j in range(3):
            in_specs.append(pl.BlockSpec((None,) + x.shape[1:3] + (LANES,),
                                         lambda bi, hp, j=j: (bi, 0, 0, 4 * j + hp)))
            args.append(x)
    stat = pltpu.VMEM((s, LANES), F32)
    return pl.pallas_call(
        _dil_attn_kernel, grid=(b, 4), in_specs=in_specs,
        out_specs=pl.BlockSpec((s, LANES), lambda bi, hp: (bi, hp)),
        out_shape=jax.ShapeDtypeStruct((b * s, 512), BF16),
        scratch_shapes=[stat] * 6 + [pltpu.VMEM((3, 2 * DIL_QB, DIL_QB + 2 * DIL_HALF), F32)],
        compiler_params=_params(("parallel", "parallel"), 40), name="dil_attn",
    )(*args)


def _l1_router_kernel(o_ref, wo_ref, gmix_ref, h_ref, gpre_ref, wr_ref,
                      h1_ref, a_ref, idx_ref, gate_ref, cnt_ref, carry_ref):
    @pl.when(pl.program_id(0) == 0)
    def _():
        carry_ref[...] = jnp.zeros_like(carry_ref)

    h1 = h_ref[...] + _rms(_dot(o_ref[...], wo_ref[...]), gmix_ref[...])
    h1_ref[...] = h1
    a = _rms(h1, gpre_ref[...]).astype(BF16)
    a_ref[...] = a
    logits = _dot(a, wr_ref[...])
    tm = logits.shape[0]
    lane = _lane_iota(logits.shape)
    lanef = lane.astype(F32)
    lg = jnp.where(lane < N_EXPERTS, logits, -jnp.inf)
    m1 = jnp.max(lg, axis=-1, keepdims=True)
    i1 = jnp.min(jnp.where(lg == m1, lanef, float(LANES)), axis=-1, keepdims=True)
    lg2 = jnp.where(lanef == i1, -jnp.inf, lg)
    m2 = jnp.max(lg2, axis=-1, keepdims=True)
    i2 = jnp.min(jnp.where(lg2 == m2, lanef, float(LANES)), axis=-1, keepdims=True)
    t = jnp.exp(m2 - m1)
    g1 = 1.0 / (1.0 + t)
    g2 = t / (1.0 + t)
    hit1 = lanef == i1
    hit2 = lanef == i2
    onehot = jnp.where(hit1 | hit2, 1.0, 0.0)
    r = lax.broadcasted_iota(jnp.int32, (tm, tm), 0)
    c = lax.broadcasted_iota(jnp.int32, (tm, tm), 1)
    tri = jnp.where(c < r, 1.0, 0.0).astype(BF16)
    before = _dot(tri, onehot.astype(BF16)) + carry_ref[...]
    rank1 = jnp.sum(jnp.where(hit1, before, 0.0), axis=-1, keepdims=True)
    rank2 = jnp.sum(jnp.where(hit2, before, 0.0), axis=-1, keepdims=True)
    info = jnp.where(lane == 0, i1, jnp.where(lane == 1, i2, jnp.where(lane == 2, rank1, rank2)))
    idx_ref[...] = info.astype(jnp.int32)
    gate_ref[...] = jnp.where(lane == 0, g1, g2)
    carry_ref[...] += jnp.sum(onehot, axis=0, keepdims=True)
    cnt_ref[...] = carry_ref[...]


def _l1_router(o, wo, gmix, h, gpre, wr):
    m = h.shape[0]
    row = lambda i: (i, 0)
    const = lambda i: (0, 0)
    full = lambda x: pl.BlockSpec(x.shape, const)
    return pl.pallas_call(
        _l1_router_kernel, grid=(m // TM,),
        in_specs=[pl.BlockSpec((TM, o.shape[1]), row), full(wo), full(gmix),
                  pl.BlockSpec((TM, D_MODEL), row), full(gpre), full(wr)],
        out_specs=[pl.BlockSpec((TM, D_MODEL), row), pl.BlockSpec((TM, D_MODEL), row),
                   pl.BlockSpec((TM, LANES), row), pl.BlockSpec((TM, LANES), row),
                   pl.BlockSpec((1, LANES), const)],
        out_shape=[jax.ShapeDtypeStruct((m, D_MODEL), F32),
                   jax.ShapeDtypeStruct((m, D_MODEL), BF16),
                   jax.ShapeDtypeStruct((m, LANES), jnp.int32),
                   jax.ShapeDtypeStruct((m, LANES), F32),
                   jax.ShapeDtypeStruct((1, LANES), F32)],
        scratch_shapes=[pltpu.VMEM((1, LANES), F32)],
        compiler_params=_params(("arbitrary",), 32), name="l1_router",
    )(o, wo, gmix, h, gpre, wr)


def _expert_kernel(te_ref, tv_ref, x_ref, wg_ref, wu_ref, wd_ref, *refs):
    y_ref, acc_ref = refs[-2:]
    j = pl.program_id(1)
    t = pl.program_id(2)
    last = pl.num_programs(1) - 1
    valid = tv_ref[pl.program_id(0) * MOE_GROUP + t] > 0

    @pl.when(valid & (j == 0))
    def _():
        acc_ref[t] = jnp.zeros(acc_ref.shape[1:], F32)

    @pl.when(valid)
    def _():
        x = x_ref[...]
        gate = _dot(x, wg_ref[...].astype(BF16))
        up = _dot(x, wu_ref[...].astype(BF16))
        acc_ref[t] += _dot((_silu(gate) * up).astype(BF16), wd_ref[...].astype(BF16))

    @pl.when(valid & (j == last))
    def _():
        y_ref[...] = acc_ref[t].astype(BF16)

    @pl.when(jnp.logical_not(valid) & (j == last))
    def _():
        y_ref[...] = jnp.zeros_like(y_ref)


def _experts(tile_e, tile_valid, xs, wg, wu, wd, ys, first_tile, n_slots):
    n_tiles = xs.shape[0] // TM_MOE
    n_j = wg.shape[2] // TF
    te = tile_e[first_tile:first_tile + n_tiles]
    tv = tile_valid[first_tile:first_tile + n_tiles]
    tile = lambda g, t: g * MOE_GROUP + t
    in_specs = [
        pl.BlockSpec((TM_MOE, D_MODEL), lambda g, j, t, te, tv: (tile(g, t), 0)),
        pl.BlockSpec((None, D_MODEL, TF),
                     lambda g, j, t, te, tv: (te[tile(g, t)], 0, j * tv[tile(g, t)])),
        pl.BlockSpec((None, D_MODEL, TF),
                     lambda g, j, t, te, tv: (te[tile(g, t)], 0, j * tv[tile(g, t)])),
        pl.BlockSpec((None, TF, D_MODEL),
                     lambda g, j, t, te, tv: (te[tile(g, t)], j * tv[tile(g, t)], 0))]
    args = [te, tv, xs, wg, wu, wd]
    aliases = {}
    if ys is not None:
        in_specs.append(pl.BlockSpec(memory_space=pl.ANY))
        aliases = {len(args): 0}
        args.append(ys)
    out_spec = pl.BlockSpec(
        (TM_MOE, D_MODEL),
        lambda g, j, t, te, tv: (first_tile + g * MOE_GROUP + jnp.where(j == n_j - 1, t, 0), 0))
    grid_spec = pltpu.PrefetchScalarGridSpec(
        num_scalar_prefetch=2, grid=(n_tiles // MOE_GROUP, n_j, MOE_GROUP), in_specs=in_specs,
        out_specs=out_spec,
        scratch_shapes=[pltpu.VMEM((MOE_GROUP, TM_MOE, D_MODEL), F32)])
    return pl.pallas_call(
        _expert_kernel, grid_spec=grid_spec,
        out_shape=jax.ShapeDtypeStruct((n_slots, D_MODEL), BF16),
        input_output_aliases=aliases,
        compiler_params=_params(("parallel", "arbitrary", "arbitrary"), 56), name="moe_experts",
    )(*args)


def _combine_ple_kernel(y0_ref, y1_ref, gate_ref, g_ref, h_ref, p_ref, ggn_ref, wpg_ref, wpp_ref,
                        gple_ref, out_ref):
    gate = gate_ref[...]
    f = y0_ref[...].astype(F32) * gate[:, 0:1] + y1_ref[...].astype(F32) * gate[:, 1:2]
    h2 = h_ref[...] + _rms(f, g_ref[...])
    out_ref[...] = _ple_update(h2, p_ref[...], ggn_ref[...], wpg_ref[...], wpp_ref[...], gple_ref[...])


def _combine_ple(y01, gate, g, h, p, ggn, wpg, wpp, gple):
    m = h.shape[0]
    nt = m // TM
    row = lambda i: (i, 0)
    const = lambda i: (0, 0)
    full = lambda x: pl.BlockSpec(x.shape, const)
    act = pl.BlockSpec((TM, D_MODEL), row)
    return pl.pallas_call(
        _combine_ple_kernel, grid=(nt,),
        in_specs=[act, pl.BlockSpec((TM, D_MODEL), lambda i: (nt + i, 0)),
                  pl.BlockSpec((TM, LANES), row), full(g), act,
                  pl.BlockSpec((None, TM, PLE_DIM), lambda i: (1, i, 0)), full(ggn), full(wpg),
                  full(wpp), full(gple)],
        out_specs=act,
        out_shape=jax.ShapeDtypeStruct((m, D_MODEL), F32),
        compiler_params=_params(("parallel",), 40), name="combine_ple",
    )(y01, y01, gate, g, h, p, ggn, wpg, wpp, gple)


def _rows(x, idx):
    return x.at[idx].get(mode='promise_in_bounds')


def _moe_plan(idx, cnt, n):
    counts = cnt[0, :N_EXPERTS].astype(jnp.int32)
    padded = (counts + TM_MOE - 1) // TM_MOE * TM_MOE
    end_pad = jnp.cumsum(padded)
    start_pad = end_pad - padded
    experts = jnp.arange(N_EXPERTS, dtype=jnp.int32)
    first = lambda e: jnp.sum(jnp.where(e[:, None] == experts, start_pad, 0), axis=-1)
    dest = (first(idx[:, 0]) + idx[:, 2], first(idx[:, 1]) + idx[:, 3])
    n_slots = n * TOP_K + N_EXPERTS * TM_MOE
    pair = 2 * jnp.arange(n, dtype=jnp.int32)
    keys = jnp.concatenate([idx[:, 0] * (2 * n) + pair, idx[:, 1] * (2 * n) + pair + 1])
    by_expert = jnp.concatenate([(jnp.sort(keys) % (2 * n)) // 2, jnp.zeros((n,), jnp.int32)])
    start_run = jnp.cumsum(counts) - counts
    slot_tok = jnp.zeros((n_slots + n,), jnp.int32)
    for e in range(N_EXPERTS):
        run = lax.dynamic_slice(by_expert, (start_run[e],), (n,))
        slot_tok = lax.dynamic_update_slice(slot_tok, run, (start_pad[e],))
    slot_tok = slot_tok[:n_slots]
    tile_start = jnp.arange(n_slots // TM_MOE, dtype=jnp.int32) * TM_MOE
    tile_e = jnp.minimum(jnp.sum(tile_start[:, None] >= end_pad[None, :], axis=1),
                         N_EXPERTS - 1).astype(jnp.int32)
    tile_valid = (tile_start < end_pad[-1]).astype(jnp.int32)
    return dest, slot_tok, tile_e, tile_valid


def _rotary_tables(pos, dim, theta):
    exponent = jnp.arange(0, dim, 2, dtype=F32) / dim
    inv_freq = jnp.power(jnp.float32(theta), -exponent)
    ang = pos.astype(F32)[:, None] * inv_freq[None, :]
    return jnp.cos(ang), jnp.sin(ang)


def _tables(s):
    t = jnp.arange(s, dtype=jnp.int32)
    cr, sr = _rotary_tables(t // GRID_W, HEAD_DIM // 2, AXIAL_THETA)
    cc, sc = _rotary_tables(t % GRID_W, HEAD_DIM // 2, AXIAL_THETA)
    ca = jnp.tile(jnp.concatenate([cr, cr, cc, cc], axis=1), (1, 2))
    sa = jnp.tile(jnp.concatenate([-sr, sr, -sc, sc], axis=1), (1, 2))
    cm_, sm_ = _rotary_tables(t, MLA_ROPE_DIM, MLA_ROPE_THETA)
    one = jnp.ones((s, 1), F32)
    zero = jnp.zeros((s, 1), F32)
    cm = jnp.concatenate([jnp.tile(one, (1, 64)), cm_, cm_, jnp.tile(one, (1, 32))], axis=1)
    sm = jnp.concatenate([jnp.tile(zero, (1, 64)), -sm_, sm_, jnp.tile(zero, (1, 32))], axis=1)
    ckr = jnp.concatenate([cm_, cm_, jnp.tile(zero, (1, 96))], axis=1)
    skr = jnp.concatenate([-sm_, sm_, jnp.tile(zero, (1, 96))], axis=1)
    cp_, sp_ = _rotary_tables(t, PARTIAL_ROPE_DIM, ROPE_THETA)
    cp = jnp.tile(jnp.concatenate([cp_, cp_, jnp.tile(one, (1, 48))], axis=1), (1, 2))
    sp = jnp.tile(jnp.concatenate([-sp_, sp_, jnp.tile(zero, (1, 48))], axis=1), (1, 2))
    scale = HEAD_DIM ** -0.5 * LOG2_E
    ctab = jnp.stack([cp * scale, cp])
    stab = jnp.stack([sp * scale, sp])
    return ca, sa, cm, sm, ckr, skr, ctab, stab


_GQA_PERM = (0, 4, 1, 5, 2, 6, 3, 7)


def _l0_weights(w_in, w_uq, w_ukv, w_out):
    d = w_in.shape[0]
    q = w_in[:, :512].reshape(d, 8, 64)[:, _GQA_PERM, :].reshape(d, 512)
    win = jnp.concatenate([q, w_in[:, 512:], jnp.zeros((d, AB_IN_PAD - w_in.shape[1]), w_in.dtype)],
                          axis=1).astype(BF16)
    uq = w_uq.reshape(MLA_Q_RANK, MLA_HEADS, MLA_NOPE_DIM + MLA_ROPE_DIM)
    wuq = jnp.pad(uq, ((0, 0), (0, 0), (0, LANES - uq.shape[2]))).reshape(MLA_Q_RANK, -1).astype(BF16)
    ukv = w_ukv.reshape(MLA_KV_RANK, MLA_HEADS, MLA_NOPE_DIM + MLA_V_DIM)
    wk = jnp.pad(ukv[:, :, :MLA_NOPE_DIM], ((0, 0), (0, 0), (0, LANES - MLA_NOPE_DIM)))
    wk = wk.reshape(MLA_KV_RANK, -1).astype(BF16)
    wv = ukv[:, :, MLA_NOPE_DIM:].reshape(MLA_KV_RANK, -1).astype(BF16)
    src = jnp.arange(LANES)[:, None]
    dst = jnp.arange(MLA_HEADS * LANES)[None, :]
    wp = ((src < MLA_ROPE_DIM) & (dst % LANES == src + MLA_NOPE_DIM)).astype(BF16)
    oa = w_out[:512].reshape(8, 64, -1)[_GQA_PERM, :, :].reshape(512, -1)
    wout = jnp.concatenate([oa, w_out[512:]], axis=0).astype(BF16)
    return win, wuq, wk, wp, wv, wout


def kernel(x, p, mix_pre_g, mix_post_g, ffn_pre_g, ffn_post_g, ple_w_proj, ple_gate_norm_g, ple_w_gate, ple_post_g, ab_w_in, gqa_q_norm_g, gqa_k_norm_g, mla_q_norm_g, mla_w_uq, mla_kv_norm_g, mla_w_ukv, ab_w_out, ffn_w_gate, ffn_w_up, ffn_w_down, dil_w_qkv, dil_w_out, moe_w_router, moe_w_gate, moe_w_up, moe_w_down):
    b, s, d = x.shape
    m = b * s
    row = lambda g: g.reshape(1, -1).astype(F32)
    ca, sa, cm, sm, ckr, skr, ctab, stab = _tables(s)
    h = x.reshape(m, d)
    pf = p.reshape(p.shape[0], m, PLE_DIM)

    win, wuq, wk, wp, wv, wout = _l0_weights(ab_w_in[0], mla_w_uq[0], mla_w_ukv[0], ab_w_out[0])
    gq = jnp.tile(row(gqa_q_norm_g[0]), (1, 2))
    gk = jnp.tile(row(gqa_k_norm_g[0]), (1, 2))
    qa, ka, vat, qm, km, vmt = _l0_in(h, row(mix_pre_g[0]), win, gq, gk, ca, sa,
                                      row(mla_q_norm_g[0]), wuq, cm, sm,
                                      row(mla_kv_norm_g[0]), wk, wp, wv, ckr, skr, b, s)
    oa = _gqa_attn(qa, ka, vat, b, s)
    om = _mla_attn(qm, km, vmt, b, s)
    h = _l0_tail(oa, om, wout, row(mix_post_g[0]), h, row(ffn_pre_g[0]),
                 ffn_w_gate[0].astype(BF16), ffn_w_up[0].astype(BF16), ffn_w_down[0].astype(BF16),
                 row(ffn_post_g[0]), pf, row(ple_gate_norm_g[0]), ple_w_gate[0].astype(BF16),
                 ple_w_proj[0].astype(BF16), row(ple_post_g[0]))

    qkvs = _l1_qkv(h, row(mix_pre_g[1]), dil_w_qkv[0].astype(BF16), ctab, stab, b, s)
    o = _dil_attn(qkvs, b, s)
    wr = jnp.pad(moe_w_router[0].astype(BF16), ((0, 0), (0, LANES - N_EXPERTS)))
    h, a, idx, gate, cnt = _l1_router(o, dil_w_out[0].astype(BF16), row(mix_post_g[1]), h,
                                      row(ffn_pre_g[1]), wr)
    dest, slot_tok, tile_e, tile_valid = _moe_plan(idx, cnt, m)
    n_slots = slot_tok.shape[0]
    ys, first_tile = None, 0
    n_groups = n_slots // (TM_MOE * MOE_GROUP)
    sizes = [n_groups * share // sum(MOE_CHUNKS) for share in MOE_CHUNKS]
    sizes[-1] += n_groups - sum(sizes)
    for n_tiles in [size * MOE_GROUP for size in sizes if size > 0]:
        rows = slice(first_tile * TM_MOE, (first_tile + n_tiles) * TM_MOE)
        ys = _experts(tile_e, tile_valid, _rows(a, slot_tok[rows]), moe_w_gate[0], moe_w_up[0],
                      moe_w_down[0], ys, first_tile, n_slots)
        first_tile += n_tiles
    h = _combine_ple(_rows(ys, jnp.concatenate(dest)), gate, row(ffn_post_g[1]), h, pf,
                     row(ple_gate_norm_g[1]), ple_w_gate[1].astype(BF16), ple_w_proj[1].astype(BF16),
                     row(ple_post_g[1]))
    return h.reshape(b, s, d)
```

```python
import functools

import jax
import jax.numpy as jnp
from jax import lax
from jax.experimental import pallas as pl
from jax.experimental.pallas import tpu as pltpu

F32 = jnp.float32
BF16 = jnp.bfloat16

D_MODEL = 1024
GRID_W = 64
HEAD_DIM = 64
NORM_EPS = 1e-6
NEG_INF = -1e30
ROPE_THETA = 500000.0
PARTIAL_ROPE_DIM = HEAD_DIM // 4
GQA_Q_HEADS = 8
GQA_KV_HEADS = 2
AXIAL_THETA = 10000.0
MLA_HEADS = 8
MLA_Q_RANK = 256
MLA_KV_RANK = 128
MLA_NOPE_DIM = 64
MLA_ROPE_DIM = 32
MLA_V_DIM = 64
MLA_ROPE_THETA = 10000.0
DIL_CONFIGS = ((128, 1), (512, 4), (2048, 16))
DIL_GROUPS = len(DIL_CONFIGS)
DIL_HEADS = 8
DIL_HALF = 64
DIL_QB = 128
LOG2_E = 1.4426950408889634
DIL_IN_WIDTH = DIL_GROUPS * 3 * DIL_HEADS * HEAD_DIM
D_FF = 3584
N_EXPERTS = 8
TOP_K = 2
PLE_DIM = 256

LANES = 128
VMEM_BYTES = 64 * 1024 * 1024
MIB = 1024 * 1024

TM = 512
TQ = 1024
KEY_CHUNK = 256
PV_LAG = 2
V_ONES = 16
V_ROWS = LANES + V_ONES
TF = 512
TM_FFN = 1024
TAIL_ROWS = 256
TM_MOE = 1024
AB_IN_PAD = 1280
MOE_GROUP = 4
MOE_CHUNKS = (1, 2, 3, 3)


def _params(semantics, vmem_mib):
    return pltpu.CompilerParams(dimension_semantics=semantics,
                                vmem_limit_bytes=vmem_mib * MIB)


def _dot(a, b):
    return jnp.dot(a, b, preferred_element_type=F32)


def _rms(x, g):
    return x * lax.rsqrt(jnp.mean(x * x, axis=-1, keepdims=True) + NORM_EPS) * g


def _lane_iota(shape):
    return lax.broadcasted_iota(jnp.int32, shape, len(shape) - 1)


def _swap_halves(x, k):
    w = x.shape[-1]
    fwd = pltpu.roll(x, w - k, 1)
    bwd = pltpu.roll(x, k, 1)
    return jnp.where((_lane_iota(x.shape) % (2 * k)) < k, fwd, bwd)


def _rotate(x, c, s, k):
    return x * c + _swap_halves(x, k) * s


def _head_rms(x, g):
    lo = _lane_iota(x.shape) < HEAD_DIM
    x2 = x * x
    s_all = jnp.sum(x2, axis=-1, keepdims=True)
    s_lo = jnp.sum(jnp.where(lo, x2, 0.0), axis=-1, keepdims=True)
    ms = jnp.where(lo, s_lo, s_all - s_lo) * (1.0 / HEAD_DIM)
    return x * lax.rsqrt(ms + NORM_EPS) * g


def _l0_in_kernel(h_ref, gpre_ref, win_ref, gq_ref, gk_ref, ca_ref, sa_ref,
                  gmq_ref, wuq_ref, cm_ref, sm_ref, gmkv_ref, wk_ref, wp_ref, wv_ref,
                  ckr_ref, skr_ref,
                  qa_ref, ka_ref, vat_ref, qm_ref, km_ref, vmt_ref):
    a = _rms(h_ref[...], gpre_ref[...]).astype(BF16)
    z = _dot(a, win_ref[...])
    ca, sa = ca_ref[...], sa_ref[...]
    for j in range(4):
        xq = _head_rms(z[:, j * LANES:(j + 1) * LANES], gq_ref[...])
        qa_ref[:, j * LANES:(j + 1) * LANES] = (
            _rotate(xq, ca, sa, 16) * (HEAD_DIM ** -0.5 * LOG2_E)).astype(BF16)
    ka_ref[...] = _rotate(_head_rms(z[:, 512:640], gk_ref[...]), ca, sa, 16).astype(BF16)
    ones = jnp.ones((V_ONES, z.shape[0]), BF16)
    vat_ref[0:LANES, :] = z[:, 640:768].T.astype(BF16)
    vat_ref[LANES:, :] = ones
    cq = _rms(z[:, 768:1024], gmq_ref[...]).astype(BF16)
    qm = _dot(cq, wuq_ref[...])
    cm, sm = cm_ref[...], sm_ref[...]
    scale_m = (MLA_NOPE_DIM + MLA_ROPE_DIM) ** -0.5 * LOG2_E
    for hb in range(MLA_HEADS):
        x = qm[:, hb * LANES:(hb + 1) * LANES]
        qm_ref[:, hb * LANES:(hb + 1) * LANES] = (_rotate(x, cm, sm, 16) * scale_m).astype(BF16)
    ckv = _rms(z[:, 1024:1152], gmkv_ref[...]).astype(BF16)
    kr = _rotate(z[:, 1152:1280], ckr_ref[...], skr_ref[...], 16).astype(BF16)
    km = _dot(ckv, wk_ref[...]) + _dot(kr, wp_ref[...])
    km_ref[...] = km.astype(BF16)
    vm = _dot(ckv, wv_ref[...])
    for j in range(4):
        vmt_ref[j * V_ROWS:j * V_ROWS + LANES, :] = vm[:, j * LANES:(j + 1) * LANES].T.astype(BF16)
        vmt_ref[j * V_ROWS + LANES:(j + 1) * V_ROWS, :] = ones


def _l0_in(h, gpre, win, gq, gk, ca, sa, gmq, wuq, cm, sm, gmkv, wk, wp, wv, ckr, skr, b, s):
    m = h.shape[0]
    nt = s // TM
    row = lambda i: (i, 0)
    const = lambda i: (0, 0)
    tab = lambda i: (i % nt, 0)
    full = lambda a: pl.BlockSpec(a.shape, const)
    in_specs = [pl.BlockSpec((TM, D_MODEL), row), full(gpre), full(win), full(gq), full(gk),
                pl.BlockSpec((TM, LANES), tab), pl.BlockSpec((TM, LANES), tab),
                full(gmq), full(wuq), pl.BlockSpec((TM, LANES), tab), pl.BlockSpec((TM, LANES), tab),
                full(gmkv), full(wk), full(wp), full(wv),
                pl.BlockSpec((TM, LANES), tab), pl.BlockSpec((TM, LANES), tab)]
    tr = lambda i: (i // nt, 0, i % nt)
    out_specs = [pl.BlockSpec((TM, 512), row),
                 pl.BlockSpec((TM, LANES), row),
                 pl.BlockSpec((None, V_ROWS, TM), tr),
                 pl.BlockSpec((TM, 1024), row),
                 pl.BlockSpec((TM, 1024), row),
                 pl.BlockSpec((None, 4 * V_ROWS, TM), tr)]
    out_shape = [jax.ShapeDtypeStruct((m, 512), BF16),
                 jax.ShapeDtypeStruct((m, LANES), BF16),
                 jax.ShapeDtypeStruct((b, V_ROWS, s), BF16),
                 jax.ShapeDtypeStruct((m, 1024), BF16),
                 jax.ShapeDtypeStruct((m, 1024), BF16),
                 jax.ShapeDtypeStruct((b, 4 * V_ROWS, s), BF16)]
    return pl.pallas_call(
        _l0_in_kernel, grid=(m // TM,), in_specs=in_specs, out_specs=out_specs,
        out_shape=out_shape, compiler_params=_params(("parallel",), 48), name="l0_in",
    )(h, gpre, win, gq, gk, ca, sa, gmq, wuq, cm, sm, gmkv, wk, wp, wv, ckr, skr)


def _attend_pair(score_t, vt_ref, o_ref):
    n_chunks = vt_ref.shape[1] // KEY_CHUNK
    tq = o_ref.shape[0]
    keys = [slice(c * KEY_CHUNK, (c + 1) * KEY_CHUNK) for c in range(n_chunks)]
    parts = []
    s_next = score_t(keys[0])
    pending = []
    for c in range(n_chunks):
        s_t = s_next
        if c + 1 < n_chunks:
            s_next = score_t(keys[c + 1])
        m = jnp.max(s_t, axis=0, keepdims=True)
        pending.append((c, m, jnp.exp2(s_t - m).astype(BF16)))
        if len(pending) > PV_LAG:
            c0, m0, p0 = pending.pop(0)
            parts.append((m0, _dot(vt_ref[:, keys[c0]], p0)))
    for c0, m0, p0 in pending:
        parts.append((m0, _dot(vt_ref[:, keys[c0]], p0)))
    m_all = functools.reduce(jnp.maximum, [m for m, _ in parts])
    o_t = sum(o_c * jnp.exp2(m_c - m_all) for m_c, o_c in parts)
    o_t = o_t[0:LANES, :] / o_t[LANES:LANES + 1, :]
    first = lax.broadcasted_iota(jnp.int32, (LANES, tq), 0) < HEAD_DIM
    o_ref[...] = jnp.where(first, o_t[:, 0:tq], o_t[:, tq:]).T.astype(BF16)


def _transposed(q):
    return q.astype(F32).T.astype(BF16)


def _gqa_attn_kernel(q_ref, k_ref, vt_ref, o_ref):
    qt = _transposed(q_ref[...])
    first = lax.broadcasted_iota(jnp.int32, qt.shape, 0) < HEAD_DIM
    zero = jnp.zeros_like(qt)
    w = jnp.concatenate([jnp.where(first, qt, zero), jnp.where(first, zero, qt)], axis=1)
    _attend_pair(lambda keys: _dot(k_ref[keys, :], w), vt_ref, o_ref)


def _gqa_attn(qa, ka, vat, b, s):
    m = qa.shape[0]
    nq = s // TQ
    return pl.pallas_call(
        _gqa_attn_kernel, grid=(b, nq, 4),
        in_specs=[pl.BlockSpec((TQ, LANES), lambda bi, i, j: (bi * nq + i, j)),
                  pl.BlockSpec((s, LANES), lambda bi, i, j: (bi, 0)),
                  pl.BlockSpec((None, V_ROWS, s), lambda bi, i, j: (bi, 0, 0))],
        out_specs=pl.BlockSpec((TQ, LANES), lambda bi, i, j: (bi * nq + i, j)),
        out_shape=jax.ShapeDtypeStruct((m, 512), BF16),
        compiler_params=_params(("parallel", "parallel", "parallel"), 32), name="gqa_attn",
    )(qa, ka, vat)


def _mla_attn_kernel(q_ref, k_ref, vt_ref, o_ref):
    qt0 = _transposed(q_ref[:, 0:LANES])
    qt1 = _transposed(q_ref[:, LANES:2 * LANES])

    def score_t(keys):
        return jnp.concatenate([_dot(k_ref[keys, 0:LANES], qt0),
                                _dot(k_ref[keys, LANES:2 * LANES], qt1)], axis=1)

    _attend_pair(score_t, vt_ref, o_ref)


def _mla_attn(qm, km, vmt, b, s):
    m = qm.shape[0]
    nq = s // TQ
    return pl.pallas_call(
        _mla_attn_kernel, grid=(b, nq, 4),
        in_specs=[pl.BlockSpec((TQ, 2 * LANES), lambda bi, i, j: (bi * nq + i, j)),
                  pl.BlockSpec((s, 2 * LANES), lambda bi, i, j: (bi, j)),
                  pl.BlockSpec((None, V_ROWS, s), lambda bi, i, j: (bi, j, 0))],
        out_specs=pl.BlockSpec((TQ, LANES), lambda bi, i, j: (bi * nq + i, j)),
        out_shape=jax.ShapeDtypeStruct((m, 512), BF16),
        compiler_params=_params(("parallel", "parallel", "parallel"), 32), name="mla_attn",
    )(qm, km, vmt)


def _silu(x):
    return x * (1.0 / (1.0 + jnp.exp(-x)))


def _ple_update(h, p, ggn, wg, wp, gpost):
    gate = _dot(_rms(h, ggn).astype(BF16), wg)
    gate = 1.0 / (1.0 + jnp.exp(-gate))
    e = _dot(p.astype(BF16), wp) * gate
    return h + _rms(e, gpost)


def _l0_tail_kernel(oa_ref, om_ref, wo_ref, gmix_ref, h_ref, gpre_ref, wg_ref, wu_ref, wd_ref,
                    gpost_ref, p_ref, ggn_ref, wpg_ref, wpp_ref, gple_ref, out_ref, a_ref, acc_ref):
    j = pl.program_id(1)

    chunks = [slice(r, r + TAIL_ROWS) for r in range(0, out_ref.shape[0], TAIL_ROWS)]

    @pl.when(j == 0)
    def _():
        k = oa_ref.shape[1]
        for rows in chunks:
            y = _dot(oa_ref[rows, :], wo_ref[0:k, :]) + _dot(om_ref[rows, :], wo_ref[k:, :])
            h1 = h_ref[rows, :] + _rms(y, gmix_ref[...])
            out_ref[rows, :] = h1
            a_ref[rows, :] = _rms(h1, gpre_ref[...]).astype(BF16)
        acc_ref[...] = jnp.zeros_like(acc_ref)

    a = a_ref[...]
    mid = (_silu(_dot(a, wg_ref[...])) * _dot(a, wu_ref[...])).astype(BF16)
    acc_ref[...] += _dot(mid, wd_ref[...])

    @pl.when(j == pl.num_programs(1) - 1)
    def _():
        for rows in chunks:
            h2 = out_ref[rows, :] + _rms(acc_ref[rows, :], gpost_ref[...])
            out_ref[rows, :] = _ple_update(h2, p_ref[rows, :], ggn_ref[...], wpg_ref[...],
                                           wpp_ref[...], gple_ref[...])


def _l0_tail(oa, om, wo, gmix, h, gpre, wg, wu, wd, gpost, p, ggn, wpg, wpp, gple):
    m = h.shape[0]
    dff = wg.shape[1]
    row = lambda i, j: (i, 0)
    const = lambda i, j: (0, 0)
    vec = lambda g: pl.BlockSpec(g.shape, const)
    once = lambda w: pl.BlockSpec(w.shape, const, pipeline_mode=pl.Buffered(1))
    return pl.pallas_call(
        _l0_tail_kernel, grid=(m // TM_FFN, dff // TF),
        in_specs=[pl.BlockSpec((TM_FFN, oa.shape[1]), row), pl.BlockSpec((TM_FFN, om.shape[1]), row),
                  once(wo), vec(gmix), pl.BlockSpec((TM_FFN, D_MODEL), row), vec(gpre),
                  pl.BlockSpec((D_MODEL, TF), lambda i, j: (0, j)),
                  pl.BlockSpec((D_MODEL, TF), lambda i, j: (0, j)),
                  pl.BlockSpec((TF, D_MODEL), lambda i, j: (j, 0)),
                  vec(gpost), pl.BlockSpec((None, TM_FFN, PLE_DIM), lambda i, j: (0, i, 0)), vec(ggn), once(wpg), once(wpp),
                  vec(gple)],
        out_specs=pl.BlockSpec((TM_FFN, D_MODEL), row),
        out_shape=jax.ShapeDtypeStruct((m, D_MODEL), F32),
        scratch_shapes=[pltpu.VMEM((TM_FFN, D_MODEL), BF16), pltpu.VMEM((TM_FFN, D_MODEL), F32)],
        compiler_params=_params(("parallel", "arbitrary"), 56), name="l0_tail",
    )(oa, om, wo, gmix, h, gpre, wg, wu, wd, gpost, p, ggn, wpg, wpp, gple)


def _l1_qkv_kernel(h_ref, gpre_ref, w_ref, c_ref, s_ref, out0_ref, out1_ref, out2_ref, stage_ref):
    a = _rms(h_ref[...], gpre_ref[...]).astype(BF16)
    tm = a.shape[0]
    outs = (out0_ref, out1_ref, out2_ref)
    for blk in range(DIL_IN_WIDTH // 512):
        gi, kind = divmod(blk, 3)
        dil = DIL_CONFIGS[gi][1]
        z = _dot(a, w_ref[:, blk * 512:(blk + 1) * 512])
        if kind < 2:
            c, s = c_ref[kind], s_ref[kind]
            z = jnp.concatenate(
                [_rotate(z[:, sub * LANES:(sub + 1) * LANES], c, s, PARTIAL_ROPE_DIM // 2)
                 for sub in range(4)], axis=1)
        cols = slice(kind * 512, (kind + 1) * 512)
        if dil == 1:
            outs[gi][0, :, cols] = z.astype(BF16)
            continue
        for sub in range(4):
            stage = stage_ref.at[(blk % 2) * 4 + sub]
            stage[...] = z[:, sub * LANES:(sub + 1) * LANES]
            lanes = slice(kind * 512 + sub * LANES, kind * 512 + (sub + 1) * LANES)
            for r in range(dil):
                outs[gi][r, :, lanes] = stage[pl.ds(r, tm // dil, stride=dil), :].astype(BF16)


def _l1_qkv(h, gpre, w, ctab, stab, b, s):
    m = h.shape[0]
    nt = s // TM
    const = lambda i: (0, 0)
    out_specs, out_shape = [], []
    for _, dil in DIL_CONFIGS:
        out_specs.append(pl.BlockSpec((None, dil, TM // dil, 1536), lambda i: (i // nt, 0, i % nt, 0)))
        out_shape.append(jax.ShapeDtypeStruct((b, dil, s // dil, 1536), BF16))
    return pl.pallas_call(
        _l1_qkv_kernel, grid=(m // TM,),
        in_specs=[pl.BlockSpec((TM, D_MODEL), lambda i: (i, 0)),
                  pl.BlockSpec(gpre.shape, const),
                  pl.BlockSpec(w.shape, const, pipeline_mode=pl.Buffered(1)),
                  pl.BlockSpec((2, TM, LANES), lambda i: (0, i % nt, 0)),
                  pl.BlockSpec((2, TM, LANES), lambda i: (0, i % nt, 0))],
        out_specs=out_specs, out_shape=out_shape,
        scratch_shapes=[pltpu.VMEM((8, TM, LANES), F32)],
        compiler_params=_params(("parallel",), 48), name="l1_qkv",
    )(h, gpre, w, ctab, stab)


def _dil_attn_kernel(q0_ref, k0_ref, v0_ref, q1_ref, k1_ref, v1_ref, q2_ref, k2_ref, v2_ref, o_ref,
                     ra_ref, rl_ref, rm_ref, acc_ref, sum_ref, max_ref, bias_ref):
    seq = o_ref.shape[0]
    lo = _lane_iota((DIL_QB, LANES)) < HEAD_DIM
    groups = ((q0_ref, k0_ref, v0_ref), (q1_ref, k1_ref, v1_ref), (q2_ref, k2_ref, v2_ref))
    sets = ((acc_ref, sum_ref, max_ref), (ra_ref, rl_ref, rm_ref))
    prev_dil = None
    for gi, (q_ref, k_ref, v_ref) in enumerate(groups):
        dil, length = q_ref.shape[0], q_ref.shape[1]
        width = min(DIL_QB + 2 * DIL_HALF, length)
        nblk = length // DIL_QB
        ta, tl, tm_ = sets[gi % 2]
        row = lax.broadcasted_iota(jnp.int32, (2 * DIL_QB, width), 0) % DIL_QB
        col = lax.broadcasted_iota(jnp.int32, (2 * DIL_QB, width), 1)
        rel = row - col
        ones = jnp.ones((width, LANES), BF16)
        for case in range(1 if nblk == 1 else 3):
            bias_ref[case, :, 0:width] = jnp.where(jnp.abs(rel + case * DIL_HALF) <= DIL_HALF,
                                                   0.0, NEG_INF)

        def body(it, carry, length=length, width=width, nblk=nblk, ta=ta, tl=tl, tm_=tm_,
                 ones=ones, q_ref=q_ref, k_ref=k_ref, v_ref=v_ref):
            r = it // nblk
            i = it % nblk
            q0 = pl.multiple_of(i * DIL_QB, DIL_QB)
            start = pl.multiple_of(jnp.clip(q0 - DIL_HALF, 0, length - width), DIL_HALF)
            qrow = pl.multiple_of(r * length + q0, DIL_HALF)
            q = q_ref[r, pl.ds(q0, DIL_QB), :]
            kw = k_ref[r, pl.ds(start, width), :]
            vw = jnp.concatenate([v_ref[r, pl.ds(start, width), :], ones], axis=1)
            zero = jnp.zeros_like(q)
            q2 = jnp.concatenate([jnp.where(lo, q, zero), jnp.where(lo, zero, q)], axis=0)
            sc = lax.dot_general(q2, kw, (((1,), (1,)), ((), ())), preferred_element_type=F32)
            sc = sc + bias_ref[lax.shift_right_logical(q0 - start, 6), :, 0:width]
            m = jnp.max(sc, axis=-1, keepdims=True)
            p = jnp.exp2(sc - m).astype(BF16)
            o2 = _dot(p, vw)
            ta[pl.ds(qrow, DIL_QB), :] = jnp.where(lo, o2[:DIL_QB, 0:LANES], o2[DIL_QB:, 0:LANES])
            tl[pl.ds(qrow, DIL_QB), :] = jnp.where(lo, o2[:DIL_QB, LANES:], o2[DIL_QB:, LANES:])
            tm_[pl.ds(qrow, DIL_QB), :] = jnp.where(lo, m[:DIL_QB], m[DIL_QB:])
            return carry

        lax.fori_loop(0, dil * nblk, body, 0, unroll=8)
        if gi > 0:
            pa, ps, pm = sets[(gi - 1) % 2]
            step, prev_len = dil // prev_dil, seq // prev_dil
            for r in range(dil):
                rows = pl.ds((r % prev_dil) * prev_len + r // prev_dil, length, stride=step)
                blk = slice(r * length, (r + 1) * length)
                m_old, m_blk = pm[rows, :], tm_[blk, :]
                m_new = jnp.maximum(m_old, m_blk)
                w_old, w_blk = jnp.exp2(m_old - m_new), jnp.exp2(m_blk - m_new)
                ta[blk, :] = pa[rows, :] * w_old + ta[blk, :] * w_blk
                tl[blk, :] = ps[rows, :] * w_old + tl[blk, :] * w_blk
                tm_[blk, :] = m_new
        prev_dil = dil
    fa, fs, _ = sets[(len(groups) - 1) % 2]
    stage = sets[len(groups) % 2][2]
    length = seq // prev_dil
    for r in range(prev_dil):
        blk = slice(r * length, (r + 1) * length)
        stage[pl.ds(r, length, stride=prev_dil), :] = fa[blk, :] / fs[blk, :]
    o_ref[...] = stage[...].astype(BF16)


def _dil_attn(qkvs, b, s):
    in_specs, args = [], []
    for x in qkvs:
        for j in range(3):
            in_specs.append(pl.BlockSpec((None,) + x.shape[1:3] + (LANES,),
                                         lambda bi, hp, j=j: (bi, 0, 0, 4 * j + hp)))
            args.append(x)
    stat = pltpu.VMEM((s, LANES), F32)
    return pl.pallas_call(
        _dil_attn_kernel, grid=(b, 4), in_specs=in_specs,
        out_specs=pl.BlockSpec((s, LANES), lambda bi, hp: (bi, hp)),
        out_shape=jax.ShapeDtypeStruct((b * s, 512), BF16),
        scratch_shapes=[stat] * 6 + [pltpu.VMEM((3, 2 * DIL_QB, DIL_QB + 2 * DIL_HALF), F32)],
        compiler_params=_params(("parallel", "parallel"), 40), name="dil_attn",
    )(*args)


def _l1_router_kernel(o_ref, wo_ref, gmix_ref, h_ref, gpre_ref, wr_ref,
                      h1_ref, a_ref, idx_ref, gate_ref, cnt_ref, carry_ref):
    @pl.when(pl.program_id(0) == 0)
    def _():
        carry_ref[...] = jnp.zeros_like(carry_ref)

    h1 = h_ref[...] + _rms(_dot(o_ref[...], wo_ref[...]), gmix_ref[...])
    h1_ref[...] = h1
    a = _rms(h1, gpre_ref[...]).astype(BF16)
    a_ref[...] = a
    logits = _dot(a, wr_ref[...])
    tm = logits.shape[0]
    lane = _lane_iota(logits.shape)
    lanef = lane.astype(F32)
    lg = jnp.where(lane < N_EXPERTS, logits, -jnp.inf)
    m1 = jnp.max(lg, axis=-1, keepdims=True)
    i1 = jnp.min(jnp.where(lg == m1, lanef, float(LANES)), axis=-1, keepdims=True)
    lg2 = jnp.where(lanef == i1, -jnp.inf, lg)
    m2 = jnp.max(lg2, axis=-1, keepdims=True)
    i2 = jnp.min(jnp.where(lg2 == m2, lanef, float(LANES)), axis=-1, keepdims=True)
    t = jnp.exp(m2 - m1)
    g1 = 1.0 / (1.0 + t)
    g2 = t / (1.0 + t)
    hit1 = lanef == i1
    hit2 = lanef == i2
    onehot = jnp.where(hit1 | hit2, 1.0, 0.0)
    r = lax.broadcasted_iota(jnp.int32, (tm, tm), 0)
    c = lax.broadcasted_iota(jnp.int32, (tm, tm), 1)
    tri = jnp.where(c < r, 1.0, 0.0).astype(BF16)
    before = _dot(tri, onehot.astype(BF16)) + carry_ref[...]
    rank1 = jnp.sum(jnp.where(hit1, before, 0.0), axis=-1, keepdims=True)
    rank2 = jnp.sum(jnp.where(hit2, before, 0.0), axis=-1, keepdims=True)
    info = jnp.where(lane == 0, i1, jnp.where(lane == 1, i2, jnp.where(lane == 2, rank1, rank2)))
    idx_ref[...] = info.astype(jnp.int32)
    gate_ref[...] = jnp.where(lane == 0, g1, g2)
    carry_ref[...] += jnp.sum(onehot, axis=0, keepdims=True)
    cnt_ref[...] = carry_ref[...]


def _l1_router(o, wo, gmix, h, gpre, wr):
    m = h.shape[0]
    row = lambda i: (i, 0)
    const = lambda i: (0, 0)
    full = lambda x: pl.BlockSpec(x.shape, const)
    return pl.pallas_call(
        _l1_router_kernel, grid=(m // TM,),
        in_specs=[pl.BlockSpec((TM, o.shape[1]), row), full(wo), full(gmix),
                  pl.BlockSpec((TM, D_MODEL), row), full(gpre), full(wr)],
        out_specs=[pl.BlockSpec((TM, D_MODEL), row), pl.BlockSpec((TM, D_MODEL), row),
                   pl.BlockSpec((TM, LANES), row), pl.BlockSpec((TM, LANES), row),
                   pl.BlockSpec((1, LANES), const)],
        out_shape=[jax.ShapeDtypeStruct((m, D_MODEL), F32),
                   jax.ShapeDtypeStruct((m, D_MODEL), BF16),
                   jax.ShapeDtypeStruct((m, LANES), jnp.int32),
                   jax.ShapeDtypeStruct((m, LANES), F32),
                   jax.ShapeDtypeStruct((1, LANES), F32)],
        scratch_shapes=[pltpu.VMEM((1, LANES), F32)],
        compiler_params=_params(("arbitrary",), 32), name="l1_router",
    )(o, wo, gmix, h, gpre, wr)


def _expert_kernel(te_ref, tv_ref, x_ref, wg_ref, wu_ref, wd_ref, *refs):
    y_ref, acc_ref = refs[-2:]
    j = pl.program_id(1)
    t = pl.program_id(2)
    last = pl.num_programs(1) - 1
    valid = tv_ref[pl.program_id(0) * MOE_GROUP + t] > 0

    @pl.when(valid & (j == 0))
    def _():
        acc_ref[t] = jnp.zeros(acc_ref.shape[1:], F32)

    @pl.when(valid)
    def _():
        x = x_ref[...]
        gate = _dot(x, wg_ref[...].astype(BF16))
        up = _dot(x, wu_ref[...].astype(BF16))
        acc_ref[t] += _dot((_silu(gate) * up).astype(BF16), wd_ref[...].astype(BF16))

    @pl.when(valid & (j == last))
    def _():
        y_ref[...] = acc_ref[t].astype(BF16)

    @pl.when(jnp.logical_not(valid) & (j == last))
    def _():
        y_ref[...] = jnp.zeros_like(y_ref)


def _experts(tile_e, tile_valid, xs, wg, wu, wd, ys, first_tile, n_slots):
    n_tiles = xs.shape[0] // TM_MOE
    n_j = wg.shape[2] // TF
    te = tile_e[first_tile:first_tile + n_tiles]
    tv = tile_valid[first_tile:first_tile + n_tiles]
    tile = lambda g, t: g * MOE_GROUP + t
    in_specs = [
        pl.BlockSpec((TM_MOE, D_MODEL), lambda g, j, t, te, tv: (tile(g, t), 0)),
        pl.BlockSpec((None, D_MODEL, TF),
                     lambda g, j, t, te, tv: (te[tile(g, t)], 0, j * tv[tile(g, t)])),
        pl.BlockSpec((None, D_MODEL, TF),
                     lambda g, j, t, te, tv: (te[tile(g, t)], 0, j * tv[tile(g, t)])),
        pl.BlockSpec((None, TF, D_MODEL),
                     lambda g, j, t, te, tv: (te[tile(g, t)], j * tv[tile(g, t)], 0))]
    args = [te, tv, xs, wg, wu, wd]
    aliases = {}
    if ys is not None:
        in_specs.append(pl.BlockSpec(memory_space=pl.ANY))
        aliases = {len(args): 0}
        args.append(ys)
    out_spec = pl.BlockSpec(
        (TM_MOE, D_MODEL),
        lambda g, j, t, te, tv: (first_tile + g * MOE_GROUP + jnp.where(j == n_j - 1, t, 0), 0))
    grid_spec = pltpu.PrefetchScalarGridSpec(
        num_scalar_prefetch=2, grid=(n_tiles // MOE_GROUP, n_j, MOE_GROUP), in_specs=in_specs,
        out_specs=out_spec,
        scratch_shapes=[pltpu.VMEM((MOE_GROUP, TM_MOE, D_MODEL), F32)])
    return pl.pallas_call(
        _expert_kernel, grid_spec=grid_spec,
        out_shape=jax.ShapeDtypeStruct((n_slots, D_MODEL), BF16),
        input_output_aliases=aliases,
        compiler_params=_params(("parallel", "arbitrary", "arbitrary"), 56), name="moe_experts",
    )(*args)


def _combine_ple_kernel(y0_ref, y1_ref, gate_ref, g_ref, h_ref, p_ref, ggn_ref, wpg_ref, wpp_ref,
                        gple_ref, out_ref):
    gate = gate_ref[...]
    f = y0_ref[...].astype(F32) * gate[:, 0:1] + y1_ref[...].astype(F32) * gate[:, 1:2]
    h2 = h_ref[...] + _rms(f, g_ref[...])
    out_ref[...] = _ple_update(h2, p_ref[...], ggn_ref[...], wpg_ref[...], wpp_ref[...], gple_ref[...])


def _combine_ple(y01, gate, g, h, p, ggn, wpg, wpp, gple):
    m = h.shape[0]
    nt = m // TM
    row = lambda i: (i, 0)
    const = lambda i: (0, 0)
    full = lambda x: pl.BlockSpec(x.shape, const)
    act = pl.BlockSpec((TM, D_MODEL), row)
    return pl.pallas_call(
        _combine_ple_kernel, grid=(nt,),
        in_specs=[act, pl.BlockSpec((TM, D_MODEL), lambda i: (nt + i, 0)),
                  pl.BlockSpec((TM, LANES), row), full(g), act,
                  pl.BlockSpec((None, TM, PLE_DIM), lambda i: (1, i, 0)), full(ggn), full(wpg),
                  full(wpp), full(gple)],
        out_specs=act,
        out_shape=jax.ShapeDtypeStruct((m, D_MODEL), F32),
        compiler_params=_params(("parallel",), 40), name="combine_ple",
    )(y01, y01, gate, g, h, p, ggn, wpg, wpp, gple)


def _rows(x, idx):
    return x.at[idx].get(mode='promise_in_bounds')


def _moe_plan(idx, cnt, n):
    counts = cnt[0, :N_EXPERTS].astype(jnp.int32)
    padded = (counts + TM_MOE - 1) // TM_MOE * TM_MOE
    end_pad = jnp.cumsum(padded)
    start_pad = end_pad - padded
    experts = jnp.arange(N_EXPERTS, dtype=jnp.int32)
    first = lambda e: jnp.sum(jnp.where(e[:, None] == experts, start_pad, 0), axis=-1)
    dest = (first(idx[:, 0]) + idx[:, 2], first(idx[:, 1]) + idx[:, 3])
    n_slots = n * TOP_K + N_EXPERTS * TM_MOE
    pair = 2 * jnp.arange(n, dtype=jnp.int32)
    keys = jnp.concatenate([idx[:, 0] * (2 * n) + pair, idx[:, 1] * (2 * n) + pair + 1])
    by_expert = jnp.concatenate([(jnp.sort(keys) % (2 * n)) // 2, jnp.zeros((n,), jnp.int32)])
    start_run = jnp.cumsum(counts) - counts
    slot_tok = jnp.zeros((n_slots + n,), jnp.int32)
    for e in range(N_EXPERTS):
        run = lax.dynamic_slice(by_expert, (start_run[e],), (n,))
        slot_tok = lax.dynamic_update_slice(slot_tok, run, (start_pad[e],))
    slot_tok = slot_tok[:n_slots]
    tile_start = jnp.arange(n_slots // TM_MOE, dtype=jnp.int32) * TM_MOE
    tile_e = jnp.minimum(jnp.sum(tile_start[:, None] >= end_pad[None, :], axis=1),
                         N_EXPERTS - 1).astype(jnp.int32)
    tile_valid = (tile_start < end_pad[-1]).astype(jnp.int32)
    return dest, slot_tok, tile_e, tile_valid


def _rotary_tables(pos, dim, theta):
    exponent = jnp.arange(0, dim, 2, dtype=F32) / dim
    inv_freq = jnp.power(jnp.float32(theta), -exponent)
    ang = pos.astype(F32)[:, None] * inv_freq[None, :]
    return jnp.cos(ang), jnp.sin(ang)


def _tables(s):
    t = jnp.arange(s, dtype=jnp.int32)
    cr, sr = _rotary_tables(t // GRID_W, HEAD_DIM // 2, AXIAL_THETA)
    cc, sc = _rotary_tables(t % GRID_W, HEAD_DIM // 2, AXIAL_THETA)
    ca = jnp.tile(jnp.concatenate([cr, cr, cc, cc], axis=1), (1, 2))
    sa = jnp.tile(jnp.concatenate([-sr, sr, -sc, sc], axis=1), (1, 2))
    cm_, sm_ = _rotary_tables(t, MLA_ROPE_DIM, MLA_ROPE_THETA)
    one = jnp.ones((s, 1), F32)
    zero = jnp.zeros((s, 1), F32)
    cm = jnp.concatenate([jnp.tile(one, (1, 64)), cm_, cm_, jnp.tile(one, (1, 32))], axis=1)
    sm = jnp.concatenate([jnp.tile(zero, (1, 64)), -sm_, sm_, jnp.tile(zero, (1, 32))], axis=1)
    ckr = jnp.concatenate([cm_, cm_, jnp.tile(zero, (1, 96))], axis=1)
    skr = jnp.concatenate([-sm_, sm_, jnp.tile(zero, (1, 96))], axis=1)
    cp_, sp_ = _rotary_tables(t, PARTIAL_ROPE_DIM, ROPE_THETA)
    cp = jnp.tile(jnp.concatenate([cp_, cp_, jnp.tile(one, (1, 48))], axis=1), (1, 2))
    sp = jnp.tile(jnp.concatenate([-sp_, sp_, jnp.tile(zero, (1, 48))], axis=1), (1, 2))
    scale = HEAD_DIM ** -0.5 * LOG2_E
    ctab = jnp.stack([cp * scale, cp])
    stab = jnp.stack([sp * scale, sp])
    return ca, sa, cm, sm, ckr, skr, ctab, stab


_GQA_PERM = (0, 4, 1, 5, 2, 6, 3, 7)


def _l0_weights(w_in, w_uq, w_ukv, w_out):
    d = w_in.shape[0]
    q = w_in[:, :512].reshape(d, 8, 64)[:, _GQA_PERM, :].reshape(d, 512)
    win = jnp.concatenate([q, w_in[:, 512:], jnp.zeros((d, AB_IN_PAD - w_in.shape[1]), w_in.dtype)],
                          axis=1).astype(BF16)
    uq = w_uq.reshape(MLA_Q_RANK, MLA_HEADS, MLA_NOPE_DIM + MLA_ROPE_DIM)
    wuq = jnp.pad(uq, ((0, 0), (0, 0), (0, LANES - uq.shape[2]))).reshape(MLA_Q_RANK, -1).astype(BF16)
    ukv = w_ukv.reshape(MLA_KV_RANK, MLA_HEADS, MLA_NOPE_DIM + MLA_V_DIM)
    wk = jnp.pad(ukv[:, :, :MLA_NOPE_DIM], ((0, 0), (0, 0), (0, LANES - MLA_NOPE_DIM)))
    wk = wk.reshape(MLA_KV_RANK, -1).astype(BF16)
    wv = ukv[:, :, MLA_NOPE_DIM:].reshape(MLA_KV_RANK, -1).astype(BF16)
    src = jnp.arange(LANES)[:, None]
    dst = jnp.arange(MLA_HEADS * LANES)[None, :]
    wp = ((src < MLA_ROPE_DIM) & (dst % LANES == src + MLA_NOPE_DIM)).astype(BF16)
    oa = w_out[:512].reshape(8, 64, -1)[_GQA_PERM, :, :].reshape(512, -1)
    wout = jnp.concatenate([oa, w_out[512:]], axis=0).astype(BF16)
    return win, wuq, wk, wp, wv, wout


def kernel(x, p, mix_pre_g, mix_post_g, ffn_pre_g, ffn_post_g, ple_w_proj, ple_gate_norm_g, ple_w_gate, ple_post_g, ab_w_in, gqa_q_norm_g, gqa_k_norm_g, mla_q_norm_g, mla_w_uq, mla_kv_norm_g, mla_w_ukv, ab_w_out, ffn_w_gate, ffn_w_up, ffn_w_down, dil_w_qkv, dil_w_out, moe_w_router, moe_w_gate, moe_w_up, moe_w_down):
    b, s, d = x.shape
    m = b * s
    row = lambda g: g.reshape(1, -1).astype(F32)
    ca, sa, cm, sm, ckr, skr, ctab, stab = _tables(s)
    h = x.reshape(m, d)
    pf = p.reshape(p.shape[0], m, PLE_DIM)

    win, wuq, wk, wp, wv, wout = _l0_weights(ab_w_in[0], mla_w_uq[0], mla_w_ukv[0], ab_w_out[0])
    gq = jnp.tile(row(gqa_q_norm_g[0]), (1, 2))
    gk = jnp.tile(row(gqa_k_norm_g[0]), (1, 2))
    qa, ka, vat, qm, km, vmt = _l0_in(h, row(mix_pre_g[0]), win, gq, gk, ca, sa,
                                      row(mla_q_norm_g[0]), wuq, cm, sm,
                                      row(mla_kv_norm_g[0]), wk, wp, wv, ckr, skr, b, s)
    oa = _gqa_attn(qa, ka, vat, b, s)
    om = _mla_attn(qm, km, vmt, b, s)
    h = _l0_tail(oa, om, wout, row(mix_post_g[0]), h, row(ffn_pre_g[0]),
                 ffn_w_gate[0].astype(BF16), ffn_w_up[0].astype(BF16), ffn_w_down[0].astype(BF16),
                 row(ffn_post_g[0]), pf, row(ple_gate_norm_g[0]), ple_w_gate[0].astype(BF16),
                 ple_w_proj[0].astype(BF16), row(ple_post_g[0]))

    qkvs = _l1_qkv(h, row(mix_pre_g[1]), dil_w_qkv[0].astype(BF16), ctab, stab, b, s)
    o = _dil_attn(qkvs, b, s)
    wr = jnp.pad(moe_w_router[0].astype(BF16), ((0, 0), (0, LANES - N_EXPERTS)))
    h, a, idx, gate, cnt = _l1_router(o, dil_w_out[0].astype(BF16), row(mix_post_g[1]), h,
                                      row(ffn_pre_g[1]), wr)
    dest, slot_tok, tile_e, tile_valid = _moe_plan(idx, cnt, m)
    n_slots = slot_tok.shape[0]
    ys, first_tile = None, 0
    n_groups = n_slots // (TM_MOE * MOE_GROUP)
    sizes = [n_groups * share // sum(MOE_CHUNKS) for share in MOE_CHUNKS]
    sizes[-1] += n_groups - sum(sizes)
    for n_tiles in [size * MOE_GROUP for size in sizes if size > 0]:
        rows = slice(first_tile * TM_MOE, (first_tile + n_tiles) * TM_MOE)
        ys = _experts(tile_e, tile_valid, _rows(a, slot_tok[rows]), moe_w_gate[0], moe_w_up[0],
                      moe_w_down[0], ys, first_tile, n_slots)
        first_tile += n_tiles
    h = _combine_ple(_rows(ys, jnp.concatenate(dest)), gate, row(ffn_post_g[1]), h, pf,
                     row(ple_gate_norm_g[1]), ple_w_gate[1].astype(BF16), ple_w_proj[1].astype(BF16),
                     row(ple_post_g[1]))
    return h.reshape(b, s, d)
```

```python
import functools

import jax
import jax.numpy as jnp
from jax import lax
from jax.experimental import pallas as pl
from jax.experimental.pallas import tpu as pltpu

F32 = jnp.float32
BF16 = jnp.bfloat16

D_MODEL = 1024
GRID_W = 64
HEAD_DIM = 64
NORM_EPS = 1e-6
NEG_INF = -1e30
ROPE_THETA = 500000.0
PARTIAL_ROPE_DIM = HEAD_DIM // 4
GQA_Q_HEADS = 8
GQA_KV_HEADS = 2
AXIAL_THETA = 10000.0
MLA_HEADS = 8
MLA_Q_RANK = 256
MLA_KV_RANK = 128
MLA_NOPE_DIM = 64
MLA_ROPE_DIM = 32
MLA_V_DIM = 64
MLA_ROPE_THETA = 10000.0
DIL_CONFIGS = ((128, 1), (512, 4), (2048, 16))
DIL_GROUPS = len(DIL_CONFIGS)
DIL_HEADS = 8
DIL_HALF = 64
DIL_QB = 128
LOG2_E = 1.4426950408889634
DIL_IN_WIDTH = DIL_GROUPS * 3 * DIL_HEADS * HEAD_DIM
D_FF = 3584
N_EXPERTS = 8
TOP_K = 2
PLE_DIM = 256

LANES = 128
VMEM_BYTES = 64 * 1024 * 1024
MIB = 1024 * 1024

TM = 512
TQ = 1024
KEY_CHUNK = 256
PV_LAG = 2
V_ONES = 16
V_ROWS = LANES + V_ONES
TF = 512
TM_FFN = 1024
TAIL_ROWS = 256
TM_MOE = 1024
AB_IN_PAD = 1280
MOE_GROUP = 4
MOE_CHUNKS = (1, 2, 3, 3)


def _params(semantics, vmem_mib):
    return pltpu.CompilerParams(dimension_semantics=semantics,
                                vmem_limit_bytes=vmem_mib * MIB)


def _dot(a, b):
    return jnp.dot(a, b, preferred_element_type=F32)


def _rms(x, g):
    return x * lax.rsqrt(jnp.mean(x * x, axis=-1, keepdims=True) + NORM_EPS) * g


def _lane_iota(shape):
    return lax.broadcasted_iota(jnp.int32, shape, len(shape) - 1)


def _swap_halves(x, k):
    w = x.shape[-1]
    fwd = pltpu.roll(x, w - k, 1)
    bwd = pltpu.roll(x, k, 1)
    return jnp.where((_lane_iota(x.shape) % (2 * k)) < k, fwd, bwd)


def _rotate(x, c, s, k):
    return x * c + _swap_halves(x, k) * s


def _head_rms(x, g):
    lo = _lane_iota(x.shape) < HEAD_DIM
    x2 = x * x
    s_all = jnp.sum(x2, axis=-1, keepdims=True)
    s_lo = jnp.sum(jnp.where(lo, x2, 0.0), axis=-1, keepdims=True)
    ms = jnp.where(lo, s_lo, s_all - s_lo) * (1.0 / HEAD_DIM)
    return x * lax.rsqrt(ms + NORM_EPS) * g


def _l0_in_kernel(h_ref, gpre_ref, win_ref, gq_ref, gk_ref, ca_ref, sa_ref,
                  gmq_ref, wuq_ref, cm_ref, sm_ref, gmkv_ref, wk_ref, wp_ref, wv_ref,
                  ckr_ref, skr_ref,
                  qa_ref, ka_ref, vat_ref, qm_ref, km_ref, vmt_ref):
    a = _rms(h_ref[...], gpre_ref[...]).astype(BF16)
    z = _dot(a, win_ref[...])
    ca, sa = ca_ref[...], sa_ref[...]
    for j in range(4):
        xq = _head_rms(z[:, j * LANES:(j + 1) * LANES], gq_ref[...])
        qa_ref[:, j * LANES:(j + 1) * LANES] = (
            _rotate(xq, ca, sa, 16) * (HEAD_DIM ** -0.5 * LOG2_E)).astype(BF16)
    ka_ref[...] = _rotate(_head_rms(z[:, 512:640], gk_ref[...]), ca, sa, 16).astype(BF16)
    ones = jnp.ones((V_ONES, z.shape[0]), BF16)
    vat_ref[0:LANES, :] = z[:, 640:768].T.astype(BF16)
    vat_ref[LANES:, :] = ones
    cq = _rms(z[:, 768:1024], gmq_ref[...]).astype(BF16)
    qm = _dot(cq, wuq_ref[...])
    cm, sm = cm_ref[...], sm_ref[...]
    scale_m = (MLA_NOPE_DIM + MLA_ROPE_DIM) ** -0.5 * LOG2_E
    for hb in range(MLA_HEADS):
        x = qm[:, hb * LANES:(hb + 1) * LANES]
        qm_ref[:, hb * LANES:(hb + 1) * LANES] = (_rotate(x, cm, sm, 16) * scale_m).astype(BF16)
    ckv = _rms(z[:, 1024:1152], gmkv_ref[...]).astype(BF16)
    kr = _rotate(z[:, 1152:1280], ckr_ref[...], skr_ref[...], 16).astype(BF16)
    km = _dot(ckv, wk_ref[...]) + _dot(kr, wp_ref[...])
    km_ref[...] = km.astype(BF16)
    vm = _dot(ckv, wv_ref[...])
    for j in range(4):
        vmt_ref[j * V_ROWS:j * V_ROWS + LANES, :] = vm[:, j * LANES:(j + 1) * LANES].T.astype(BF16)
        vmt_ref[j * V_ROWS + LANES:(j + 1) * V_ROWS, :] = ones


def _l0_in(h, gpre, win, gq, gk, ca, sa, gmq, wuq, cm, sm, gmkv, wk, wp, wv, ckr, skr, b, s):
    m = h.shape[0]
    nt = s // TM
    row = lambda i: (i, 0)
    const = lambda i: (0, 0)
    tab = lambda i: (i % nt, 0)
    full = lambda a: pl.BlockSpec(a.shape, const)
    in_specs = [pl.BlockSpec((TM, D_MODEL), row), full(gpre), full(win), full(gq), full(gk),
                pl.BlockSpec((TM, LANES), tab), pl.BlockSpec((TM, LANES), tab),
                full(gmq), full(wuq), pl.BlockSpec((TM, LANES), tab), pl.BlockSpec((TM, LANES), tab),
                full(gmkv), full(wk), full(wp), full(wv),
                pl.BlockSpec((TM, LANES), tab), pl.BlockSpec((TM, LANES), tab)]
    tr = lambda i: (i // nt, 0, i % nt)
    out_specs = [pl.BlockSpec((TM, 512), row),
                 pl.BlockSpec((TM, LANES), row),
                 pl.BlockSpec((None, V_ROWS, TM), tr),
                 pl.BlockSpec((TM, 1024), row),
                 pl.BlockSpec((TM, 1024), row),
                 pl.BlockSpec((None, 4 * V_ROWS, TM), tr)]
    out_shape = [jax.ShapeDtypeStruct((m, 512), BF16),
                 jax.ShapeDtypeStruct((m, LANES), BF16),
                 jax.ShapeDtypeStruct((b, V_ROWS, s), BF16),
                 jax.ShapeDtypeStruct((m, 1024), BF16),
                 jax.ShapeDtypeStruct((m, 1024), BF16),
                 jax.ShapeDtypeStruct((b, 4 * V_ROWS, s), BF16)]
    return pl.pallas_call(
        _l0_in_kernel, grid=(m // TM,), in_specs=in_specs, out_specs=out_specs,
        out_shape=out_shape, compiler_params=_params(("parallel",), 48), name="l0_in",
    )(h, gpre, win, gq, gk, ca, sa, gmq, wuq, cm, sm, gmkv, wk, wp, wv, ckr, skr)


def _attend_pair(score_t, vt_ref, o_ref):
    n_chunks = vt_ref.shape[1] // KEY_CHUNK
    tq = o_ref.shape[0]
    keys = [slice(c * KEY_CHUNK, (c + 1) * KEY_CHUNK) for c in range(n_chunks)]
    parts = []
    s_next = score_t(keys[0])
    pending = []
    for c in range(n_chunks):
        s_t = s_next
        if c + 1 < n_chunks:
            s_next = score_t(keys[c + 1])
        m = jnp.max(s_t, axis=0, keepdims=True)
        pending.append((c, m, jnp.exp2(s_t - m).astype(BF16)))
        if len(pending) > PV_LAG:
            c0, m0, p0 = pending.pop(0)
            parts.append((m0, _dot(vt_ref[:, keys[c0]], p0)))
    for c0, m0, p0 in pending:
        parts.append((m0, _dot(vt_ref[:, keys[c0]], p0)))
    m_all = functools.reduce(jnp.maximum, [m for m, _ in parts])
    o_t = sum(o_c * jnp.exp2(m_c - m_all) for m_c, o_c in parts)
    o_t = o_t[0:LANES, :] / o_t[LANES:LANES + 1, :]
    first = lax.broadcasted_iota(jnp.int32, (LANES, tq), 0) < HEAD_DIM
    o_ref[...] = jnp.where(first, o_t[:, 0:tq], o_t[:, tq:]).T.astype(BF16)


def _transposed(q):
    return q.astype(F32).T.astype(BF16)


def _gqa_attn_kernel(q_ref, k_ref, vt_ref, o_ref):
    qt = _transposed(q_ref[...])
    first = lax.broadcasted_iota(jnp.int32, qt.shape, 0) < HEAD_DIM
    zero = jnp.zeros_like(qt)
    w = jnp.concatenate([jnp.where(first, qt, zero), jnp.where(first, zero, qt)], axis=1)
    _attend_pair(lambda keys: _dot(k_ref[keys, :], w), vt_ref, o_ref)


def _gqa_attn(qa, ka, vat, b, s):
    m = qa.shape[0]
    nq = s // TQ
    return pl.pallas_call(
        _gqa_attn_kernel, grid=(b, nq, 4),
        in_specs=[pl.BlockSpec((TQ, LANES), lambda bi, i, j: (bi * nq + i, j)),
                  pl.BlockSpec((s, LANES), lambda bi, i, j: (bi, 0)),
                  pl.BlockSpec((None, V_ROWS, s), lambda bi, i, j: (bi, 0, 0))],
        out_specs=pl.BlockSpec((TQ, LANES), lambda bi, i, j: (bi * nq + i, j)),
        out_shape=jax.ShapeDtypeStruct((m, 512), BF16),
        compiler_params=_params(("parallel", "parallel", "parallel"), 32), name="gqa_attn",
    )(qa, ka, vat)


def _mla_attn_kernel(q_ref, k_ref, vt_ref, o_ref):
    qt0 = _transposed(q_ref[:, 0:LANES])
    qt1 = _transposed(q_ref[:, LANES:2 * LANES])

    def score_t(keys):
        return jnp.concatenate([_dot(k_ref[keys, 0:LANES], qt0),
                                _dot(k_ref[keys, LANES:2 * LANES], qt1)], axis=1)

    _attend_pair(score_t, vt_ref, o_ref)


def _mla_attn(qm, km, vmt, b, s):
    m = qm.shape[0]
    nq = s // TQ
    return pl.pallas_call(
        _mla_attn_kernel, grid=(b, nq, 4),
        in_specs=[pl.BlockSpec((TQ, 2 * LANES), lambda bi, i, j: (bi * nq + i, j)),
                  pl.BlockSpec((s, 2 * LANES), lambda bi, i, j: (bi, j)),
                  pl.BlockSpec((None, V_ROWS, s), lambda bi, i, j: (bi, j, 0))],
        out_specs=pl.BlockSpec((TQ, LANES), lambda bi, i, j: (bi * nq + i, j)),
        out_shape=jax.ShapeDtypeStruct((m, 512), BF16),
        compiler_params=_params(("parallel", "parallel", "parallel"), 32), name="mla_attn",
    )(qm, km, vmt)


def _silu(x):
    return x * (1.0 / (1.0 + jnp.exp(-x)))


def _ple_update(h, p, ggn, wg, wp, gpost):
    gate = _dot(_rms(h, ggn).astype(BF16), wg)
    gate = 1.0 / (1.0 + jnp.exp(-gate))
    e = _dot(p.astype(BF16), wp) * gate
    return h + _rms(e, gpost)


def _l0_tail_kernel(oa_ref, om_ref, wo_ref, gmix_ref, h_ref, gpre_ref, wg_ref, wu_ref, wd_ref,
                    gpost_ref, p_ref, ggn_ref, wpg_ref, wpp_ref, gple_ref, out_ref, a_ref, acc_ref):
    j = pl.program_id(1)

    chunks = [slice(r, r + TAIL_ROWS) for r in range(0, out_ref.shape[0], TAIL_ROWS)]

    @pl.when(j == 0)
    def _():
        k = oa_ref.shape[1]
        for rows in chunks:
            y = _dot(oa_ref[rows, :], wo_ref[0:k, :]) + _dot(om_ref[rows, :], wo_ref[k:, :])
            h1 = h_ref[rows, :] + _rms(y, gmix_ref[...])
            out_ref[rows, :] = h1
            a_ref[rows, :] = _rms(h1, gpre_ref[...]).astype(BF16)
        acc_ref[...] = jnp.zeros_like(acc_ref)

    a = a_ref[...]
    mid = (_silu(_dot(a, wg_ref[...])) * _dot(a, wu_ref[...])).astype(BF16)
    acc_ref[...] += _dot(mid, wd_ref[...])

    @pl.when(j == pl.num_programs(1) - 1)
    def _():
        for rows in chunks:
            h2 = out_ref[rows, :] + _rms(acc_ref[rows, :], gpost_ref[...])
            out_ref[rows, :] = _ple_update(h2, p_ref[rows, :], ggn_ref[...], wpg_ref[...],
                                           wpp_ref[...], gple_ref[...])


def _l0_tail(oa, om, wo, gmix, h, gpre, wg, wu, wd, gpost, p, ggn, wpg, wpp, gple):
    m = h.shape[0]
    dff = wg.shape[1]
    row = lambda i, j: (i, 0)
    const = lambda i, j: (0, 0)
    vec = lambda g: pl.BlockSpec(g.shape, const)
    once = lambda w: pl.BlockSpec(w.shape, const, pipeline_mode=pl.Buffered(1))
    return pl.pallas_call(
        _l0_tail_kernel, grid=(m // TM_FFN, dff // TF),
        in_specs=[pl.BlockSpec((TM_FFN, oa.shape[1]), row), pl.BlockSpec((TM_FFN, om.shape[1]), row),
                  once(wo), vec(gmix), pl.BlockSpec((TM_FFN, D_MODEL), row), vec(gpre),
                  pl.BlockSpec((D_MODEL, TF), lambda i, j: (0, j)),
                  pl.BlockSpec((D_MODEL, TF), lambda i, j: (0, j)),
                  pl.BlockSpec((TF, D_MODEL), lambda i, j: (j, 0)),
                  vec(gpost), pl.BlockSpec((None, TM_FFN, PLE_DIM), lambda i, j: (0, i, 0)), vec(ggn), once(wpg), once(wpp),
                  vec(gple)],
        out_specs=pl.BlockSpec((TM_FFN, D_MODEL), row),
        out_shape=jax.ShapeDtypeStruct((m, D_MODEL), F32),
        scratch_shapes=[pltpu.VMEM((TM_FFN, D_MODEL), BF16), pltpu.VMEM((TM_FFN, D_MODEL), F32)],
        compiler_params=_params(("parallel", "arbitrary"), 56), name="l0_tail",
    )(oa, om, wo, gmix, h, gpre, wg, wu, wd, gpost, p, ggn, wpg, wpp, gple)


def _l1_qkv_kernel(h_ref, gpre_ref, w_ref, c_ref, s_ref, out0_ref, out1_ref, out2_ref, stage_ref):
    a = _rms(h_ref[...], gpre_ref[...]).astype(BF16)
    tm = a.shape[0]
    outs = (out0_ref, out1_ref, out2_ref)
    for blk in range(DIL_IN_WIDTH // 512):
        gi, kind = divmod(blk, 3)
        dil = DIL_CONFIGS[gi][1]
        z = _dot(a, w_ref[:, blk * 512:(blk + 1) * 512])
        if kind < 2:
            c, s = c_ref[kind], s_ref[kind]
            z = jnp.concatenate(
                [_rotate(z[:, sub * LANES:(sub + 1) * LANES], c, s, PARTIAL_ROPE_DIM // 2)
                 for sub in range(4)], axis=1)
        cols = slice(kind * 512, (kind + 1) * 512)
        if dil == 1:
            outs[gi][0, :, cols] = z.astype(BF16)
            continue
        for sub in range(4):
            stage = stage_ref.at[(blk % 2) * 4 + sub]
            stage[...] = z[:, sub * LANES:(sub + 1) * LANES]
            lanes = slice(kind * 512 + sub * LANES, kind * 512 + (sub + 1) * LANES)
            for r in range(dil):
                outs[gi][r, :, lanes] = stage[pl.ds(r, tm // dil, stride=dil), :].astype(BF16)


def _l1_qkv(h, gpre, w, ctab, stab, b, s):
    m = h.shape[0]
    nt = s // TM
    const = lambda i: (0, 0)
    out_specs, out_shape = [], []
    for _, dil in DIL_CONFIGS:
        out_specs.append(pl.BlockSpec((None, dil, TM // dil, 1536), lambda i: (i // nt, 0, i % nt, 0)))
        out_shape.append(jax.ShapeDtypeStruct((b, dil, s // dil, 1536), BF16))
    return pl.pallas_call(
        _l1_qkv_kernel, grid=(m // TM,),
        in_specs=[pl.BlockSpec((TM, D_MODEL), lambda i: (i, 0)),
                  pl.BlockSpec(gpre.shape, const),
                  pl.BlockSpec(w.shape, const, pipeline_mode=pl.Buffered(1)),
                  pl.BlockSpec((2, TM, LANES), lambda i: (0, i % nt, 0)),
                  pl.BlockSpec((2, TM, LANES), lambda i: (0, i % nt, 0))],
        out_specs=out_specs, out_shape=out_shape,
        scratch_shapes=[pltpu.VMEM((8, TM, LANES), F32)],
        compiler_params=_params(("parallel",), 48), name="l1_qkv",
    )(h, gpre, w, ctab, stab)


def _dil_attn_kernel(q0_ref, k0_ref, v0_ref, q1_ref, k1_ref, v1_ref, q2_ref, k2_ref, v2_ref, o_ref,
                     ra_ref, rl_ref, rm_ref, acc_ref, sum_ref, max_ref, bias_ref):
    seq = o_ref.shape[0]
    lo = _lane_iota((DIL_QB, LANES)) < HEAD_DIM
    groups = ((q0_ref, k0_ref, v0_ref), (q1_ref, k1_ref, v1_ref), (q2_ref, k2_ref, v2_ref))
    sets = ((acc_ref, sum_ref, max_ref), (ra_ref, rl_ref, rm_ref))
    prev_dil = None
    for gi, (q_ref, k_ref, v_ref) in enumerate(groups):
        dil, length = q_ref.shape[0], q_ref.shape[1]
        width = min(DIL_QB + 2 * DIL_HALF, length)
        nblk = length // DIL_QB
        ta, tl, tm_ = sets[gi % 2]
        row = lax.broadcasted_iota(jnp.int32, (2 * DIL_QB, width), 0) % DIL_QB
        col = lax.broadcasted_iota(jnp.int32, (2 * DIL_QB, width), 1)
        rel = row - col
        ones = jnp.ones((width, LANES), BF16)
        for case in range(1 if nblk == 1 else 3):
            bias_ref[case, :, 0:width] = jnp.where(jnp.abs(rel + case * DIL_HALF) <= DIL_HALF,
                                                   0.0, NEG_INF)

        def body(it, carry, length=length, width=width, nblk=nblk, ta=ta, tl=tl, tm_=tm_,
                 ones=ones, q_ref=q_ref, k_ref=k_ref, v_ref=v_ref):
            r = it // nblk
            i = it % nblk
            q0 = pl.multiple_of(i * DIL_QB, DIL_QB)
            start = pl.multiple_of(jnp.clip(q0 - DIL_HALF, 0, length - width), DIL_HALF)
            qrow = pl.multiple_of(r * length + q0, DIL_HALF)
            q = q_ref[r, pl.ds(q0, DIL_QB), :]
            kw = k_ref[r, pl.ds(start, width), :]
            vw = jnp.concatenate([v_ref[r, pl.ds(start, width), :], ones], axis=1)
            zero = jnp.zeros_like(q)
            q2 = jnp.concatenate([jnp.where(lo, q, zero), jnp.where(lo, zero, q)], axis=0)
            sc = lax.dot_general(q2, kw, (((1,), (1,)), ((), ())), preferred_element_type=F32)
            sc = sc + bias_ref[lax.shift_right_logical(q0 - start, 6), :, 0:width]
            m = jnp.max(sc, axis=-1, keepdims=True)
            p = jnp.exp2(sc - m).astype(BF16)
            o2 = _dot(p, vw)
            ta[pl.ds(qrow, DIL_QB), :] = jnp.where(lo, o2[:DIL_QB, 0:LANES], o2[DIL_QB:, 0:LANES])
            tl[pl.ds(qrow, DIL_QB), :] = jnp.where(lo, o2[:DIL_QB, LANES:], o2[DIL_QB:, LANES:])
            tm_[pl.ds(qrow, DIL_QB), :] = jnp.where(lo, m[:DIL_QB], m[DIL_QB:])
            return carry

        lax.fori_loop(0, dil * nblk, body, 0, unroll=8)
        if gi > 0:
            pa, ps, pm = sets[(gi - 1) % 2]
            step, prev_len = dil // prev_dil, seq // prev_dil
            for r in range(dil):
                rows = pl.ds((r % prev_dil) * prev_len + r // prev_dil, length, stride=step)
                blk = slice(r * length, (r + 1) * length)
                m_old, m_blk = pm[rows, :], tm_[blk, :]
                m_new = jnp.maximum(m_old, m_blk)
                w_old, w_blk = jnp.exp2(m_old - m_new), jnp.exp2(m_blk - m_new)
                ta[blk, :] = pa[rows, :] * w_old + ta[blk, :] * w_blk
                tl[blk, :] = ps[rows, :] * w_old + tl[blk, :] * w_blk
                tm_[blk, :] = m_new
        prev_dil = dil
    fa, fs, _ = sets[(len(groups) - 1) % 2]
    stage = sets[len(groups) % 2][2]
    length = seq // prev_dil
    for r in range(prev_dil):
        blk = slice(r * length, (r + 1) * length)
        stage[pl.ds(r, length, stride=prev_dil), :] = fa[blk, :] / fs[blk, :]
    o_ref[...] = stage[...].astype(BF16)


def _dil_attn(qkvs, b, s):
    in_specs, args = [], []
    for x in qkvs:
        for j in range(3):
            in_specs.append(pl.BlockSpec((None,) + x.shape[1:3] + (LANES,),
                                         lambda bi, hp, j=j: (bi, 0, 0, 4 * j + hp)))
            args.append(x)
    stat = pltpu.VMEM((s, LANES), F32)
    return pl.pallas_call(
        _dil_attn_kernel, grid=(b, 4), in_specs=in_specs,
        out_specs=pl.BlockSpec((s, LANES), lambda bi, hp: (bi, hp)),
        out_shape=jax.ShapeDtypeStruct((b * s, 512), BF16),
        scratch_shapes=[stat] * 6 + [pltpu.VMEM((3, 2 * DIL_QB, DIL_QB + 2 * DIL_HALF), F32)],
        compiler_params=_params(("parallel", "parallel"), 40), name="dil_attn",
    )(*args)


def _l1_router_kernel(o_ref, wo_ref, gmix_ref, h_ref, gpre_ref, wr_ref,
                      h1_ref, a_ref, idx_ref, gate_ref, cnt_ref, carry_ref):
    @pl.when(pl.program_id(0) == 0)
    def _():
        carry_ref[...] = jnp.zeros_like(carry_ref)

    h1 = h_ref[...] + _rms(_dot(o_ref[...], wo_ref[...]), gmix_ref[...])
    h1_ref[...] = h1
    a = _rms(h1, gpre_ref[...]).astype(BF16)
    a_ref[...] = a
    logits = _dot(a, wr_ref[...])
    tm = logits.shape[0]
    lane = _lane_iota(logits.shape)
    lanef = lane.astype(F32)
    lg = jnp.where(lane < N_EXPERTS, logits, -jnp.inf)
    m1 = jnp.max(lg, axis=-1, keepdims=True)
    i1 = jnp.min(jnp.where(lg == m1, lanef, float(LANES)), axis=-1, keepdims=True)
    lg2 = jnp.where(lanef == i1, -jnp.inf, lg)
    m2 = jnp.max(lg2, axis=-1, keepdims=True)
    i2 = jnp.min(jnp.where(lg2 == m2, lanef, float(LANES)), axis=-1, keepdims=True)
    t = jnp.exp(m2 - m1)
    g1 = 1.0 / (1.0 + t)
    g2 = t / (1.0 + t)
    hit1 = lanef == i1
    hit2 = lanef == i2
    onehot = jnp.where(hit1 | hit2, 1.0, 0.0)
    r = lax.broadcasted_iota(jnp.int32, (tm, tm), 0)
    c = lax.broadcasted_iota(jnp.int32, (tm, tm), 1)
    tri = jnp.where(c < r, 1.0, 0.0).astype(BF16)
    before = _dot(tri, onehot.astype(BF16)) + carry_ref[...]
    rank1 = jnp.sum(jnp.where(hit1, before, 0.0), axis=-1, keepdims=True)
    rank2 = jnp.sum(jnp.where(hit2, before, 0.0), axis=-1, keepdims=True)
    info = jnp.where(lane == 0, i1, jnp.where(lane == 1, i2, jnp.where(lane == 2, rank1, rank2)))
    idx_ref[...] = info.astype(jnp.int32)
    gate_ref[...] = jnp.where(lane == 0, g1, g2)
    carry_ref[...] += jnp.sum(onehot, axis=0, keepdims=True)
    cnt_ref[...] = carry_ref[...]


def _l1_router(o, wo, gmix, h, gpre, wr):
    m = h.shape[0]
    row = lambda i: (i, 0)
    const = lambda i: (0, 0)
    full = lambda x: pl.BlockSpec(x.shape, const)
    return pl.pallas_call(
        _l1_router_kernel, grid=(m // TM,),
        in_specs=[pl.BlockSpec((TM, o.shape[1]), row), full(wo), full(gmix),
                  pl.BlockSpec((TM, D_MODEL), row), full(gpre), full(wr)],
        out_specs=[pl.BlockSpec((TM, D_MODEL), row), pl.BlockSpec((TM, D_MODEL), row),
                   pl.BlockSpec((TM, LANES), row), pl.BlockSpec((TM, LANES), row),
                   pl.BlockSpec((1, LANES), const)],
        out_shape=[jax.ShapeDtypeStruct((m, D_MODEL), F32),
                   jax.ShapeDtypeStruct((m, D_MODEL), BF16),
                   jax.ShapeDtypeStruct((m, LANES), jnp.int32),
                   jax.ShapeDtypeStruct((m, LANES), F32),
                   jax.ShapeDtypeStruct((1, LANES), F32)],
        scratch_shapes=[pltpu.VMEM((1, LANES), F32)],
        compiler_params=_params(("arbitrary",), 32), name="l1_router",
    )(o, wo, gmix, h, gpre, wr)


def _expert_kernel(te_ref, tv_ref, x_ref, wg_ref, wu_ref, wd_ref, *refs):
    y_ref, acc_ref = refs[-2:]
    j = pl.program_id(1)
    t = pl.program_id(2)
    last = pl.num_programs(1) - 1
    valid = tv_ref[pl.program_id(0) * MOE_GROUP + t] > 0

    @pl.when(valid & (j == 0))
    def _():
        acc_ref[t] = jnp.zeros(acc_ref.shape[1:], F32)

    @pl.when(valid)
    def _():
        x = x_ref[...]
        gate = _dot(x, wg_ref[...].astype(BF16))
        up = _dot(x, wu_ref[...].astype(BF16))
        acc_ref[t] += _dot((_silu(gate) * up).astype(BF16), wd_ref[...].astype(BF16))

    @pl.when(valid & (j == last))
    def _():
        y_ref[...] = acc_ref[t].astype(BF16)

    @pl.when(jnp.logical_not(valid) & (j == last))
    def _():
        y_ref[...] = jnp.zeros_like(y_ref)


def _experts(tile_e, tile_valid, xs, wg, wu, wd, ys, first_tile, n_slots):
    n_tiles = xs.shape[0] // TM_MOE
    n_j = wg.shape[2] // TF
    te = tile_e[first_tile:first_tile + n_tiles]
    tv = tile_valid[first_tile:first_tile + n_tiles]
    tile = lambda g, t: g * MOE_GROUP + t
    in_specs = [
        pl.BlockSpec((TM_MOE, D_MODEL), lambda g, j, t, te, tv: (tile(g, t), 0)),
        pl.BlockSpec((None, D_MODEL, TF),
                     lambda g, j, t, te, tv: (te[tile(g, t)], 0, j * tv[tile(g, t)])),
        pl.BlockSpec((None, D_MODEL, TF),
                     lambda g, j, t, te, tv: (te[tile(g, t)], 0, j * tv[tile(g, t)])),
        pl.BlockSpec((None, TF, D_MODEL),
                     lambda g, j, t, te, tv: (te[tile(g, t)], j * tv[tile(g, t)], 0))]
    args = [te, tv, xs, wg, wu, wd]
    aliases = {}
    if ys is not None:
        in_specs.append(pl.BlockSpec(memory_space=pl.ANY))
        aliases = {len(args): 0}
        args.append(ys)
    out_spec = pl.BlockSpec(
        (TM_MOE, D_MODEL),
        lambda g, j, t, te, tv: (first_tile + g * MOE_GROUP + jnp.where(j == n_j - 1, t, 0), 0))
    grid_spec = pltpu.PrefetchScalarGridSpec(
        num_scalar_prefetch=2, grid=(n_tiles // MOE_GROUP, n_j, MOE_GROUP), in_specs=in_specs,
        out_specs=out_spec,
        scratch_shapes=[pltpu.VMEM((MOE_GROUP, TM_MOE, D_MODEL), F32)])
    return pl.pallas_call(
        _expert_kernel, grid_spec=grid_spec,
        out_shape=jax.ShapeDtypeStruct((n_slots, D_MODEL), BF16),
        input_output_aliases=aliases,
        compiler_params=_params(("parallel", "arbitrary", "arbitrary"), 56), name="moe_experts",
    )(*args)


def _combine_ple_kernel(y0_ref, y1_ref, gate_ref, g_ref, h_ref, p_ref, ggn_ref, wpg_ref, wpp_ref,
                        gple_ref, out_ref):
    gate = gate_ref[...]
    f = y0_ref[...].astype(F32) * gate[:, 0:1] + y1_ref[...].astype(F32) * gate[:, 1:2]
    h2 = h_ref[...] + _rms(f, g_ref[...])
    out_ref[...] = _ple_update(h2, p_ref[...], ggn_ref[...], wpg_ref[...], wpp_ref[...], gple_ref[...])


def _combine_ple(y01, gate, g, h, p, ggn, wpg, wpp, gple):
    m = h.shape[0]
    nt = m // TM
    row = lambda i: (i, 0)
    const = lambda i: (0, 0)
    full = lambda x: pl.BlockSpec(x.shape, const)
    act = pl.BlockSpec((TM, D_MODEL), row)
    return pl.pallas_call(
        _combine_ple_kernel, grid=(nt,),
        in_specs=[act, pl.BlockSpec((TM, D_MODEL), lambda i: (nt + i, 0)),
                  pl.BlockSpec((TM, LANES), row), full(g), act,
                  pl.BlockSpec((None, TM, PLE_DIM), lambda i: (1, i, 0)), full(ggn), full(wpg),
                  full(wpp), full(gple)],
        out_specs=act,
        out_shape=jax.ShapeDtypeStruct((m, D_MODEL), F32),
        compiler_params=_params(("parallel",), 40), name="combine_ple",
    )(y01, y01, gate, g, h, p, ggn, wpg, wpp, gple)


def _rows(x, idx):
    return x.at[idx].get(mode='promise_in_bounds')


def _moe_plan(idx, cnt, n):
    counts = cnt[0, :N_EXPERTS].astype(jnp.int32)
    padded = (counts + TM_MOE - 1) // TM_MOE * TM_MOE
    end_pad = jnp.cumsum(padded)
    start_pad = end_pad - padded
    experts = jnp.arange(N_EXPERTS, dtype=jnp.int32)
    first = lambda e: jnp.sum(jnp.where(e[:, None] == experts, start_pad, 0), axis=-1)
    dest = (first(idx[:, 0]) + idx[:, 2], first(idx[:, 1]) + idx[:, 3])
    n_slots = n * TOP_K + N_EXPERTS * TM_MOE
    pair = 2 * jnp.arange(n, dtype=jnp.int32)
    keys = jnp.concatenate([idx[:, 0] * (2 * n) + pair, idx[:, 1] * (2 * n) + pair + 1])
    by_expert = jnp.concatenate([(jnp.sort(keys) % (2 * n)) // 2, jnp.zeros((n,), jnp.int32)])
    start_run = jnp.cumsum(counts) - counts
    slot_tok = jnp.zeros((n_slots + n,), jnp.int32)
    for e in range(N_EXPERTS):
        run = lax.dynamic_slice(by_expert, (start_run[e],), (n,))
        slot_tok = lax.dynamic_update_slice(slot_tok, run, (start_pad[e],))
    slot_tok = slot_tok[:n_slots]
    tile_start = jnp.arange(n_slots // TM_MOE, dtype=jnp.int32) * TM_MOE
    tile_e = jnp.minimum(jnp.sum(tile_start[:, None] >= end_pad[None, :], axis=1),
                         N_EXPERTS - 1).astype(jnp.int32)
    tile_valid = (tile_start < end_pad[-1]).astype(jnp.int32)
    return dest, slot_tok, tile_e, tile_valid


def _rotary_tables(pos, dim, theta):
    exponent = jnp.arange(0, dim, 2, dtype=F32) / dim
    inv_freq = jnp.power(jnp.float32(theta), -exponent)
    ang = pos.astype(F32)[:, None] * inv_freq[None, :]
    return jnp.cos(ang), jnp.sin(ang)


def _tables(s):
    t = jnp.arange(s, dtype=jnp.int32)
    cr, sr = _rotary_tables(t // GRID_W, HEAD_DIM // 2, AXIAL_THETA)
    cc, sc = _rotary_tables(t % GRID_W, HEAD_DIM // 2, AXIAL_THETA)
    ca = jnp.tile(jnp.concatenate([cr, cr, cc, cc], axis=1), (1, 2))
    sa = jnp.tile(jnp.concatenate([-sr, sr, -sc, sc], axis=1), (1, 2))
    cm_, sm_ = _rotary_tables(t, MLA_ROPE_DIM, MLA_ROPE_THETA)
    one = jnp.ones((s, 1), F32)
    zero = jnp.zeros((s, 1), F32)
    cm = jnp.concatenate([jnp.tile(one, (1, 64)), cm_, cm_, jnp.tile(one, (1, 32))], axis=1)
    sm = jnp.concatenate([jnp.tile(zero, (1, 64)), -sm_, sm_, jnp.tile(zero, (1, 32))], axis=1)
    ckr = jnp.concatenate([cm_, cm_, jnp.tile(zero, (1, 96))], axis=1)
    skr = jnp.concatenate([-sm_, sm_, jnp.tile(zero, (1, 96))], axis=1)
    cp_, sp_ = _rotary_tables(t, PARTIAL_ROPE_DIM, ROPE_THETA)
    cp = jnp.tile(jnp.concatenate([cp_, cp_, jnp.tile(one, (1, 48))], axis=1), (1, 2))
    sp = jnp.tile(jnp.concatenate([-sp_, sp_, jnp.tile(zero, (1, 48))], axis=1), (1, 2))
    scale = HEAD_DIM ** -0.5 * LOG2_E
    ctab = jnp.stack([cp * scale, cp])
    stab = jnp.stack([sp * scale, sp])
    return ca, sa, cm, sm, ckr, skr, ctab, stab


_GQA_PERM = (0, 4, 1, 5, 2, 6, 3, 7)


def _l0_weights(w_in, w_uq, w_ukv, w_out):
    d = w_in.shape[0]
    q = w_in[:, :512].reshape(d, 8, 64)[:, _GQA_PERM, :].reshape(d, 512)
    win = jnp.concatenate([q, w_in[:, 512:], jnp.zeros((d, AB_IN_PAD - w_in.shape[1]), w_in.dtype)],
                          axis=1).astype(BF16)
    uq = w_uq.reshape(MLA_Q_RANK, MLA_HEADS, MLA_NOPE_DIM + MLA_ROPE_DIM)
    wuq = jnp.pad(uq, ((0, 0), (0, 0), (0, LANES - uq.shape[2]))).reshape(MLA_Q_RANK, -1).astype(BF16)
    ukv = w_ukv.reshape(MLA_KV_RANK, MLA_HEADS, MLA_NOPE_DIM + MLA_V_DIM)
    wk = jnp.pad(ukv[:, :, :MLA_NOPE_DIM], ((0, 0), (0, 0), (0, LANES - MLA_NOPE_DIM)))
    wk = wk.reshape(MLA_KV_RANK, -1).astype(BF16)
    wv = ukv[:, :, MLA_NOPE_DIM:].reshape(MLA_KV_RANK, -1).astype(BF16)
    src = jnp.arange(LANES)[:, None]
    dst = jnp.arange(MLA_HEADS * LANES)[None, :]
    wp = ((src < MLA_ROPE_DIM) & (dst % LANES == src + MLA_NOPE_DIM)).astype(BF16)
    oa = w_out[:512].reshape(8, 64, -1)[_GQA_PERM, :, :].reshape(512, -1)
    wout = jnp.concatenate([oa, w_out[512:]], axis=0).astype(BF16)
    return win, wuq, wk, wp, wv, wout


def kernel(x, p, mix_pre_g, mix_post_g, ffn_pre_g, ffn_post_g, ple_w_proj, ple_gate_norm_g, ple_w_gate, ple_post_g, ab_w_in, gqa_q_norm_g, gqa_k_norm_g, mla_q_norm_g, mla_w_uq, mla_kv_norm_g, mla_w_ukv, ab_w_out, ffn_w_gate, ffn_w_up, ffn_w_down, dil_w_qkv, dil_w_out, moe_w_router, moe_w_gate, moe_w_up, moe_w_down):
    b, s, d = x.shape
    m = b * s
    row = lambda g: g.reshape(1, -1).astype(F32)
    ca, sa, cm, sm, ckr, skr, ctab, stab = _tables(s)
    h = x.reshape(m, d)
    pf = p.reshape(p.shape[0], m, PLE_DIM)

    win, wuq, wk, wp, wv, wout = _l0_weights(ab_w_in[0], mla_w_uq[0], mla_w_ukv[0], ab_w_out[0])
    gq = jnp.tile(row(gqa_q_norm_g[0]), (1, 2))
    gk = jnp.tile(row(gqa_k_norm_g[0]), (1, 2))
    qa, ka, vat, qm, km, vmt = _l0_in(h, row(mix_pre_g[0]), win, gq, gk, ca, sa,
                                      row(mla_q_norm_g[0]), wuq, cm, sm,
                                      row(mla_kv_norm_g[0]), wk, wp, wv, ckr, skr, b, s)
    oa = _gqa_attn(qa, ka, vat, b, s)
    om = _mla_attn(qm, km, vmt, b, s)
    h = _l0_tail(oa, om, wout, row(mix_post_g[0]), h, row(ffn_pre_g[0]),
                 ffn_w_gate[0].astype(BF16), ffn_w_up[0].astype(BF16), ffn_w_down[0].astype(BF16),
                 row(ffn_post_g[0]), pf, row(ple_gate_norm_g[0]), ple_w_gate[0].astype(BF16),
                 ple_w_proj[0].astype(BF16), row(ple_post_g[0]))

    qkvs = _l1_qkv(h, row(mix_pre_g[1]), dil_w_qkv[0].astype(BF16), ctab, stab, b, s)
    o = _dil_attn(qkvs, b, s)
    wr = jnp.pad(moe_w_router[0].astype(BF16), ((0, 0), (0, LANES - N_EXPERTS)))
    h, a, idx, gate, cnt = _l1_router(o, dil_w_out[0].astype(BF16), row(mix_post_g[1]), h,
                                      row(ffn_pre_g[1]), wr)
    dest, slot_tok, tile_e, tile_valid = _moe_plan(idx, cnt, m)
    n_slots = slot_tok.shape[0]
    ys, first_tile = jnp.zeros((n_slots, D_MODEL), BF16), 0
    n_groups = n_slots // (TM_MOE * MOE_GROUP)
    sizes = [n_groups * share // sum(MOE_CHUNKS) for share in MOE_CHUNKS]
    sizes[-1] += n_groups - sum(sizes)
    for n_tiles in [size * MOE_GROUP for size in sizes if size > 0]:
        rows = slice(first_tile * TM_MOE, (first_tile + n_tiles) * TM_MOE)
        ys = _experts(tile_e, tile_valid, _rows(a, slot_tok[rows]), moe_w_gate[0], moe_w_up[0],
                      moe_w_down[0], ys, first_tile, n_slots)
        first_tile += n_tiles
    h = _combine_ple(_rows(ys, jnp.concatenate(dest)), gate, row(ffn_post_g[1]), h, pf,
                     row(ple_gate_norm_g[1]), ple_w_gate[1].astype(BF16), ple_w_proj[1].astype(BF16),
                     row(ple_post_g[1]))
    return h.reshape(b, s, d)
```
